```python
import jax, jax.numpy as jnp
from jax import lax
import numpy as np

D_MODEL = 1024
BATCH = 2
SEQ = 16384
DEPTH = 2

GRID_W = 64
MIX_W = D_MODEL
EPS = 1e-6
D_RNN = MIX_W
RNN_BLOCKS = 4
RNN_BW = D_RNN // RNN_BLOCKS
CONV_A_W = 4
CONV_A_LEFT = 2
LRU_C = 8.0
N_HEADS = 8
N_KV = 2
HEAD_DIM = MIX_W // N_HEADS
GROUP = N_HEADS // N_KV
ROPE_THETA = 10000.0
Q_BLOCK = 128
D_CONV = MIX_W
CONV_C_W = 3
CONV_C_LEFT = 1
N_BRANCH = 3
IN_SIZES = (D_RNN, D_RNN,
            N_HEADS * HEAD_DIM, N_KV * HEAD_DIM, N_KV * HEAD_DIM, N_HEADS * HEAD_DIM,
            D_CONV, D_CONV, D_CONV, D_CONV,
            N_BRANCH * D_MODEL)
N_IN = sum(IN_SIZES)

kernel_name = "hybrid_rglru_gqa_shortconv_encoder"


def rms_norm(x, g):
    xf = x.astype(jnp.float32)
    y = xf * lax.rsqrt(jnp.mean(xf * xf, axis=-1, keepdims=True) + EPS)
    return (y * g.astype(jnp.float32)).astype(x.dtype)


def dwconv(x, w, left):
    K = w.shape[0]
    S = x.shape[1]
    xp = jnp.pad(x, ((0, 0), (left, K - 1 - left), (0, 0)))
    y = xp[:, 0:S] * w[0]
    for k in range(1, K):
        y = y + xp[:, k:k + S] * w[k]
    return y


def _combine(e1, e2):
    a1, b1 = e1
    a2, b2 = e2
    return a1 * a2, a2 * b1 + b2


def rg_lru_dir(x, w_r, b_r, w_i, b_i, lam, reverse):
    B_, S, _ = x.shape
    xb = x.reshape(B_, S, RNN_BLOCKS, RNN_BW)
    r = jax.nn.sigmoid(jnp.einsum('bsnc,ncd->bsnd', xb, w_r).reshape(B_, S, D_RNN) + b_r)
    i = jax.nn.sigmoid(jnp.einsum('bsnc,ncd->bsnd', xb, w_i).reshape(B_, S, D_RNN) + b_i)
    log_a = -LRU_C * r.astype(jnp.float32) * jax.nn.softplus(-lam.astype(jnp.float32))
    a = jnp.exp(log_a)
    u = jnp.sqrt(-jnp.expm1(2.0 * log_a)) * (i * x).astype(jnp.float32)
    _, h = lax.associative_scan(_combine, (a, u), axis=1, reverse=reverse)
    return h


def axial_rope_tables(S):
    rows = S // GRID_W
    row = jnp.repeat(jnp.arange(rows, dtype=jnp.float32), GRID_W)
    col = jnp.tile(jnp.arange(GRID_W, dtype=jnp.float32), rows)
    half = HEAD_DIM // 2
    inv = ROPE_THETA ** (-jnp.arange(0, half, 2, dtype=jnp.float32) / half)
    ang_r = row[:, None] * inv
    ang_c = col[:, None] * inv
    return jnp.cos(ang_r), jnp.sin(ang_r), jnp.cos(ang_c), jnp.sin(ang_c)


def rope_1d(x, cos, sin):
    f = x.shape[-1] // 2
    x1, x2 = x[..., :f], x[..., f:]
    cs = cos[None, :, None, :]
    sn = sin[None, :, None, :]
    return jnp.concatenate([x1 * cs - x2 * sn, x2 * cs + x1 * sn], axis=-1)


def apply_axial_rope(x, tabs):
    cos_r, sin_r, cos_c, sin_c = tabs
    half = HEAD_DIM // 2
    xf = x.astype(jnp.float32)
    y = jnp.concatenate([rope_1d(xf[..., :half], cos_r, sin_r),
                         rope_1d(xf[..., half:], cos_c, sin_c)], axis=-1)
    return y.astype(x.dtype)


def gqa_attention(q, k, v, q_g, k_g, tabs):
    B_, S, _ = q.shape
    q = q.reshape(B_, S, N_HEADS, HEAD_DIM)
    k = k.reshape(B_, S, N_KV, HEAD_DIM)
    v = v.reshape(B_, S, N_KV, HEAD_DIM)
    q = apply_axial_rope(rms_norm(q, q_g), tabs) * (HEAD_DIM ** -0.5)
    k = apply_axial_rope(rms_norm(k, k_g), tabs)
    qb = q.reshape(B_, S // Q_BLOCK, Q_BLOCK, N_KV, GROUP, HEAD_DIM).transpose(1, 0, 2, 3, 4, 5)

    def block(qi):
        s = jnp.einsum('bqkgd,bskd->bkgqs', qi, k).astype(jnp.float32)
        p = jax.nn.softmax(s, axis=-1).astype(v.dtype)
        return jnp.einsum('bkgqs,bskd->bqkgd', p, v)

    o = lax.map(block, qb)
    return o.transpose(1, 0, 2, 3, 4, 5).reshape(B_, S, N_HEADS * HEAD_DIM)


def setup_inputs(seed: int = 0) -> dict:
    key = jax.random.key(seed)
    ks = jax.random.split(key, 20)
    f32 = jnp.float32
    nrm = lambda k, shp, s: jax.random.normal(k, shp, f32) * s
    a8 = jax.random.uniform(ks[12], (DEPTH, 2, D_RNN), f32, minval=0.9, maxval=0.999)
    a = a8 ** (1.0 / LRU_C)
    lam = jnp.log(a) - jnp.log1p(-a)
    return {
        "x": nrm(ks[0], (BATCH, SEQ, D_MODEL), 1.0),
        "c": nrm(ks[1], (BATCH, D_MODEL), 1.0),
        "w_ada": nrm(ks[2], (DEPTH, D_MODEL, 3 * D_MODEL), 0.5 * D_MODEL ** -0.5),
        "b_ada": nrm(ks[3], (DEPTH, 3 * D_MODEL), 0.01),
        "norm_g": 1.0 + nrm(ks[4], (DEPTH, D_MODEL), 0.02),
        "w_in": nrm(ks[5], (DEPTH, D_MODEL, N_IN), D_MODEL ** -0.5),
        "conv_a_w": nrm(ks[6], (DEPTH, CONV_A_W, D_RNN), CONV_A_W ** -0.5),
        "conv_a_b": nrm(ks[7], (DEPTH, D_RNN), 0.01),
        "w_rg": nrm(ks[8], (DEPTH, 2, RNN_BLOCKS, RNN_BW, RNN_BW), RNN_BW ** -0.5),
        "b_rg": nrm(ks[9], (DEPTH, 2, D_RNN), 0.01),
        "w_ig": nrm(ks[10], (DEPTH, 2, RNN_BLOCKS, RNN_BW, RNN_BW), RNN_BW ** -0.5),
        "b_ig": nrm(ks[11], (DEPTH, 2, D_RNN), 0.01),
        "lru_lam": lam,
        "q_norm_g": 1.0 + nrm(ks[13], (DEPTH, HEAD_DIM), 0.02),
        "k_norm_g": 1.0 + nrm(ks[14], (DEPTH, HEAD_DIM), 0.02),
        "conv_c_w": nrm(ks[15], (DEPTH, CONV_C_W, D_CONV), CONV_C_W ** -0.5),
        "w_branch": nrm(ks[16], (DEPTH, N_BRANCH, MIX_W, D_MODEL), MIX_W ** -0.5),
        "w_out": nrm(ks[17], (DEPTH, D_MODEL, D_MODEL), D_MODEL ** -0.5),
    }


def reference(x, c, w_ada, b_ada, norm_g, w_in, conv_a_w, conv_a_b, w_rg, b_rg, w_ig, b_ig,
              lru_lam, q_norm_g, k_norm_g, conv_c_w, w_branch, w_out):
    B_, S, D = x.shape
    tabs = axial_rope_tables(S)
    offsets = [int(o) for o in np.cumsum(IN_SIZES)[:-1]]
    for l in range(DEPTH):
        mod = c @ w_ada[l] + b_ada[l]
        shift, scale, gate = jnp.split(mod, 3, axis=-1)
        h = rms_norm(x, norm_g[l]) * (1.0 + scale[:, None, :]) + shift[:, None, :]
        proj = h @ w_in[l]
        (xa, ga, q, k, v, gb, xc, bc, cc, gc, mg) = jnp.split(proj, offsets, axis=-1)

        ua = dwconv(xa, conv_a_w[l], CONV_A_LEFT) + conv_a_b[l]
        hf = rg_lru_dir(ua, w_rg[l, 0], b_rg[l, 0], w_ig[l, 0], b_ig[l, 0], lru_lam[l, 0], False)
        hb = rg_lru_dir(ua, w_rg[l, 1], b_rg[l, 1], w_ig[l, 1], b_ig[l, 1], lru_lam[l, 1], True)
        ya = (hf + hb).astype(x.dtype) * jax.nn.silu(ga)

        yb = gqa_attention(q, k, v, q_norm_g[l], k_norm_g[l], tabs) * jax.nn.silu(gb)

        yc = bc * dwconv(cc * xc, conv_c_w[l], CONV_C_LEFT) * jax.nn.silu(gc)

        br = jnp.stack([ya, yb, yc], axis=2)
        pb = jnp.einsum('bskw,kwd->bskd', br, w_branch[l])
        gates = jax.nn.sigmoid(mg).reshape(B_, S, N_BRANCH, D)
        merged = jnp.einsum('bskd,bskd->bsd', gates, pb)
        x = x + gate[:, None, :] * (merged @ w_out[l])
    return x
```

```python
import functools
import math

import numpy as np
import jax
import jax.numpy as jnp
from jax import lax
from jax.experimental import pallas as pl
from jax.experimental.pallas import tpu as pltpu

D_MODEL = 1024
EPS = 1e-6
GRID_W = 64
N_HEADS = 8
N_KV = 2
GROUP = N_HEADS // N_KV
HEAD_DIM = 128
RNN_BLOCKS = 4
RNN_BW = D_MODEL // RNN_BLOCKS
LRU_C = 8.0
ROPE_THETA = 10000.0
N_BRANCH = 3
LOG2E = 1.4426950408889634

COL_XA, COL_GA, COL_Q, COL_GB, COL_XC, COL_BC, COL_CC, COL_GC, COL_MG = 0, 1, 2, 3, 4, 5, 6, 7, 8
COL_KV = 11
N_IN = 11 * D_MODEL + 2 * N_KV * HEAD_DIM
KV_W = N_KV * HEAD_DIM

HALO = 16
VMEM_LIMIT = 56 * 1024 * 1024

F32 = jnp.float32
BF16 = jnp.bfloat16


def _sigmoid(x):
    return 1.0 / (1.0 + jnp.exp(-x))


def _silu(x):
    return x * _sigmoid(x)


def _cparams(sem):
    return pltpu.CompilerParams(dimension_semantics=sem, vmem_limit_bytes=VMEM_LIMIT)


def _mod_kernel(c_ref, w_ref, b_ref, o_ref):
    acc = jnp.dot(c_ref[...], w_ref[0], preferred_element_type=F32, precision=lax.Precision.HIGHEST)
    o_ref[0] = acc + b_ref[0]


def _modulation(c, w_ada, b_ada):
    depth, d, n = w_ada.shape
    bsz = c.shape[0]
    rows = 8
    c8 = jnp.zeros((rows, d), F32).at[:bsz].set(c)
    tn = 1024
    out = pl.pallas_call(
        _mod_kernel,
        grid=(depth, n // tn),
        in_specs=[
            pl.BlockSpec((rows, d), lambda l, j: (0, 0)),
            pl.BlockSpec((1, d, tn), lambda l, j: (l, 0, j)),
            pl.BlockSpec((1, 1, tn), lambda l, j: (l, 0, j)),
        ],
        out_specs=pl.BlockSpec((1, rows, tn), lambda l, j: (l, 0, j)),
        out_shape=jax.ShapeDtypeStruct((depth, rows, n), F32),
        compiler_params=_cparams(("parallel", "parallel")),
        name="adaln_mod",
    )(c8, w_ada, b_ada.reshape(depth, 1, n))
    return out[:, :bsz].reshape(depth, bsz, 3, d)


def _inproj_kernel(x_ref, mod_ref, g_ref, w_ref, o_ref, h_ref):
    @pl.when(pl.program_id(2) == 0)
    def _():
        xv = x_ref[0]
        ms = jnp.mean(xv * xv, axis=-1, keepdims=True)
        y = xv * lax.rsqrt(ms + EPS) * g_ref[...]
        shift = mod_ref[0, 0:1, :]
        scale = mod_ref[0, 1:2, :]
        h_ref[...] = (y * (1.0 + scale) + shift).astype(BF16)

    o_ref[0] = jnp.dot(h_ref[...], w_ref[...], preferred_element_type=F32).astype(BF16)


def _in_projection(x, mod, norm_g, w_in_p):
    bsz, s, d = x.shape
    tm = min(2048, s)
    tn = 512
    return pl.pallas_call(
        _inproj_kernel,
        grid=(bsz, s // tm, N_IN // tn),
        in_specs=[
            pl.BlockSpec((1, tm, d), lambda b, i, j: (b, i, 0)),
            pl.BlockSpec((1, 3, d), lambda b, i, j: (b, 0, 0)),
            pl.BlockSpec((1, d), lambda b, i, j: (0, 0)),
            pl.BlockSpec((d, tn), lambda b, i, j: (0, j)),
        ],
        out_specs=pl.BlockSpec((1, tm, tn), lambda b, i, j: (b, i, j)),
        out_shape=jax.ShapeDtypeStruct((bsz, s, N_IN), BF16),
        scratch_shapes=[pltpu.VMEM((tm, d), BF16)],
        compiler_params=_cparams(("parallel", "parallel", "arbitrary")),
        name="norm_inproj",
    )(x, mod, norm_g.reshape(1, d), w_in_p)


def _rope_tables(s):
    half = HEAD_DIM // 2
    inv = ROPE_THETA ** (-np.arange(0, half, 2, dtype=np.float64) / half)
    rows = s // GRID_W
    ang_r = np.arange(rows, dtype=np.float64)[:, None] * inv
    ang_c = np.arange(GRID_W, dtype=np.float64)[:, None] * inv
    cr, sr = np.cos(ang_r), np.sin(ang_r)
    cc, sc = np.cos(ang_c), np.sin(ang_c)
    cos_r = jnp.asarray(np.concatenate([cr, cr], -1), F32)
    sin_r = jnp.asarray(np.concatenate([-sr, sr], -1), F32)
    cos_c = jnp.asarray(np.concatenate([cc, cc], -1), F32)
    sin_c = jnp.asarray(np.concatenate([-sc, sc], -1), F32)
    cos_t = jnp.concatenate([jnp.repeat(cos_r, GRID_W, axis=0), jnp.tile(cos_c, (rows, 1))], -1)
    sin_t = jnp.concatenate([jnp.repeat(sin_r, GRID_W, axis=0), jnp.tile(sin_c, (rows, 1))], -1)
    return cos_t, sin_t


def _norm_rope(xh, g, cos_t, sin_t, lane_lo):
    ms = jnp.mean(xh * xh, axis=-1, keepdims=True)
    y = xh * lax.rsqrt(ms + EPS) * g
    partner = jnp.where(lane_lo, pltpu.roll(y, HEAD_DIM - 32, axis=1), pltpu.roll(y, 32, axis=1))
    return y * cos_t + partner * sin_t


def _prep_kernel(q_ref, kv_ref, cos_ref, sin_ref, qg_ref, kg_ref, qt_ref, k_ref, vt_ref):
    cos_t = cos_ref[...]
    sin_t = sin_ref[...]
    lane = lax.broadcasted_iota(jnp.int32, cos_t.shape, 1)
    lane_lo = (lane % 64) < 32
    q_scale = (HEAD_DIM ** -0.5) * LOG2E
    for h in range(N_HEADS):
        qh = q_ref[0, :, h * HEAD_DIM:(h + 1) * HEAD_DIM].astype(F32)
        qh = _norm_rope(qh, qg_ref[...], cos_t, sin_t, lane_lo) * q_scale
        qt_ref[0, h] = qh.T.astype(BF16)
    for h in range(N_KV):
        kh = kv_ref[0, :, h * HEAD_DIM:(h + 1) * HEAD_DIM].astype(F32)
        k_ref[0, h, 0] = _norm_rope(kh, kg_ref[...], cos_t, sin_t, lane_lo).astype(BF16)
        vh = kv_ref[0, :, KV_W + h * HEAD_DIM:KV_W + (h + 1) * HEAD_DIM].astype(F32)
        vt_ref[0, h, 0] = vh.T.astype(BF16)


def _attn_prep(proj, cos_t, sin_t, q_g, k_g, tk):
    bsz, s, _ = proj.shape
    nkb = s // tk
    return pl.pallas_call(
        _prep_kernel,
        grid=(bsz, nkb),
        in_specs=[
            pl.BlockSpec((1, tk, D_MODEL), lambda b, i: (b, i, COL_Q)),
            pl.BlockSpec((1, tk, 2 * KV_W), lambda b, i: (b, i, COL_KV * D_MODEL // (2 * KV_W))),
            pl.BlockSpec((tk, HEAD_DIM), lambda b, i: (i, 0)),
            pl.BlockSpec((tk, HEAD_DIM), lambda b, i: (i, 0)),
            pl.BlockSpec((1, HEAD_DIM), lambda b, i: (0, 0)),
            pl.BlockSpec((1, HEAD_DIM), lambda b, i: (0, 0)),
        ],
        out_specs=[
            pl.BlockSpec((1, N_HEADS, HEAD_DIM, tk), lambda b, i: (b, 0, 0, i)),
            pl.BlockSpec((1, N_KV, 1, tk, HEAD_DIM), lambda b, i: (b, 0, i, 0, 0)),
            pl.BlockSpec((1, N_KV, 1, HEAD_DIM, tk), lambda b, i: (b, 0, i, 0, 0)),
        ],
        out_shape=[
            jax.ShapeDtypeStruct((bsz, N_HEADS, HEAD_DIM, s), BF16),
            jax.ShapeDtypeStruct((bsz, N_KV, nkb, tk, HEAD_DIM), BF16),
            jax.ShapeDtypeStruct((bsz, N_KV, nkb, HEAD_DIM, tk), BF16),
        ],
        compiler_params=_cparams(("parallel", "parallel")),
        name="attn_prep",
    )(proj, proj, cos_t, sin_t, q_g.reshape(1, HEAD_DIM), k_g.reshape(1, HEAD_DIM))


def _attn_kernel(qt_ref, k_ref, vt_ref, gb_ref, o_ref, m_ref, l_ref, acc_ref, *, nkb):
    m_ref[...] = jnp.full(m_ref.shape, -jnp.inf, F32)
    l_ref[...] = jnp.zeros(l_ref.shape, F32)
    acc_ref[...] = jnp.zeros(acc_ref.shape, F32)

    def kv_step(j, carry):
        kblk = k_ref[0, 0, j]
        vblk = vt_ref[0, 0, j]
        for h in range(GROUP):
            s = jnp.dot(kblk, qt_ref[0, h], preferred_element_type=F32)
            m_old = m_ref[h]
            m_new = jnp.maximum(m_old, jnp.max(s, axis=0, keepdims=True))
            alpha = jnp.exp2(m_old - m_new)
            p = jnp.exp2(s - m_new)
            l_ref[h] = alpha * l_ref[h] + jnp.sum(p, axis=0, keepdims=True)
            acc_ref[h] = alpha * acc_ref[h] + jnp.dot(vblk, p.astype(BF16), preferred_element_type=F32)
            m_ref[h] = m_new
        return carry

    lax.fori_loop(0, nkb, kv_step, 0)

    for h in range(GROUP):
        o = acc_ref[h] * (1.0 / l_ref[h])
        gate = _silu(gb_ref[0, :, h * HEAD_DIM:(h + 1) * HEAD_DIM].astype(F32))
        o_ref[0, :, h * HEAD_DIM:(h + 1) * HEAD_DIM] = (o.T * gate).astype(BF16)


def _attention(qt, kk, vt, proj, tq):
    bsz, _, _, s = qt.shape
    nkb, tk = kk.shape[2], kk.shape[3]
    gw = GROUP * HEAD_DIM
    return pl.pallas_call(
        functools.partial(_attn_kernel, nkb=nkb),
        grid=(bsz, N_KV, s // tq),
        in_specs=[
            pl.BlockSpec((1, GROUP, HEAD_DIM, tq), lambda b, g, i: (b, g, 0, i)),
            pl.BlockSpec((1, 1, nkb, tk, HEAD_DIM), lambda b, g, i: (b, g, 0, 0, 0)),
            pl.BlockSpec((1, 1, nkb, HEAD_DIM, tk), lambda b, g, i: (b, g, 0, 0, 0)),
            pl.BlockSpec((1, tq, gw), lambda b, g, i: (b, i, COL_GB * D_MODEL // gw + g)),
        ],
        out_specs=pl.BlockSpec((1, tq, gw), lambda b, g, i: (b, i, g)),
        out_shape=jax.ShapeDtypeStruct((bsz, s, D_MODEL), BF16),
        scratch_shapes=[
            pltpu.VMEM((GROUP, 1, tq), F32),
            pltpu.VMEM((GROUP, 1, tq), F32),
            pltpu.VMEM((GROUP, HEAD_DIM, tq), F32),
        ],
        compiler_params=_cparams(("parallel", "parallel", "arbitrary")),
        name="gqa_attention",
    )(qt, kk, vt, proj)


def _shift_rows(tile, halo, shift):
    t = tile.shape[0]
    row8 = lax.broadcasted_iota(jnp.int32, (8, tile.shape[1]), 0)
    if shift > 0:
        rolled = pltpu.roll(tile, shift, axis=0)
        edge = pltpu.roll(halo, shift, axis=0)[0:8]
        fixed = jnp.where(row8 < shift, edge, rolled[0:8])
        return jnp.concatenate([fixed, rolled[8:]], axis=0)
    k = -shift
    rolled = pltpu.roll(tile, t - k, axis=0)
    edge = pltpu.roll(halo, HALO - k, axis=0)[HALO - 8:HALO]
    fixed = jnp.where(row8 >= 8 - k, edge, rolled[t - 8:t])
    return jnp.concatenate([rolled[:t - 8], fixed], axis=0)


def _lru_inputs(x_ref, prev_ref, next_ref, first, last, cw_ref, cb_ref, wr_ref, br_ref, wi_ref, bi_ref,
                lam_ref, d):
    tile = x_ref[0].astype(F32)
    prev = jnp.where(first, 0.0, prev_ref[0].astype(F32))
    nxt = jnp.where(last, 0.0, next_ref[0].astype(F32))
    ua = (cw_ref[0:1, :] * _shift_rows(tile, prev, 2)
          + cw_ref[1:2, :] * _shift_rows(tile, prev, 1)
          + cw_ref[2:3, :] * tile
          + cw_ref[3:4, :] * _shift_rows(tile, nxt, -1)
          + cb_ref[...])
    ub = ua.astype(BF16)
    r_parts, i_parts = [], []
    for n in range(RNN_BLOCKS):
        blk = ub[:, n * RNN_BW:(n + 1) * RNN_BW]
        r_parts.append(jnp.dot(blk, wr_ref[d, n], preferred_element_type=F32))
        i_parts.append(jnp.dot(blk, wi_ref[d, n], preferred_element_type=F32))
    r = _sigmoid(jnp.concatenate(r_parts, axis=-1) + br_ref[d:d + 1, :])
    ig = _sigmoid(jnp.concatenate(i_parts, axis=-1) + bi_ref[d:d + 1, :])
    nl = -lam_ref[d:d + 1, :]
    softplus = jnp.maximum(nl, 0.0) + jnp.log1p(jnp.exp(-jnp.abs(nl)))
    log_a = (-LRU_C) * r * softplus
    a = jnp.exp(log_a)
    u = jnp.sqrt(-jnp.tanh(log_a) * (1.0 + a * a)) * (ig * ua)
    return a, u


def _block_scan(a, u, reverse):
    t, c = a.shape
    a3 = a.reshape(t // 8, 8, c)
    u3 = u.reshape(t // 8, 8, c)
    pos = lax.broadcasted_iota(jnp.int32, a3.shape, 1)
    for dist in (1, 2, 4):
        if reverse:
            sh = 8 - dist
            valid = pos < 8 - dist
        else:
            sh = dist
            valid = pos >= dist
        a_sh = jnp.where(valid, pltpu.roll(a3, sh, axis=1), 1.0)
        u_sh = jnp.where(valid, pltpu.roll(u3, sh, axis=1), 0.0)
        u3 = a3 * u_sh + u3
        a3 = a3 * a_sh
    return a3, u3


def _lru_kernel(xf_ref, pf_ref, nf_ref, xb_ref, pb_ref, nb_ref, cw_ref, cb_ref, wr_ref, br_ref, wi_ref,
                bi_ref, lam_ref, hf_ref, hb_ref, cf_ref, cbk_ref, af_ref, uf_ref, *, nt):
    i = pl.program_id(1)

    @pl.when(i == 0)
    def _():
        cf_ref[...] = jnp.zeros(cf_ref.shape, F32)
        cbk_ref[...] = jnp.zeros(cbk_ref.shape, F32)

    nblk = af_ref.shape[0]

    a, u = _lru_inputs(xf_ref, pf_ref, nf_ref, i == 0, i == nt - 1, cw_ref, cb_ref, wr_ref, br_ref, wi_ref,
                       bi_ref, lam_ref, 0)
    a3, u3 = _block_scan(a, u, reverse=False)
    af_ref[...] = a3.reshape(af_ref.shape)
    uf_ref[...] = u3.reshape(uf_ref.shape)

    def fwd_step(k, carry):
        ak = af_ref[k]
        uk = uf_ref[k]
        h0 = ak[0:8] * carry + uk[0:8]
        h1 = ak[8:16] * jnp.broadcast_to(h0[7:8, :], h0.shape) + uk[8:16]
        hf_ref[0, pl.ds(pl.multiple_of(k * 16, 16), 16), :] = jnp.concatenate([h0, h1], 0).astype(hf_ref.dtype)
        return jnp.broadcast_to(h1[7:8, :], h1.shape)

    cf_ref[...] = lax.fori_loop(0, nblk, fwd_step, cf_ref[...])

    a, u = _lru_inputs(xb_ref, pb_ref, nb_ref, i == nt - 1, i == 0, cw_ref, cb_ref, wr_ref, br_ref, wi_ref,
                       bi_ref, lam_ref, 1)
    a3, u3 = _block_scan(a, u, reverse=True)
    af_ref[...] = a3.reshape(af_ref.shape)
    uf_ref[...] = u3.reshape(uf_ref.shape)

    def bwd_step(k, carry):
        kk = nblk - 1 - k
        ak = af_ref[kk]
        uk = uf_ref[kk]
        h1 = ak[8:16] * carry + uk[8:16]
        h0 = ak[0:8] * jnp.broadcast_to(h1[0:1, :], h1.shape) + uk[0:8]
        hb_ref[0, pl.ds(pl.multiple_of(kk * 16, 16), 16), :] = jnp.concatenate([h0, h1], 0).astype(hb_ref.dtype)
        return jnp.broadcast_to(h0[0:1, :], h0.shape)

    cbk_ref[...] = lax.fori_loop(0, nblk, bwd_step, cbk_ref[...])


def _rg_lru(proj, conv_w, conv_b, w_rg, b_rg, w_ig, b_ig, lam):
    bsz, s, _ = proj.shape
    d = D_MODEL
    tm = min(256, s)
    nt = s // tm
    hb = tm // HALO
    nhalo = s // HALO

    def tile_specs(tile_of):
        return [
            pl.BlockSpec((1, tm, d), lambda b, i: (b, tile_of(i), COL_XA)),
            pl.BlockSpec((1, HALO, d), lambda b, i: (b, jnp.maximum(tile_of(i) * hb - 1, 0), COL_XA)),
            pl.BlockSpec((1, HALO, d), lambda b, i: (b, jnp.minimum((tile_of(i) + 1) * hb, nhalo - 1), COL_XA)),
        ]

    const2 = lambda b, i: (0, 0)
    const4 = lambda b, i: (0, 0, 0, 0)
    out_sds = jax.ShapeDtypeStruct((bsz, s, d), BF16)
    return pl.pallas_call(
        functools.partial(_lru_kernel, nt=nt),
        grid=(bsz, nt),
        in_specs=tile_specs(lambda i: i) + tile_specs(lambda i: nt - 1 - i) + [
            pl.BlockSpec(conv_w.shape, const2),
            pl.BlockSpec((1, d), const2),
            pl.BlockSpec(w_rg.shape, const4),
            pl.BlockSpec(b_rg.shape, const2),
            pl.BlockSpec(w_ig.shape, const4),
            pl.BlockSpec(b_ig.shape, const2),
            pl.BlockSpec(lam.shape, const2),
        ],
        out_specs=[
            pl.BlockSpec((1, tm, d), lambda b, i: (b, i, 0)),
            pl.BlockSpec((1, tm, d), lambda b, i: (b, nt - 1 - i, 0)),
        ],
        out_shape=[out_sds, out_sds],
        scratch_shapes=[
            pltpu.VMEM((8, d), F32),
            pltpu.VMEM((8, d), F32),
            pltpu.VMEM((tm // 16, 16, d), F32),
            pltpu.VMEM((tm // 16, 16, d), F32),
        ],
        compiler_params=_cparams(("parallel", "arbitrary")),
        name="rg_lru",
    )(proj, proj, proj, proj, proj, proj, conv_w, conv_b.reshape(1, d), w_rg, b_rg, w_ig, b_ig, lam)


def _merge_kernel(hf_ref, hb_ref, ga_ref, yb_ref, xc_ref, bc_ref, cc_ref, gc_ref, xcp_ref, ccp_ref, xcn_ref,
                  ccn_ref, mg0_ref, mg1_ref, mg2_ref, x_ref, mod_ref, cw_ref, wb_ref, wo_ref, o_ref, *, nt):
    i = pl.program_id(1)
    ya = (hf_ref[0].astype(F32) + hb_ref[0].astype(F32)) * _silu(ga_ref[0].astype(F32))

    z = cc_ref[0].astype(F32) * xc_ref[0].astype(F32)
    z_prev = jnp.where(i == 0, 0.0, ccp_ref[0].astype(F32) * xcp_ref[0].astype(F32))
    z_next = jnp.where(i == nt - 1, 0.0, ccn_ref[0].astype(F32) * xcn_ref[0].astype(F32))
    conv = (cw_ref[0:1, :] * _shift_rows(z, z_prev, 1) + cw_ref[1:2, :] * z
            + cw_ref[2:3, :] * _shift_rows(z, z_next, -1))
    yc = bc_ref[0].astype(F32) * conv * _silu(gc_ref[0].astype(F32))

    merged = _sigmoid(mg0_ref[0].astype(F32)) * jnp.dot(ya.astype(BF16), wb_ref[0], preferred_element_type=F32)
    merged += _sigmoid(mg1_ref[0].astype(F32)) * jnp.dot(yb_ref[0], wb_ref[1], preferred_element_type=F32)
    merged += _sigmoid(mg2_ref[0].astype(F32)) * jnp.dot(yc.astype(BF16), wb_ref[2], preferred_element_type=F32)
    out = jnp.dot(merged.astype(BF16), wo_ref[...], preferred_element_type=F32)
    o_ref[0] = x_ref[0] + mod_ref[0, 2:3, :] * out


def _merge(hf, hb, yb, proj, x, mod, conv_c_w, w_branch, w_out):
    bsz, s, d = x.shape
    tm = min(256, s)
    nt = s // tm
    hbk = tm // HALO
    nhalo = s // HALO

    def col(c):
        return pl.BlockSpec((1, tm, d), lambda b, i: (b, i, c))

    def halo_prev(c):
        return pl.BlockSpec((1, HALO, d), lambda b, i: (b, jnp.maximum(i * hbk - 1, 0), c))

    def halo_next(c):
        return pl.BlockSpec((1, HALO, d), lambda b, i: (b, jnp.minimum((i + 1) * hbk, nhalo - 1), c))

    return pl.pallas_call(
        functools.partial(_merge_kernel, nt=nt),
        grid=(bsz, nt),
        in_specs=[
            col(0), col(0), col(COL_GA), col(0), col(COL_XC), col(COL_BC), col(COL_CC), col(COL_GC),
            halo_prev(COL_XC), halo_prev(COL_CC), halo_next(COL_XC), halo_next(COL_CC),
            col(COL_MG), col(COL_MG + 1), col(COL_MG + 2),
            col(0),
            pl.BlockSpec((1, 3, d), lambda b, i: (b, 0, 0)),
            pl.BlockSpec(conv_c_w.shape, lambda b, i: (0, 0)),
            pl.BlockSpec(w_branch.shape, lambda b, i: (0, 0, 0)),
            pl.BlockSpec(w_out.shape, lambda b, i: (0, 0)),
        ],
        out_specs=pl.BlockSpec((1, tm, d), lambda b, i: (b, i, 0)),
        out_shape=jax.ShapeDtypeStruct((bsz, s, d), F32),
        compiler_params=_cparams(("parallel", "parallel")),
        name="merge_outproj",
    )(hf, hb, proj, yb, proj, proj, proj, proj, proj, proj, proj, proj, proj, proj, proj, x, mod,
      conv_c_w, w_branch, w_out)


def _permute_w_in(w):
    d = D_MODEL
    o_q = 2 * d
    o_k = o_q + d
    o_v = o_k + KV_W
    o_gb = o_v + KV_W
    o_xc = o_gb + d
    return jnp.concatenate([w[:, :o_q], w[:, o_q:o_k], w[:, o_gb:o_xc], w[:, o_xc:], w[:, o_k:o_gb]], axis=1)


def _attn_tiles(s):
    tq = min(256, s)
    tk = min(512, s)
    return tq, tk


def kernel(x, c, w_ada, b_ada, norm_g, w_in, conv_a_w, conv_a_b, w_rg, b_rg, w_ig, b_ig, lru_lam, q_norm_g,
           k_norm_g, conv_c_w, w_branch, w_out):
    bsz, s, d = x.shape
    depth = w_ada.shape[0]
    tq, tk = _attn_tiles(s)
    cos_t, sin_t = _rope_tables(s)
    mods = _modulation(c, w_ada, b_ada)
    for l in range(depth):
        mod = mods[l]
        proj = _in_projection(x, mod, norm_g[l], _permute_w_in(w_in[l]).astype(BF16))
        hf, hb = _rg_lru(proj, conv_a_w[l], conv_a_b[l], w_rg[l].astype(BF16), b_rg[l], w_ig[l].astype(BF16),
                         b_ig[l], lru_lam[l])
        qt, kk, vt = _attn_prep(proj, cos_t, sin_t, q_norm_g[l], k_norm_g[l], tk)
        yb = _attention(qt, kk, vt, proj, tq)
        x = _merge(hf, hb, yb, proj, x, mod, conv_c_w[l], w_branch[l].astype(BF16), w_out[l].astype(BF16))
    return x
```

```python
import functools
import math

import numpy as np
import jax
import jax.numpy as jnp
from jax import lax
from jax.experimental import pallas as pl
from jax.experimental.pallas import tpu as pltpu

D_MODEL = 1024
EPS = 1e-6
GRID_W = 64
N_HEADS = 8
N_KV = 2
GROUP = N_HEADS // N_KV
HEAD_DIM = 128
RNN_BLOCKS = 4
RNN_BW = D_MODEL // RNN_BLOCKS
LRU_C = 8.0
ROPE_THETA = 10000.0
N_BRANCH = 3
LOG2E = 1.4426950408889634

COL_XA, COL_GA, COL_Q, COL_GB, COL_XC, COL_BC, COL_CC, COL_GC, COL_MG = 0, 1, 2, 3, 4, 5, 6, 7, 8
COL_KV = 11
N_IN = 11 * D_MODEL + 2 * N_KV * HEAD_DIM
KV_W = N_KV * HEAD_DIM

HALO = 16
VMEM_LIMIT = 56 * 1024 * 1024

F32 = jnp.float32
BF16 = jnp.bfloat16


def _sigmoid(x):
    return 1.0 / (1.0 + jnp.exp(-x))


def _silu(x):
    return x * _sigmoid(x)


def _cparams(sem):
    return pltpu.CompilerParams(dimension_semantics=sem, vmem_limit_bytes=VMEM_LIMIT)


def _mod_kernel(c_ref, w_ref, b_ref, o_ref):
    acc = jnp.dot(c_ref[...], w_ref[0], preferred_element_type=F32, precision=lax.Precision.HIGHEST)
    o_ref[0] = acc + b_ref[0]


def _modulation(c, w_ada, b_ada):
    depth, d, n = w_ada.shape
    bsz = c.shape[0]
    rows = 8
    c8 = jnp.zeros((rows, d), F32).at[:bsz].set(c)
    tn = 1024
    out = pl.pallas_call(
        _mod_kernel,
        grid=(depth, n // tn),
        in_specs=[
            pl.BlockSpec((rows, d), lambda l, j: (0, 0)),
            pl.BlockSpec((1, d, tn), lambda l, j: (l, 0, j)),
            pl.BlockSpec((1, 1, tn), lambda l, j: (l, 0, j)),
        ],
        out_specs=pl.BlockSpec((1, rows, tn), lambda l, j: (l, 0, j)),
        out_shape=jax.ShapeDtypeStruct((depth, rows, n), F32),
        compiler_params=_cparams(("parallel", "parallel")),
        name="adaln_mod",
    )(c8, w_ada, b_ada.reshape(depth, 1, n))
    return out[:, :bsz].reshape(depth, bsz, 3, d)


def _inproj_kernel(x_ref, mod_ref, g_ref, w_ref, o_ref, h_ref):
    @pl.when(pl.program_id(2) == 0)
    def _():
        xv = x_ref[0]
        ms = jnp.mean(xv * xv, axis=-1, keepdims=True)
        y = xv * lax.rsqrt(ms + EPS) * g_ref[...]
        shift = mod_ref[0, 0:1, :]
        scale = mod_ref[0, 1:2, :]
        h_ref[...] = (y * (1.0 + scale) + shift).astype(BF16)

    o_ref[0] = jnp.dot(h_ref[...], w_ref[...], preferred_element_type=F32).astype(BF16)


def _in_projection(x, mod, norm_g, w_in_p):
    bsz, s, d = x.shape
    tm = min(2048, s)
    tn = 512
    return pl.pallas_call(
        _inproj_kernel,
        grid=(bsz, s // tm, N_IN // tn),
        in_specs=[
            pl.BlockSpec((1, tm, d), lambda b, i, j: (b, i, 0)),
            pl.BlockSpec((1, 3, d), lambda b, i, j: (b, 0, 0)),
            pl.BlockSpec((1, d), lambda b, i, j: (0, 0)),
            pl.BlockSpec((d, tn), lambda b, i, j: (0, j)),
        ],
        out_specs=pl.BlockSpec((1, tm, tn), lambda b, i, j: (b, i, j)),
        out_shape=jax.ShapeDtypeStruct((bsz, s, N_IN), BF16),
        scratch_shapes=[pltpu.VMEM((tm, d), BF16)],
        compiler_params=_cparams(("parallel", "parallel", "arbitrary")),
        name="norm_inproj",
    )(x, mod, norm_g.reshape(1, d), w_in_p)


def _rope_tables(s):
    half = HEAD_DIM // 2
    inv = ROPE_THETA ** (-np.arange(0, half, 2, dtype=np.float64) / half)
    rows = s // GRID_W
    ang_r = np.arange(rows, dtype=np.float64)[:, None] * inv
    ang_c = np.arange(GRID_W, dtype=np.float64)[:, None] * inv
    cr, sr = np.cos(ang_r), np.sin(ang_r)
    cc, sc = np.cos(ang_c), np.sin(ang_c)
    cos_r = jnp.asarray(np.concatenate([cr, cr], -1), F32)
    sin_r = jnp.asarray(np.concatenate([-sr, sr], -1), F32)
    cos_c = jnp.asarray(np.concatenate([cc, cc], -1), F32)
    sin_c = jnp.asarray(np.concatenate([-sc, sc], -1), F32)
    cos_t = jnp.concatenate([jnp.repeat(cos_r, GRID_W, axis=0), jnp.tile(cos_c, (rows, 1))], -1)
    sin_t = jnp.concatenate([jnp.repeat(sin_r, GRID_W, axis=0), jnp.tile(sin_c, (rows, 1))], -1)
    return cos_t, sin_t


def _norm_rope(xh, g, cos_t, sin_t, lane_lo):
    ms = jnp.mean(xh * xh, axis=-1, keepdims=True)
    y = xh * lax.rsqrt(ms + EPS) * g
    partner = jnp.where(lane_lo, pltpu.roll(y, HEAD_DIM - 32, axis=1), pltpu.roll(y, 32, axis=1))
    return y * cos_t + partner * sin_t


def _prep_kernel(q_ref, kv_ref, cos_ref, sin_ref, qg_ref, kg_ref, qt_ref, k_ref, vt_ref):
    cos_t = cos_ref[...]
    sin_t = sin_ref[...]
    lane = lax.broadcasted_iota(jnp.int32, cos_t.shape, 1)
    lane_lo = (lane % 64) < 32
    q_scale = (HEAD_DIM ** -0.5) * LOG2E
    for h in range(N_HEADS):
        qh = q_ref[0, :, h * HEAD_DIM:(h + 1) * HEAD_DIM].astype(F32)
        qh = _norm_rope(qh, qg_ref[...], cos_t, sin_t, lane_lo) * q_scale
        qt_ref[0, h] = qh.T.astype(BF16)
    for h in range(N_KV):
        kh = kv_ref[0, :, h * HEAD_DIM:(h + 1) * HEAD_DIM].astype(F32)
        k_ref[0, h, 0] = _norm_rope(kh, kg_ref[...], cos_t, sin_t, lane_lo).astype(BF16)
        vh = kv_ref[0, :, KV_W + h * HEAD_DIM:KV_W + (h + 1) * HEAD_DIM].astype(F32)
        vt_ref[0, h, 0] = vh.T.astype(BF16)


def _attn_prep(proj, cos_t, sin_t, q_g, k_g, tk):
    bsz, s, _ = proj.shape
    nkb = s // tk
    return pl.pallas_call(
        _prep_kernel,
        grid=(bsz, nkb),
        in_specs=[
            pl.BlockSpec((1, tk, D_MODEL), lambda b, i: (b, i, COL_Q)),
            pl.BlockSpec((1, tk, 2 * KV_W), lambda b, i: (b, i, COL_KV * D_MODEL // (2 * KV_W))),
            pl.BlockSpec((tk, HEAD_DIM), lambda b, i: (i, 0)),
            pl.BlockSpec((tk, HEAD_DIM), lambda b, i: (i, 0)),
            pl.BlockSpec((1, HEAD_DIM), lambda b, i: (0, 0)),
            pl.BlockSpec((1, HEAD_DIM), lambda b, i: (0, 0)),
        ],
        out_specs=[
            pl.BlockSpec((1, N_HEADS, HEAD_DIM, tk), lambda b, i: (b, 0, 0, i)),
            pl.BlockSpec((1, N_KV, 1, tk, HEAD_DIM), lambda b, i: (b, 0, i, 0, 0)),
            pl.BlockSpec((1, N_KV, 1, HEAD_DIM, tk), lambda b, i: (b, 0, i, 0, 0)),
        ],
        out_shape=[
            jax.ShapeDtypeStruct((bsz, N_HEADS, HEAD_DIM, s), BF16),
            jax.ShapeDtypeStruct((bsz, N_KV, nkb, tk, HEAD_DIM), BF16),
            jax.ShapeDtypeStruct((bsz, N_KV, nkb, HEAD_DIM, tk), BF16),
        ],
        compiler_params=_cparams(("parallel", "parallel")),
        name="attn_prep",
    )(proj, proj, cos_t, sin_t, q_g.reshape(1, HEAD_DIM), k_g.reshape(1, HEAD_DIM))


SCORE_BOUND_NOSHIFT = 40.0


def _score_bound(q_g, k_g):
    q_scale = (HEAD_DIM ** -0.5) * LOG2E
    return HEAD_DIM * q_scale * jnp.max(jnp.abs(q_g)) * jnp.max(jnp.abs(k_g))


def _attn_kernel(flag_ref, qt_ref, k_ref, vt_ref, gb_ref, o_ref, m_ref, l_ref, acc_ref, *, nkb):
    tq = acc_ref.shape[-1]
    l_ref[...] = jnp.zeros(l_ref.shape, F32)
    acc_ref[...] = jnp.zeros(acc_ref.shape, F32)
    bounded = flag_ref[0] != 0

    @pl.when(bounded)
    def _():
        def kv_step(j, carry):
            kblk = k_ref[0, 0, j]
            vblk = vt_ref[0, 0, j]
            for h in range(GROUP):
                s = jnp.dot(kblk, qt_ref[0, h], preferred_element_type=F32)
                p = jnp.exp2(s)
                l_ref[h] += jnp.sum(p.reshape(-1, 8, tq), axis=0)
                acc_ref[h] += jnp.dot(vblk, p.astype(BF16), preferred_element_type=F32)
            return carry

        lax.fori_loop(0, nkb, kv_step, 0)

    @pl.when(jnp.logical_not(bounded))
    def _():
        m_ref[...] = jnp.full(m_ref.shape, -jnp.inf, F32)

        def kv_step(j, carry):
            kblk = k_ref[0, 0, j]
            vblk = vt_ref[0, 0, j]
            for h in range(GROUP):
                s = jnp.dot(kblk, qt_ref[0, h], preferred_element_type=F32)
                m_old = m_ref[h]
                m_new = jnp.maximum(m_old, jnp.max(s, axis=0, keepdims=True))
                alpha = jnp.exp2(m_old - m_new)
                p = jnp.exp2(s - m_new)
                l_ref[h] = alpha * l_ref[h] + jnp.sum(p.reshape(-1, 8, tq), axis=0)
                acc_ref[h] = alpha * acc_ref[h] + jnp.dot(vblk, p.astype(BF16), preferred_element_type=F32)
                m_ref[h] = m_new
            return carry

        lax.fori_loop(0, nkb, kv_step, 0)

    for h in range(GROUP):
        l = jnp.sum(l_ref[h], axis=0, keepdims=True)
        o = acc_ref[h] * (1.0 / l)
        gate = _silu(gb_ref[0, :, h * HEAD_DIM:(h + 1) * HEAD_DIM].astype(F32))
        o_ref[0, :, h * HEAD_DIM:(h + 1) * HEAD_DIM] = (o.T * gate).astype(BF16)


def _attention(qt, kk, vt, proj, bounded_flag, tq):
    bsz, _, _, s = qt.shape
    nkb, tk = kk.shape[2], kk.shape[3]
    gw = GROUP * HEAD_DIM
    grid_spec = pltpu.PrefetchScalarGridSpec(
        num_scalar_prefetch=1,
        grid=(bsz, N_KV, s // tq),
        in_specs=[
            pl.BlockSpec((1, GROUP, HEAD_DIM, tq), lambda b, g, i, f: (b, g, 0, i)),
            pl.BlockSpec((1, 1, nkb, tk, HEAD_DIM), lambda b, g, i, f: (b, g, 0, 0, 0)),
            pl.BlockSpec((1, 1, nkb, HEAD_DIM, tk), lambda b, g, i, f: (b, g, 0, 0, 0)),
            pl.BlockSpec((1, tq, gw), lambda b, g, i, f: (b, i, COL_GB * D_MODEL // gw + g)),
        ],
        out_specs=pl.BlockSpec((1, tq, gw), lambda b, g, i, f: (b, i, g)),
        scratch_shapes=[
            pltpu.VMEM((GROUP, 1, tq), F32),
            pltpu.VMEM((GROUP, 8, tq), F32),
            pltpu.VMEM((GROUP, HEAD_DIM, tq), F32),
        ],
    )
    return pl.pallas_call(
        functools.partial(_attn_kernel, nkb=nkb),
        grid_spec=grid_spec,
        out_shape=jax.ShapeDtypeStruct((bsz, s, D_MODEL), BF16),
        compiler_params=_cparams(("parallel", "parallel", "arbitrary")),
        name="gqa_attention",
    )(bounded_flag, qt, kk, vt, proj)


def _shift_rows(tile, halo, shift):
    t = tile.shape[0]
    row8 = lax.broadcasted_iota(jnp.int32, (8, tile.shape[1]), 0)
    if shift > 0:
        rolled = pltpu.roll(tile, shift, axis=0)
        edge = pltpu.roll(halo, shift, axis=0)[0:8]
        fixed = jnp.where(row8 < shift, edge, rolled[0:8])
        return jnp.concatenate([fixed, rolled[8:]], axis=0)
    k = -shift
    rolled = pltpu.roll(tile, t - k, axis=0)
    edge = pltpu.roll(halo, HALO - k, axis=0)[HALO - 8:HALO]
    fixed = jnp.where(row8 >= 8 - k, edge, rolled[t - 8:t])
    return jnp.concatenate([rolled[:t - 8], fixed], axis=0)


def _lru_inputs(x_ref, prev_ref, next_ref, first, last, cw_ref, cb_ref, wr_ref, br_ref, wi_ref, bi_ref,
                lam_ref, d):
    tile = x_ref[0].astype(F32)
    prev = jnp.where(first, 0.0, prev_ref[0].astype(F32))
    nxt = jnp.where(last, 0.0, next_ref[0].astype(F32))
    ua = (cw_ref[0:1, :] * _shift_rows(tile, prev, 2)
          + cw_ref[1:2, :] * _shift_rows(tile, prev, 1)
          + cw_ref[2:3, :] * tile
          + cw_ref[3:4, :] * _shift_rows(tile, nxt, -1)
          + cb_ref[...])
    ub = ua.astype(BF16)
    r_parts, i_parts = [], []
    for n in range(RNN_BLOCKS):
        blk = ub[:, n * RNN_BW:(n + 1) * RNN_BW]
        r_parts.append(jnp.dot(blk, wr_ref[d, n], preferred_element_type=F32))
        i_parts.append(jnp.dot(blk, wi_ref[d, n], preferred_element_type=F32))
    r = _sigmoid(jnp.concatenate(r_parts, axis=-1) + br_ref[d:d + 1, :])
    ig = _sigmoid(jnp.concatenate(i_parts, axis=-1) + bi_ref[d:d + 1, :])
    nl = -lam_ref[d:d + 1, :]
    softplus = jnp.maximum(nl, 0.0) + jnp.log1p(jnp.exp(-jnp.abs(nl)))
    log_a = (-LRU_C) * r * softplus
    a = jnp.exp(log_a)
    u = jnp.sqrt(-jnp.tanh(log_a) * (1.0 + a * a)) * (ig * ua)
    return a, u


def _block_scan(a, u, reverse):
    t, c = a.shape
    a3 = a.reshape(t // 8, 8, c)
    u3 = u.reshape(t // 8, 8, c)
    pos = lax.broadcasted_iota(jnp.int32, a3.shape, 1)
    for dist in (1, 2, 4):
        if reverse:
            sh = 8 - dist
            valid = pos < 8 - dist
        else:
            sh = dist
            valid = pos >= dist
        a_sh = jnp.where(valid, pltpu.roll(a3, sh, axis=1), 1.0)
        u_sh = jnp.where(valid, pltpu.roll(u3, sh, axis=1), 0.0)
        u3 = a3 * u_sh + u3
        a3 = a3 * a_sh
    return a3, u3


def _lru_kernel(xf_ref, pf_ref, nf_ref, xb_ref, pb_ref, nb_ref, cw_ref, cb_ref, wr_ref, br_ref, wi_ref,
                bi_ref, lam_ref, hf_ref, hb_ref, cf_ref, cbk_ref, af_ref, uf_ref, *, nt):
    i = pl.program_id(1)

    @pl.when(i == 0)
    def _():
        cf_ref[...] = jnp.zeros(cf_ref.shape, F32)
        cbk_ref[...] = jnp.zeros(cbk_ref.shape, F32)

    nblk = af_ref.shape[0]

    a, u = _lru_inputs(xf_ref, pf_ref, nf_ref, i == 0, i == nt - 1, cw_ref, cb_ref, wr_ref, br_ref, wi_ref,
                       bi_ref, lam_ref, 0)
    a3, u3 = _block_scan(a, u, reverse=False)
    af_ref[...] = a3.reshape(af_ref.shape)
    uf_ref[...] = u3.reshape(uf_ref.shape)

    def fwd_step(k, carry):
        ak = af_ref[k]
        uk = uf_ref[k]
        h0 = ak[0:8] * carry + uk[0:8]
        h1 = ak[8:16] * jnp.broadcast_to(h0[7:8, :], h0.shape) + uk[8:16]
        hf_ref[0, pl.ds(pl.multiple_of(k * 16, 16), 16), :] = jnp.concatenate([h0, h1], 0).astype(hf_ref.dtype)
        return jnp.broadcast_to(h1[7:8, :], h1.shape)

    cf_ref[...] = lax.fori_loop(0, nblk, fwd_step, cf_ref[...])

    a, u = _lru_inputs(xb_ref, pb_ref, nb_ref, i == nt - 1, i == 0, cw_ref, cb_ref, wr_ref, br_ref, wi_ref,
                       bi_ref, lam_ref, 1)
    a3, u3 = _block_scan(a, u, reverse=True)
    af_ref[...] = a3.reshape(af_ref.shape)
    uf_ref[...] = u3.reshape(uf_ref.shape)

    def bwd_step(k, carry):
        kk = nblk - 1 - k
        ak = af_ref[kk]
        uk = uf_ref[kk]
        h1 = ak[8:16] * carry + uk[8:16]
        h0 = ak[0:8] * jnp.broadcast_to(h1[0:1, :], h1.shape) + uk[0:8]
        hb_ref[0, pl.ds(pl.multiple_of(kk * 16, 16), 16), :] = jnp.concatenate([h0, h1], 0).astype(hb_ref.dtype)
        return jnp.broadcast_to(h0[0:1, :], h0.shape)

    cbk_ref[...] = lax.fori_loop(0, nblk, bwd_step, cbk_ref[...])


def _rg_lru(proj, conv_w, conv_b, w_rg, b_rg, w_ig, b_ig, lam):
    bsz, s, _ = proj.shape
    d = D_MODEL
    tm = min(256, s)
    nt = s // tm
    hb = tm // HALO
    nhalo = s // HALO

    def tile_specs(tile_of):
        return [
            pl.BlockSpec((1, tm, d), lambda b, i: (b, tile_of(i), COL_XA)),
            pl.BlockSpec((1, HALO, d), lambda b, i: (b, jnp.maximum(tile_of(i) * hb - 1, 0), COL_XA)),
            pl.BlockSpec((1, HALO, d), lambda b, i: (b, jnp.minimum((tile_of(i) + 1) * hb, nhalo - 1), COL_XA)),
        ]

    const2 = lambda b, i: (0, 0)
    const4 = lambda b, i: (0, 0, 0, 0)
    out_sds = jax.ShapeDtypeStruct((bsz, s, d), BF16)
    return pl.pallas_call(
        functools.partial(_lru_kernel, nt=nt),
        grid=(bsz, nt),
        in_specs=tile_specs(lambda i: i) + tile_specs(lambda i: nt - 1 - i) + [
            pl.BlockSpec(conv_w.shape, const2),
            pl.BlockSpec((1, d), const2),
            pl.BlockSpec(w_rg.shape, const4),
            pl.BlockSpec(b_rg.shape, const2),
            pl.BlockSpec(w_ig.shape, const4),
            pl.BlockSpec(b_ig.shape, const2),
            pl.BlockSpec(lam.shape, const2),
        ],
        out_specs=[
            pl.BlockSpec((1, tm, d), lambda b, i: (b, i, 0)),
            pl.BlockSpec((1, tm, d), lambda b, i: (b, nt - 1 - i, 0)),
        ],
        out_shape=[out_sds, out_sds],
        scratch_shapes=[
            pltpu.VMEM((8, d), F32),
            pltpu.VMEM((8, d), F32),
            pltpu.VMEM((tm // 16, 16, d), F32),
            pltpu.VMEM((tm // 16, 16, d), F32),
        ],
        compiler_params=_cparams(("parallel", "arbitrary")),
        name="rg_lru",
    )(proj, proj, proj, proj, proj, proj, conv_w, conv_b.reshape(1, d), w_rg, b_rg, w_ig, b_ig, lam)


def _merge_kernel(hf_ref, hb_ref, ga_ref, yb_ref, xc_ref, bc_ref, cc_ref, gc_ref, xcp_ref, ccp_ref, xcn_ref,
                  ccn_ref, mg0_ref, mg1_ref, mg2_ref, x_ref, mod_ref, cw_ref, wb_ref, wo_ref, o_ref, *, nt):
    i = pl.program_id(1)
    ya = (hf_ref[0].astype(F32) + hb_ref[0].astype(F32)) * _silu(ga_ref[0].astype(F32))

    z = cc_ref[0].astype(F32) * xc_ref[0].astype(F32)
    z_prev = jnp.where(i == 0, 0.0, ccp_ref[0].astype(F32) * xcp_ref[0].astype(F32))
    z_next = jnp.where(i == nt - 1, 0.0, ccn_ref[0].astype(F32) * xcn_ref[0].astype(F32))
    conv = (cw_ref[0:1, :] * _shift_rows(z, z_prev, 1) + cw_ref[1:2, :] * z
            + cw_ref[2:3, :] * _shift_rows(z, z_next, -1))
    yc = bc_ref[0].astype(F32) * conv * _silu(gc_ref[0].astype(F32))

    merged = _sigmoid(mg0_ref[0].astype(F32)) * jnp.dot(ya.astype(BF16), wb_ref[0], preferred_element_type=F32)
    merged += _sigmoid(mg1_ref[0].astype(F32)) * jnp.dot(yb_ref[0], wb_ref[1], preferred_element_type=F32)
    merged += _sigmoid(mg2_ref[0].astype(F32)) * jnp.dot(yc.astype(BF16), wb_ref[2], preferred_element_type=F32)
    out = jnp.dot(merged.astype(BF16), wo_ref[...], preferred_element_type=F32)
    o_ref[0] = x_ref[0] + mod_ref[0, 2:3, :] * out


def _merge(hf, hb, yb, proj, x, mod, conv_c_w, w_branch, w_out):
    bsz, s, d = x.shape
    tm = min(256, s)
    nt = s // tm
    hbk = tm // HALO
    nhalo = s // HALO

    def col(c):
        return pl.BlockSpec((1, tm, d), lambda b, i: (b, i, c))

    def halo_prev(c):
        return pl.BlockSpec((1, HALO, d), lambda b, i: (b, jnp.maximum(i * hbk - 1, 0), c))

    def halo_next(c):
        return pl.BlockSpec((1, HALO, d), lambda b, i: (b, jnp.minimum((i + 1) * hbk, nhalo - 1), c))

    return pl.pallas_call(
        functools.partial(_merge_kernel, nt=nt),
        grid=(bsz, nt),
        in_specs=[
            col(0), col(0), col(COL_GA), col(0), col(COL_XC), col(COL_BC), col(COL_CC), col(COL_GC),
            halo_prev(COL_XC), halo_prev(COL_CC), halo_next(COL_XC), halo_next(COL_CC),
            col(COL_MG), col(COL_MG + 1), col(COL_MG + 2),
            col(0),
            pl.BlockSpec((1, 3, d), lambda b, i: (b, 0, 0)),
            pl.BlockSpec(conv_c_w.shape, lambda b, i: (0, 0)),
            pl.BlockSpec(w_branch.shape, lambda b, i: (0, 0, 0)),
            pl.BlockSpec(w_out.shape, lambda b, i: (0, 0)),
        ],
        out_specs=pl.BlockSpec((1, tm, d), lambda b, i: (b, i, 0)),
        out_shape=jax.ShapeDtypeStruct((bsz, s, d), F32),
        compiler_params=_cparams(("parallel", "parallel")),
        name="merge_outproj",
    )(hf, hb, proj, yb, proj, proj, proj, proj, proj, proj, proj, proj, proj, proj, proj, x, mod,
      conv_c_w, w_branch, w_out)


def _permute_w_in(w):
    d = D_MODEL
    o_q = 2 * d
    o_k = o_q + d
    o_v = o_k + KV_W
    o_gb = o_v + KV_W
    o_xc = o_gb + d
    return jnp.concatenate([w[:, :o_q], w[:, o_q:o_k], w[:, o_gb:o_xc], w[:, o_xc:], w[:, o_k:o_gb]], axis=1)


def _attn_tiles(s):
    tq = min(512, s)
    tk = min(512, s)
    return tq, tk


def kernel(x, c, w_ada, b_ada, norm_g, w_in, conv_a_w, conv_a_b, w_rg, b_rg, w_ig, b_ig, lru_lam, q_norm_g,
           k_norm_g, conv_c_w, w_branch, w_out):
    bsz, s, d = x.shape
    depth = w_ada.shape[0]
    tq, tk = _attn_tiles(s)
    cos_t, sin_t = _rope_tables(s)
    mods = _modulation(c, w_ada, b_ada)
    for l in range(depth):
        mod = mods[l]
        proj = _in_projection(x, mod, norm_g[l], _permute_w_in(w_in[l]).astype(BF16))
        hf, hb = _rg_lru(proj, conv_a_w[l], conv_a_b[l], w_rg[l].astype(BF16), b_rg[l], w_ig[l].astype(BF16),
                         b_ig[l], lru_lam[l])
        qt, kk, vt = _attn_prep(proj, cos_t, sin_t, q_norm_g[l], k_norm_g[l], tk)
        bounded = (_score_bound(q_norm_g[l], k_norm_g[l]) <= SCORE_BOUND_NOSHIFT).astype(jnp.int32).reshape(1)
        yb = _attention(qt, kk, vt, proj, bounded, tq)
        x = _merge(hf, hb, yb, proj, x, mod, conv_c_w[l], w_branch[l].astype(BF16), w_out[l].astype(BF16))
    return x
```

```python
import functools
import math

import numpy as np
import jax
import jax.numpy as jnp
from jax import lax
from jax.experimental import pallas as pl
from jax.experimental.pallas import tpu as pltpu

D_MODEL = 1024
EPS = 1e-6
GRID_W = 64
N_HEADS = 8
N_KV = 2
GROUP = N_HEADS // N_KV
HEAD_DIM = 128
RNN_BLOCKS = 4
RNN_BW = D_MODEL // RNN_BLOCKS
LRU_C = 8.0
ROPE_THETA = 10000.0
N_BRANCH = 3
LOG2E = 1.4426950408889634

COL_XA, COL_GA, COL_Q, COL_GB, COL_XC, COL_BC, COL_CC, COL_GC, COL_MG = 0, 1, 2, 3, 4, 5, 6, 7, 8
COL_KV = 11
N_IN = 11 * D_MODEL + 2 * N_KV * HEAD_DIM
KV_W = N_KV * HEAD_DIM

HALO = 16
VMEM_LIMIT = 56 * 1024 * 1024

F32 = jnp.float32
BF16 = jnp.bfloat16


def _sigmoid(x):
    return 1.0 / (1.0 + jnp.exp(-x))


def _silu(x):
    return x * _sigmoid(x)


def _cparams(sem):
    return pltpu.CompilerParams(dimension_semantics=sem, vmem_limit_bytes=VMEM_LIMIT)


def _mod_kernel(c_ref, w_ref, b_ref, o_ref):
    acc = jnp.dot(c_ref[...], w_ref[0], preferred_element_type=F32, precision=lax.Precision.HIGHEST)
    o_ref[0] = acc + b_ref[0]


def _modulation(c, w_ada, b_ada):
    depth, d, n = w_ada.shape
    bsz = c.shape[0]
    rows = 8
    c8 = jnp.zeros((rows, d), F32).at[:bsz].set(c)
    tn = 1024
    out = pl.pallas_call(
        _mod_kernel,
        grid=(depth, n // tn),
        in_specs=[
            pl.BlockSpec((rows, d), lambda l, j: (0, 0)),
            pl.BlockSpec((1, d, tn), lambda l, j: (l, 0, j)),
            pl.BlockSpec((1, 1, tn), lambda l, j: (l, 0, j)),
        ],
        out_specs=pl.BlockSpec((1, rows, tn), lambda l, j: (l, 0, j)),
        out_shape=jax.ShapeDtypeStruct((depth, rows, n), F32),
        compiler_params=_cparams(("parallel", "parallel")),
        name="adaln_mod",
    )(c8, w_ada, b_ada.reshape(depth, 1, n))
    return out[:, :bsz].reshape(depth, bsz, 3, d)


def _inproj_kernel(x_ref, mod_ref, g_ref, w_ref, o_ref, h_ref):
    @pl.when(pl.program_id(2) == 0)
    def _():
        xv = x_ref[0]
        ms = jnp.mean(xv * xv, axis=-1, keepdims=True)
        y = xv * lax.rsqrt(ms + EPS) * g_ref[...]
        shift = mod_ref[0, 0:1, :]
        scale = mod_ref[0, 1:2, :]
        h_ref[...] = (y * (1.0 + scale) + shift).astype(BF16)

    o_ref[0] = jnp.dot(h_ref[...], w_ref[...], preferred_element_type=F32).astype(BF16)


def _in_projection(x, mod, norm_g, w_in_p):
    bsz, s, d = x.shape
    tm = min(2048, s)
    tn = 512
    return pl.pallas_call(
        _inproj_kernel,
        grid=(bsz, s // tm, N_IN // tn),
        in_specs=[
            pl.BlockSpec((1, tm, d), lambda b, i, j: (b, i, 0)),
            pl.BlockSpec((1, 3, d), lambda b, i, j: (b, 0, 0)),
            pl.BlockSpec((1, d), lambda b, i, j: (0, 0)),
            pl.BlockSpec((d, tn), lambda b, i, j: (0, j)),
        ],
        out_specs=pl.BlockSpec((1, tm, tn), lambda b, i, j: (b, i, j)),
        out_shape=jax.ShapeDtypeStruct((bsz, s, N_IN), BF16),
        scratch_shapes=[pltpu.VMEM((tm, d), BF16)],
        compiler_params=_cparams(("parallel", "parallel", "arbitrary")),
        name="norm_inproj",
    )(x, mod, norm_g.reshape(1, d), w_in_p)


def _rope_tables(s):
    half = HEAD_DIM // 2
    inv = ROPE_THETA ** (-np.arange(0, half, 2, dtype=np.float64) / half)
    rows = s // GRID_W
    ang_r = np.arange(rows, dtype=np.float64)[:, None] * inv
    ang_c = np.arange(GRID_W, dtype=np.float64)[:, None] * inv
    cr, sr = np.cos(ang_r), np.sin(ang_r)
    cc, sc = np.cos(ang_c), np.sin(ang_c)
    cos_r = jnp.asarray(np.concatenate([cr, cr], -1), F32)
    sin_r = jnp.asarray(np.concatenate([-sr, sr], -1), F32)
    cos_c = jnp.asarray(np.concatenate([cc, cc], -1), F32)
    sin_c = jnp.asarray(np.concatenate([-sc, sc], -1), F32)
    cos_t = jnp.concatenate([jnp.repeat(cos_r, GRID_W, axis=0), jnp.tile(cos_c, (rows, 1))], -1)
    sin_t = jnp.concatenate([jnp.repeat(sin_r, GRID_W, axis=0), jnp.tile(sin_c, (rows, 1))], -1)
    return cos_t, sin_t


def _norm_rope(xh, g, cos_t, sin_t, lane_lo):
    ms = jnp.mean(xh * xh, axis=-1, keepdims=True)
    y = xh * lax.rsqrt(ms + EPS) * g
    partner = jnp.where(lane_lo, pltpu.roll(y, HEAD_DIM - 32, axis=1), pltpu.roll(y, 32, axis=1))
    return y * cos_t + partner * sin_t


def _prep_kernel(q_ref, kv_ref, cos_ref, sin_ref, qg_ref, kg_ref, qt_ref, k_ref, vt_ref):
    cos_t = cos_ref[...]
    sin_t = sin_ref[...]
    lane = lax.broadcasted_iota(jnp.int32, cos_t.shape, 1)
    lane_lo = (lane % 64) < 32
    q_scale = (HEAD_DIM ** -0.5) * LOG2E
    for h in range(N_HEADS):
        qh = q_ref[0, :, h * HEAD_DIM:(h + 1) * HEAD_DIM].astype(F32)
        qh = _norm_rope(qh, qg_ref[...], cos_t, sin_t, lane_lo) * q_scale
        qt_ref[0, h] = qh.T.astype(BF16)
    for h in range(N_KV):
        kh = kv_ref[0, :, h * HEAD_DIM:(h + 1) * HEAD_DIM].astype(F32)
        k_ref[0, h, 0] = _norm_rope(kh, kg_ref[...], cos_t, sin_t, lane_lo).astype(BF16)
        vh = kv_ref[0, :, KV_W + h * HEAD_DIM:KV_W + (h + 1) * HEAD_DIM].astype(F32)
        vt_ref[0, h, 0] = vh.T.astype(BF16)


def _attn_prep(proj, cos_t, sin_t, q_g, k_g, tk):
    bsz, s, _ = proj.shape
    nkb = s // tk
    return pl.pallas_call(
        _prep_kernel,
        grid=(bsz, nkb),
        in_specs=[
            pl.BlockSpec((1, tk, D_MODEL), lambda b, i: (b, i, COL_Q)),
            pl.BlockSpec((1, tk, 2 * KV_W), lambda b, i: (b, i, COL_KV * D_MODEL // (2 * KV_W))),
            pl.BlockSpec((tk, HEAD_DIM), lambda b, i: (i, 0)),
            pl.BlockSpec((tk, HEAD_DIM), lambda b, i: (i, 0)),
            pl.BlockSpec((1, HEAD_DIM), lambda b, i: (0, 0)),
            pl.BlockSpec((1, HEAD_DIM), lambda b, i: (0, 0)),
        ],
        out_specs=[
            pl.BlockSpec((1, N_HEADS, HEAD_DIM, tk), lambda b, i: (b, 0, 0, i)),
            pl.BlockSpec((1, N_KV, 1, tk, HEAD_DIM), lambda b, i: (b, 0, i, 0, 0)),
            pl.BlockSpec((1, N_KV, 1, HEAD_DIM, tk), lambda b, i: (b, 0, i, 0, 0)),
        ],
        out_shape=[
            jax.ShapeDtypeStruct((bsz, N_HEADS, HEAD_DIM, s), BF16),
            jax.ShapeDtypeStruct((bsz, N_KV, nkb, tk, HEAD_DIM), BF16),
            jax.ShapeDtypeStruct((bsz, N_KV, nkb, HEAD_DIM, tk), BF16),
        ],
        compiler_params=_cparams(("parallel", "parallel")),
        name="attn_prep",
    )(proj, proj, cos_t, sin_t, q_g.reshape(1, HEAD_DIM), k_g.reshape(1, HEAD_DIM))


SCORE_BOUND_NOSHIFT = 40.0


def _score_bound(q_g, k_g):
    q_scale = (HEAD_DIM ** -0.5) * LOG2E
    return HEAD_DIM * q_scale * jnp.max(jnp.abs(q_g)) * jnp.max(jnp.abs(k_g))


def _attn_kernel(flag_ref, qt_ref, k_ref, vt_ref, gb_ref, o_ref, m_ref, l_ref, acc_ref, s_ref, *, nkb):
    tq = acc_ref.shape[-1]
    l_ref[...] = jnp.zeros(l_ref.shape, F32)
    acc_ref[...] = jnp.zeros(acc_ref.shape, F32)
    bounded = flag_ref[0] != 0

    @pl.when(bounded)
    def _():
        def scores(j, h):
            return jnp.dot(k_ref[0, 0, j], qt_ref[0, h], preferred_element_type=F32)

        s_ref[...] = scores(0, 0)

        def kv_step(j, carry):
            vblk = vt_ref[0, 0, j]
            j_next = jnp.minimum(j + 1, nkb - 1)
            s_cur = s_ref[...]
            for h in range(GROUP):
                s_next = scores(j, h + 1) if h + 1 < GROUP else scores(j_next, 0)
                p = jnp.exp2(s_cur)
                l_ref[h] += jnp.sum(p.reshape(-1, 8, tq), axis=0)
                acc_ref[h] += jnp.dot(vblk, p.astype(BF16), preferred_element_type=F32)
                s_cur = s_next
            s_ref[...] = s_cur
            return carry

        lax.fori_loop(0, nkb, kv_step, 0)

    @pl.when(jnp.logical_not(bounded))
    def _():
        m_ref[...] = jnp.full(m_ref.shape, -jnp.inf, F32)

        def kv_step(j, carry):
            kblk = k_ref[0, 0, j]
            vblk = vt_ref[0, 0, j]
            for h in range(GROUP):
                s = jnp.dot(kblk, qt_ref[0, h], preferred_element_type=F32)
                m_old = m_ref[h]
                m_new = jnp.maximum(m_old, jnp.max(s, axis=0, keepdims=True))
                alpha = jnp.exp2(m_old - m_new)
                p = jnp.exp2(s - m_new)
                l_ref[h] = alpha * l_ref[h] + jnp.sum(p.reshape(-1, 8, tq), axis=0)
                acc_ref[h] = alpha * acc_ref[h] + jnp.dot(vblk, p.astype(BF16), preferred_element_type=F32)
                m_ref[h] = m_new
            return carry

        lax.fori_loop(0, nkb, kv_step, 0)

    for h in range(GROUP):
        l = jnp.sum(l_ref[h], axis=0, keepdims=True)
        o = acc_ref[h] * (1.0 / l)
        gate = _silu(gb_ref[0, :, h * HEAD_DIM:(h + 1) * HEAD_DIM].astype(F32))
        o_ref[0, :, h * HEAD_DIM:(h + 1) * HEAD_DIM] = (o.T * gate).astype(BF16)


def _attention(qt, kk, vt, proj, bounded_flag, tq):
    bsz, _, _, s = qt.shape
    nkb, tk = kk.shape[2], kk.shape[3]
    gw = GROUP * HEAD_DIM
    grid_spec = pltpu.PrefetchScalarGridSpec(
        num_scalar_prefetch=1,
        grid=(bsz, N_KV, s // tq),
        in_specs=[
            pl.BlockSpec((1, GROUP, HEAD_DIM, tq), lambda b, g, i, f: (b, g, 0, i)),
            pl.BlockSpec((1, 1, nkb, tk, HEAD_DIM), lambda b, g, i, f: (b, g, 0, 0, 0)),
            pl.BlockSpec((1, 1, nkb, HEAD_DIM, tk), lambda b, g, i, f: (b, g, 0, 0, 0)),
            pl.BlockSpec((1, tq, gw), lambda b, g, i, f: (b, i, COL_GB * D_MODEL // gw + g)),
        ],
        out_specs=pl.BlockSpec((1, tq, gw), lambda b, g, i, f: (b, i, g)),
        scratch_shapes=[
            pltpu.VMEM((GROUP, 1, tq), F32),
            pltpu.VMEM((GROUP, 8, tq), F32),
            pltpu.VMEM((GROUP, HEAD_DIM, tq), F32),
            pltpu.VMEM((tk, tq), F32),
        ],
    )
    return pl.pallas_call(
        functools.partial(_attn_kernel, nkb=nkb),
        grid_spec=grid_spec,
        out_shape=jax.ShapeDtypeStruct((bsz, s, D_MODEL), BF16),
        compiler_params=_cparams(("parallel", "parallel", "arbitrary")),
        name="gqa_attention",
    )(bounded_flag, qt, kk, vt, proj)


def _shift_rows(tile, halo, shift):
    t = tile.shape[0]
    row8 = lax.broadcasted_iota(jnp.int32, (8, tile.shape[1]), 0)
    if shift > 0:
        rolled = pltpu.roll(tile, shift, axis=0)
        edge = pltpu.roll(halo, shift, axis=0)[0:8]
        fixed = jnp.where(row8 < shift, edge, rolled[0:8])
        return jnp.concatenate([fixed, rolled[8:]], axis=0)
    k = -shift
    rolled = pltpu.roll(tile, t - k, axis=0)
    edge = pltpu.roll(halo, HALO - k, axis=0)[HALO - 8:HALO]
    fixed = jnp.where(row8 >= 8 - k, edge, rolled[t - 8:t])
    return jnp.concatenate([rolled[:t - 8], fixed], axis=0)


def _lru_inputs(x_ref, prev_ref, next_ref, first, last, cw_ref, cb_ref, wr_ref, br_ref, wi_ref, bi_ref,
                lam_ref, d):
    tile = x_ref[0].astype(F32)
    prev = jnp.where(first, 0.0, prev_ref[0].astype(F32))
    nxt = jnp.where(last, 0.0, next_ref[0].astype(F32))
    ua = (cw_ref[0:1, :] * _shift_rows(tile, prev, 2)
          + cw_ref[1:2, :] * _shift_rows(tile, prev, 1)
          + cw_ref[2:3, :] * tile
          + cw_ref[3:4, :] * _shift_rows(tile, nxt, -1)
          + cb_ref[...])
    ub = ua.astype(BF16)
    r_parts, i_parts = [], []
    for n in range(RNN_BLOCKS):
        blk = ub[:, n * RNN_BW:(n + 1) * RNN_BW]
        r_parts.append(jnp.dot(blk, wr_ref[d, n], preferred_element_type=F32))
        i_parts.append(jnp.dot(blk, wi_ref[d, n], preferred_element_type=F32))
    r = _sigmoid(jnp.concatenate(r_parts, axis=-1) + br_ref[d:d + 1, :])
    ig = _sigmoid(jnp.concatenate(i_parts, axis=-1) + bi_ref[d:d + 1, :])
    nl = -lam_ref[d:d + 1, :]
    softplus = jnp.maximum(nl, 0.0) + jnp.log1p(jnp.exp(-jnp.abs(nl)))
    log_a = (-LRU_C) * r * softplus
    a = jnp.exp(log_a)
    u = jnp.sqrt(-jnp.tanh(log_a) * (1.0 + a * a)) * (ig * ua)
    return a, u


def _block_scan(a, u, reverse):
    t, c = a.shape
    a3 = a.reshape(t // 8, 8, c)
    u3 = u.reshape(t // 8, 8, c)
    pos = lax.broadcasted_iota(jnp.int32, a3.shape, 1)
    for dist in (1, 2, 4):
        if reverse:
            sh = 8 - dist
            valid = pos < 8 - dist
        else:
            sh = dist
            valid = pos >= dist
        a_sh = jnp.where(valid, pltpu.roll(a3, sh, axis=1), 1.0)
        u_sh = jnp.where(valid, pltpu.roll(u3, sh, axis=1), 0.0)
        u3 = a3 * u_sh + u3
        a3 = a3 * a_sh
    return a3, u3


def _lru_kernel(xf_ref, pf_ref, nf_ref, xb_ref, pb_ref, nb_ref, cw_ref, cb_ref, wr_ref, br_ref, wi_ref,
                bi_ref, lam_ref, hf_ref, hb_ref, cf_ref, cbk_ref, af_ref, uf_ref, *, nt):
    i = pl.program_id(1)

    @pl.when(i == 0)
    def _():
        cf_ref[...] = jnp.zeros(cf_ref.shape, F32)
        cbk_ref[...] = jnp.zeros(cbk_ref.shape, F32)

    nblk = af_ref.shape[0]

    a, u = _lru_inputs(xf_ref, pf_ref, nf_ref, i == 0, i == nt - 1, cw_ref, cb_ref, wr_ref, br_ref, wi_ref,
                       bi_ref, lam_ref, 0)
    a3, u3 = _block_scan(a, u, reverse=False)
    af_ref[...] = a3.reshape(af_ref.shape)
    uf_ref[...] = u3.reshape(uf_ref.shape)

    def fwd_step(k, carry):
        ak = af_ref[k]
        uk = uf_ref[k]
        h0 = ak[0:8] * carry + uk[0:8]
        h1 = ak[8:16] * jnp.broadcast_to(h0[7:8, :], h0.shape) + uk[8:16]
        hf_ref[0, pl.ds(pl.multiple_of(k * 16, 16), 16), :] = jnp.concatenate([h0, h1], 0).astype(hf_ref.dtype)
        return jnp.broadcast_to(h1[7:8, :], h1.shape)

    cf_ref[...] = lax.fori_loop(0, nblk, fwd_step, cf_ref[...])

    a, u = _lru_inputs(xb_ref, pb_ref, nb_ref, i == nt - 1, i == 0, cw_ref, cb_ref, wr_ref, br_ref, wi_ref,
                       bi_ref, lam_ref, 1)
    a3, u3 = _block_scan(a, u, reverse=True)
    af_ref[...] = a3.reshape(af_ref.shape)
    uf_ref[...] = u3.reshape(uf_ref.shape)

    def bwd_step(k, carry):
        kk = nblk - 1 - k
        ak = af_ref[kk]
        uk = uf_ref[kk]
        h1 = ak[8:16] * carry + uk[8:16]
        h0 = ak[0:8] * jnp.broadcast_to(h1[0:1, :], h1.shape) + uk[0:8]
        hb_ref[0, pl.ds(pl.multiple_of(kk * 16, 16), 16), :] = jnp.concatenate([h0, h1], 0).astype(hb_ref.dtype)
        return jnp.broadcast_to(h0[0:1, :], h0.shape)

    cbk_ref[...] = lax.fori_loop(0, nblk, bwd_step, cbk_ref[...])


def _rg_lru(proj, conv_w, conv_b, w_rg, b_rg, w_ig, b_ig, lam):
    bsz, s, _ = proj.shape
    d = D_MODEL
    tm = min(256, s)
    nt = s // tm
    hb = tm // HALO
    nhalo = s // HALO

    def tile_specs(tile_of):
        return [
            pl.BlockSpec((1, tm, d), lambda b, i: (b, tile_of(i), COL_XA)),
            pl.BlockSpec((1, HALO, d), lambda b, i: (b, jnp.maximum(tile_of(i) * hb - 1, 0), COL_XA)),
            pl.BlockSpec((1, HALO, d), lambda b, i: (b, jnp.minimum((tile_of(i) + 1) * hb, nhalo - 1), COL_XA)),
        ]

    const2 = lambda b, i: (0, 0)
    const4 = lambda b, i: (0, 0, 0, 0)
    out_sds = jax.ShapeDtypeStruct((bsz, s, d), BF16)
    return pl.pallas_call(
        functools.partial(_lru_kernel, nt=nt),
        grid=(bsz, nt),
        in_specs=tile_specs(lambda i: i) + tile_specs(lambda i: nt - 1 - i) + [
            pl.BlockSpec(conv_w.shape, const2),
            pl.BlockSpec((1, d), const2),
            pl.BlockSpec(w_rg.shape, const4),
            pl.BlockSpec(b_rg.shape, const2),
            pl.BlockSpec(w_ig.shape, const4),
            pl.BlockSpec(b_ig.shape, const2),
            pl.BlockSpec(lam.shape, const2),
        ],
        out_specs=[
            pl.BlockSpec((1, tm, d), lambda b, i: (b, i, 0)),
            pl.BlockSpec((1, tm, d), lambda b, i: (b, nt - 1 - i, 0)),
        ],
        out_shape=[out_sds, out_sds],
        scratch_shapes=[
            pltpu.VMEM((8, d), F32),
            pltpu.VMEM((8, d), F32),
            pltpu.VMEM((tm // 16, 16, d), F32),
            pltpu.VMEM((tm // 16, 16, d), F32),
        ],
        compiler_params=_cparams(("parallel", "arbitrary")),
        name="rg_lru",
    )(proj, proj, proj, proj, proj, proj, conv_w, conv_b.reshape(1, d), w_rg, b_rg, w_ig, b_ig, lam)


def _merge_kernel(hf_ref, hb_ref, ga_ref, yb_ref, xc_ref, bc_ref, cc_ref, gc_ref, xcp_ref, ccp_ref, xcn_ref,
                  ccn_ref, mg0_ref, mg1_ref, mg2_ref, x_ref, mod_ref, cw_ref, wb_ref, wo_ref, o_ref, *, nt):
    i = pl.program_id(1)
    ya = (hf_ref[0].astype(F32) + hb_ref[0].astype(F32)) * _silu(ga_ref[0].astype(F32))

    z = cc_ref[0].astype(F32) * xc_ref[0].astype(F32)
    z_prev = jnp.where(i == 0, 0.0, ccp_ref[0].astype(F32) * xcp_ref[0].astype(F32))
    z_next = jnp.where(i == nt - 1, 0.0, ccn_ref[0].astype(F32) * xcn_ref[0].astype(F32))
    conv = (cw_ref[0:1, :] * _shift_rows(z, z_prev, 1) + cw_ref[1:2, :] * z
            + cw_ref[2:3, :] * _shift_rows(z, z_next, -1))
    yc = bc_ref[0].astype(F32) * conv * _silu(gc_ref[0].astype(F32))

    merged = _sigmoid(mg0_ref[0].astype(F32)) * jnp.dot(ya.astype(BF16), wb_ref[0], preferred_element_type=F32)
    merged += _sigmoid(mg1_ref[0].astype(F32)) * jnp.dot(yb_ref[0], wb_ref[1], preferred_element_type=F32)
    merged += _sigmoid(mg2_ref[0].astype(F32)) * jnp.dot(yc.astype(BF16), wb_ref[2], preferred_element_type=F32)
    out = jnp.dot(merged.astype(BF16), wo_ref[...], preferred_element_type=F32)
    o_ref[0] = x_ref[0] + mod_ref[0, 2:3, :] * out


def _merge(hf, hb, yb, proj, x, mod, conv_c_w, w_branch, w_out):
    bsz, s, d = x.shape
    tm = min(256, s)
    nt = s // tm
    hbk = tm // HALO
    nhalo = s // HALO

    def col(c):
        return pl.BlockSpec((1, tm, d), lambda b, i: (b, i, c))

    def halo_prev(c):
        return pl.BlockSpec((1, HALO, d), lambda b, i: (b, jnp.maximum(i * hbk - 1, 0), c))

    def halo_next(c):
        return pl.BlockSpec((1, HALO, d), lambda b, i: (b, jnp.minimum((i + 1) * hbk, nhalo - 1), c))

    return pl.pallas_call(
        functools.partial(_merge_kernel, nt=nt),
        grid=(bsz, nt),
        in_specs=[
            col(0), col(0), col(COL_GA), col(0), col(COL_XC), col(COL_BC), col(COL_CC), col(COL_GC),
            halo_prev(COL_XC), halo_prev(COL_CC), halo_next(COL_XC), halo_next(COL_CC),
            col(COL_MG), col(COL_MG + 1), col(COL_MG + 2),
            col(0),
            pl.BlockSpec((1, 3, d), lambda b, i: (b, 0, 0)),
            pl.BlockSpec(conv_c_w.shape, lambda b, i: (0, 0)),
            pl.BlockSpec(w_branch.shape, lambda b, i: (0, 0, 0)),
            pl.BlockSpec(w_out.shape, lambda b, i: (0, 0)),
        ],
        out_specs=pl.BlockSpec((1, tm, d), lambda b, i: (b, i, 0)),
        out_shape=jax.ShapeDtypeStruct((bsz, s, d), F32),
        compiler_params=_cparams(("parallel", "parallel")),
        name="merge_outproj",
    )(hf, hb, proj, yb, proj, proj, proj, proj, proj, proj, proj, proj, proj, proj, proj, x, mod,
      conv_c_w, w_branch, w_out)


def _permute_w_in(w):
    d = D_MODEL
    o_q = 2 * d
    o_k = o_q + d
    o_v = o_k + KV_W
    o_gb = o_v + KV_W
    o_xc = o_gb + d
    return jnp.concatenate([w[:, :o_q], w[:, o_q:o_k], w[:, o_gb:o_xc], w[:, o_xc:], w[:, o_k:o_gb]], axis=1)


def _attn_tiles(s):
    tq = min(512, s)
    tk = min(512, s)
    return tq, tk


def kernel(x, c, w_ada, b_ada, norm_g, w_in, conv_a_w, conv_a_b, w_rg, b_rg, w_ig, b_ig, lru_lam, q_norm_g,
           k_norm_g, conv_c_w, w_branch, w_out):
    bsz, s, d = x.shape
    depth = w_ada.shape[0]
    tq, tk = _attn_tiles(s)
    cos_t, sin_t = _rope_tables(s)
    mods = _modulation(c, w_ada, b_ada)
    for l in range(depth):
        mod = mods[l]
        proj = _in_projection(x, mod, norm_g[l], _permute_w_in(w_in[l]).astype(BF16))
        hf, hb = _rg_lru(proj, conv_a_w[l], conv_a_b[l], w_rg[l].astype(BF16), b_rg[l], w_ig[l].astype(BF16),
                         b_ig[l], lru_lam[l])
        qt, kk, vt = _attn_prep(proj, cos_t, sin_t, q_norm_g[l], k_norm_g[l], tk)
        bounded = (_score_bound(q_norm_g[l], k_norm_g[l]) <= SCORE_BOUND_NOSHIFT).astype(jnp.int32).reshape(1)
        yb = _attention(qt, kk, vt, proj, bounded, tq)
        x = _merge(hf, hb, yb, proj, x, mod, conv_c_w[l], w_branch[l].astype(BF16), w_out[l].astype(BF16))
    return x
```

```python
import functools
import math

import numpy as np
import jax
import jax.numpy as jnp
from jax import lax
from jax.experimental import pallas as pl
from jax.experimental.pallas import tpu as pltpu

D_MODEL = 1024
EPS = 1e-6
GRID_W = 64
N_HEADS = 8
N_KV = 2
GROUP = N_HEADS // N_KV
HEAD_DIM = 128
RNN_BLOCKS = 4
RNN_BW = D_MODEL // RNN_BLOCKS
LRU_C = 8.0
ROPE_THETA = 10000.0
N_BRANCH = 3
LOG2E = 1.4426950408889634

COL_XA, COL_GA, COL_Q, COL_GB, COL_XC, COL_BC, COL_CC, COL_GC, COL_MG = 0, 1, 2, 3, 4, 5, 6, 7, 8
COL_KV = 11
N_IN = 11 * D_MODEL + 2 * N_KV * HEAD_DIM
KV_W = N_KV * HEAD_DIM

HALO = 16
VMEM_LIMIT = 56 * 1024 * 1024

F32 = jnp.float32
BF16 = jnp.bfloat16


def _sigmoid(x):
    return 1.0 / (1.0 + jnp.exp(-x))


def _silu(x):
    return x * _sigmoid(x)


def _cparams(sem):
    return pltpu.CompilerParams(dimension_semantics=sem, vmem_limit_bytes=VMEM_LIMIT)


def _mod_kernel(c_ref, w_ref, b_ref, o_ref):
    acc = jnp.dot(c_ref[...], w_ref[0], preferred_element_type=F32, precision=lax.Precision.HIGHEST)
    o_ref[0] = acc + b_ref[0]


def _modulation(c, w_ada, b_ada):
    depth, d, n = w_ada.shape
    bsz = c.shape[0]
    rows = 8
    c8 = jnp.zeros((rows, d), F32).at[:bsz].set(c)
    tn = 1024
    out = pl.pallas_call(
        _mod_kernel,
        grid=(depth, n // tn),
        in_specs=[
            pl.BlockSpec((rows, d), lambda l, j: (0, 0)),
            pl.BlockSpec((1, d, tn), lambda l, j: (l, 0, j)),
            pl.BlockSpec((1, 1, tn), lambda l, j: (l, 0, j)),
        ],
        out_specs=pl.BlockSpec((1, rows, tn), lambda l, j: (l, 0, j)),
        out_shape=jax.ShapeDtypeStruct((depth, rows, n), F32),
        compiler_params=_cparams(("parallel", "parallel")),
        name="adaln_mod",
    )(c8, w_ada, b_ada.reshape(depth, 1, n))
    return out[:, :bsz].reshape(depth, bsz, 3, d)


def _inproj_kernel(x_ref, mod_ref, g_ref, w_ref, o_ref, h_ref):
    @pl.when(pl.program_id(2) == 0)
    def _():
        xv = x_ref[0]
        ms = jnp.mean(xv * xv, axis=-1, keepdims=True)
        y = xv * lax.rsqrt(ms + EPS) * g_ref[...]
        shift = mod_ref[0, 0:1, :]
        scale = mod_ref[0, 1:2, :]
        h_ref[...] = (y * (1.0 + scale) + shift).astype(BF16)

    o_ref[0] = jnp.dot(h_ref[...], w_ref[...], preferred_element_type=F32).astype(BF16)


def _in_projection(x, mod, norm_g, w_in_p):
    bsz, s, d = x.shape
    tm = min(2048, s)
    tn = 512
    return pl.pallas_call(
        _inproj_kernel,
        grid=(bsz, s // tm, N_IN // tn),
        in_specs=[
            pl.BlockSpec((1, tm, d), lambda b, i, j: (b, i, 0)),
            pl.BlockSpec((1, 3, d), lambda b, i, j: (b, 0, 0)),
            pl.BlockSpec((1, d), lambda b, i, j: (0, 0)),
            pl.BlockSpec((d, tn), lambda b, i, j: (0, j)),
        ],
        out_specs=pl.BlockSpec((1, tm, tn), lambda b, i, j: (b, i, j)),
        out_shape=jax.ShapeDtypeStruct((bsz, s, N_IN), BF16),
        scratch_shapes=[pltpu.VMEM((tm, d), BF16)],
        compiler_params=_cparams(("parallel", "parallel", "arbitrary")),
        name="norm_inproj",
    )(x, mod, norm_g.reshape(1, d), w_in_p)


def _rope_tables(s):
    half = HEAD_DIM // 2
    inv = ROPE_THETA ** (-np.arange(0, half, 2, dtype=np.float64) / half)
    rows = s // GRID_W
    ang_r = np.arange(rows, dtype=np.float64)[:, None] * inv
    ang_c = np.arange(GRID_W, dtype=np.float64)[:, None] * inv
    cr, sr = np.cos(ang_r), np.sin(ang_r)
    cc, sc = np.cos(ang_c), np.sin(ang_c)
    cos_r = jnp.asarray(np.concatenate([cr, cr], -1), F32)
    sin_r = jnp.asarray(np.concatenate([-sr, sr], -1), F32)
    cos_c = jnp.asarray(np.concatenate([cc, cc], -1), F32)
    sin_c = jnp.asarray(np.concatenate([-sc, sc], -1), F32)
    cos_t = jnp.concatenate([jnp.repeat(cos_r, GRID_W, axis=0), jnp.tile(cos_c, (rows, 1))], -1)
    sin_t = jnp.concatenate([jnp.repeat(sin_r, GRID_W, axis=0), jnp.tile(sin_c, (rows, 1))], -1)
    return cos_t, sin_t


def _norm_rope(xh, g, cos_t, sin_t, lane_lo):
    ms = jnp.mean(xh * xh, axis=-1, keepdims=True)
    y = xh * lax.rsqrt(ms + EPS) * g
    partner = jnp.where(lane_lo, pltpu.roll(y, HEAD_DIM - 32, axis=1), pltpu.roll(y, 32, axis=1))
    return y * cos_t + partner * sin_t


def _prep_kernel(q_ref, kv_ref, cos_ref, sin_ref, qg_ref, kg_ref, qt_ref, k_ref, vt_ref):
    cos_t = cos_ref[...]
    sin_t = sin_ref[...]
    lane = lax.broadcasted_iota(jnp.int32, cos_t.shape, 1)
    lane_lo = (lane % 64) < 32
    q_scale = (HEAD_DIM ** -0.5) * LOG2E
    for h in range(N_HEADS):
        qh = q_ref[0, :, h * HEAD_DIM:(h + 1) * HEAD_DIM].astype(F32)
        qh = _norm_rope(qh, qg_ref[...], cos_t, sin_t, lane_lo) * q_scale
        qt_ref[0, h] = qh.T.astype(BF16)
    for h in range(N_KV):
        kh = kv_ref[0, :, h * HEAD_DIM:(h + 1) * HEAD_DIM].astype(F32)
        k_ref[0, h, 0] = _norm_rope(kh, kg_ref[...], cos_t, sin_t, lane_lo).astype(BF16)
        vh = kv_ref[0, :, KV_W + h * HEAD_DIM:KV_W + (h + 1) * HEAD_DIM].astype(F32)
        vt_ref[0, h, 0] = vh.T.astype(BF16)


def _attn_prep(proj, cos_t, sin_t, q_g, k_g, tk):
    bsz, s, _ = proj.shape
    nkb = s // tk
    return pl.pallas_call(
        _prep_kernel,
        grid=(bsz, nkb),
        in_specs=[
            pl.BlockSpec((1, tk, D_MODEL), lambda b, i: (b, i, COL_Q)),
            pl.BlockSpec((1, tk, 2 * KV_W), lambda b, i: (b, i, COL_KV * D_MODEL // (2 * KV_W))),
            pl.BlockSpec((tk, HEAD_DIM), lambda b, i: (i, 0)),
            pl.BlockSpec((tk, HEAD_DIM), lambda b, i: (i, 0)),
            pl.BlockSpec((1, HEAD_DIM), lambda b, i: (0, 0)),
            pl.BlockSpec((1, HEAD_DIM), lambda b, i: (0, 0)),
        ],
        out_specs=[
            pl.BlockSpec((1, N_HEADS, HEAD_DIM, tk), lambda b, i: (b, 0, 0, i)),
            pl.BlockSpec((1, N_KV, 1, tk, HEAD_DIM), lambda b, i: (b, 0, i, 0, 0)),
            pl.BlockSpec((1, N_KV, 1, HEAD_DIM, tk), lambda b, i: (b, 0, i, 0, 0)),
        ],
        out_shape=[
            jax.ShapeDtypeStruct((bsz, N_HEADS, HEAD_DIM, s), BF16),
            jax.ShapeDtypeStruct((bsz, N_KV, nkb, tk, HEAD_DIM), BF16),
            jax.ShapeDtypeStruct((bsz, N_KV, nkb, HEAD_DIM, tk), BF16),
        ],
        compiler_params=_cparams(("parallel", "parallel")),
        name="attn_prep",
    )(proj, proj, cos_t, sin_t, q_g.reshape(1, HEAD_DIM), k_g.reshape(1, HEAD_DIM))


SCORE_BOUND_NOSHIFT = 40.0


def _score_bound(q_g, k_g):
    q_scale = (HEAD_DIM ** -0.5) * LOG2E
    return HEAD_DIM * q_scale * jnp.max(jnp.abs(q_g)) * jnp.max(jnp.abs(k_g))


def _attn_kernel(flag_ref, qt_ref, k_ref, vt_ref, gb_ref, o_ref, m_ref, l_ref, acc_ref, s_ref, *, nkb):
    tq = acc_ref.shape[-1]
    l_ref[...] = jnp.zeros(l_ref.shape, F32)
    acc_ref[...] = jnp.zeros(acc_ref.shape, F32)
    bounded = flag_ref[0] != 0

    @pl.when(bounded)
    def _():
        def scores(j, h):
            return jnp.dot(k_ref[0, 0, j], qt_ref[0, h], preferred_element_type=F32)

        s_ref[...] = scores(0, 0)

        def kv_step(j, carry):
            vblk = vt_ref[0, 0, j]
            j_next = jnp.minimum(j + 1, nkb - 1)
            s_cur = s_ref[...]
            for h in range(GROUP):
                s_next = scores(j, h + 1) if h + 1 < GROUP else scores(j_next, 0)
                p = jnp.exp2(s_cur)
                l_ref[h] += jnp.sum(p.reshape(-1, 8, tq), axis=0)
                acc_ref[h] += jnp.dot(vblk, p.astype(BF16), preferred_element_type=F32)
                s_cur = s_next
            s_ref[...] = s_cur
            return carry

        lax.fori_loop(0, nkb, kv_step, 0, unroll=4)

    @pl.when(jnp.logical_not(bounded))
    def _():
        m_ref[...] = jnp.full(m_ref.shape, -jnp.inf, F32)

        def kv_step(j, carry):
            kblk = k_ref[0, 0, j]
            vblk = vt_ref[0, 0, j]
            for h in range(GROUP):
                s = jnp.dot(kblk, qt_ref[0, h], preferred_element_type=F32)
                m_old = m_ref[h]
                m_new = jnp.maximum(m_old, jnp.max(s, axis=0, keepdims=True))
                alpha = jnp.exp2(m_old - m_new)
                p = jnp.exp2(s - m_new)
                l_ref[h] = alpha * l_ref[h] + jnp.sum(p.reshape(-1, 8, tq), axis=0)
                acc_ref[h] = alpha * acc_ref[h] + jnp.dot(vblk, p.astype(BF16), preferred_element_type=F32)
                m_ref[h] = m_new
            return carry

        lax.fori_loop(0, nkb, kv_step, 0)

    for h in range(GROUP):
        l = jnp.sum(l_ref[h], axis=0, keepdims=True)
        o = acc_ref[h] * (1.0 / l)
        gate = _silu(gb_ref[0, :, h * HEAD_DIM:(h + 1) * HEAD_DIM].astype(F32))
        o_ref[0, :, h * HEAD_DIM:(h + 1) * HEAD_DIM] = (o.T * gate).astype(BF16)


def _attention(qt, kk, vt, proj, bounded_flag, tq):
    bsz, _, _, s = qt.shape
    nkb, tk = kk.shape[2], kk.shape[3]
    gw = GROUP * HEAD_DIM
    grid_spec = pltpu.PrefetchScalarGridSpec(
        num_scalar_prefetch=1,
        grid=(bsz, N_KV, s // tq),
        in_specs=[
            pl.BlockSpec((1, GROUP, HEAD_DIM, tq), lambda b, g, i, f: (b, g, 0, i)),
            pl.BlockSpec((1, 1, nkb, tk, HEAD_DIM), lambda b, g, i, f: (b, g, 0, 0, 0)),
            pl.BlockSpec((1, 1, nkb, HEAD_DIM, tk), lambda b, g, i, f: (b, g, 0, 0, 0)),
            pl.BlockSpec((1, tq, gw), lambda b, g, i, f: (b, i, COL_GB * D_MODEL // gw + g)),
        ],
        out_specs=pl.BlockSpec((1, tq, gw), lambda b, g, i, f: (b, i, g)),
        scratch_shapes=[
            pltpu.VMEM((GROUP, 1, tq), F32),
            pltpu.VMEM((GROUP, 8, tq), F32),
            pltpu.VMEM((GROUP, HEAD_DIM, tq), F32),
            pltpu.VMEM((tk, tq), F32),
        ],
    )
    return pl.pallas_call(
        functools.partial(_attn_kernel, nkb=nkb),
        grid_spec=grid_spec,
        out_shape=jax.ShapeDtypeStruct((bsz, s, D_MODEL), BF16),
        compiler_params=_cparams(("parallel", "parallel", "arbitrary")),
        name="gqa_attention",
    )(bounded_flag, qt, kk, vt, proj)


def _shift_rows(tile, halo, shift):
    t = tile.shape[0]
    row8 = lax.broadcasted_iota(jnp.int32, (8, tile.shape[1]), 0)
    if shift > 0:
        rolled = pltpu.roll(tile, shift, axis=0)
        edge = pltpu.roll(halo, shift, axis=0)[0:8]
        fixed = jnp.where(row8 < shift, edge, rolled[0:8])
        return jnp.concatenate([fixed, rolled[8:]], axis=0)
    k = -shift
    rolled = pltpu.roll(tile, t - k, axis=0)
    edge = pltpu.roll(halo, HALO - k, axis=0)[HALO - 8:HALO]
    fixed = jnp.where(row8 >= 8 - k, edge, rolled[t - 8:t])
    return jnp.concatenate([rolled[:t - 8], fixed], axis=0)


def _lru_inputs(x_ref, prev_ref, next_ref, first, last, cw_ref, cb_ref, wr_ref, br_ref, wi_ref, bi_ref,
                lam_ref, d):
    tile = x_ref[0].astype(F32)
    prev = jnp.where(first, 0.0, prev_ref[0].astype(F32))
    nxt = jnp.where(last, 0.0, next_ref[0].astype(F32))
    ua = (cw_ref[0:1, :] * _shift_rows(tile, prev, 2)
          + cw_ref[1:2, :] * _shift_rows(tile, prev, 1)
          + cw_ref[2:3, :] * tile
          + cw_ref[3:4, :] * _shift_rows(tile, nxt, -1)
          + cb_ref[...])
    ub = ua.astype(BF16)
    r_parts, i_parts = [], []
    for n in range(RNN_BLOCKS):
        blk = ub[:, n * RNN_BW:(n + 1) * RNN_BW]
        r_parts.append(jnp.dot(blk, wr_ref[d, n], preferred_element_type=F32))
        i_parts.append(jnp.dot(blk, wi_ref[d, n], preferred_element_type=F32))
    r = _sigmoid(jnp.concatenate(r_parts, axis=-1) + br_ref[d:d + 1, :])
    ig = _sigmoid(jnp.concatenate(i_parts, axis=-1) + bi_ref[d:d + 1, :])
    nl = -lam_ref[d:d + 1, :]
    softplus = jnp.maximum(nl, 0.0) + jnp.log1p(jnp.exp(-jnp.abs(nl)))
    log_a = (-LRU_C) * r * softplus
    a = jnp.exp(log_a)
    u = jnp.sqrt(-jnp.tanh(log_a) * (1.0 + a * a)) * (ig * ua)
    return a, u


def _block_scan(a, u, reverse):
    t, c = a.shape
    a3 = a.reshape(t // 8, 8, c)
    u3 = u.reshape(t // 8, 8, c)
    pos = lax.broadcasted_iota(jnp.int32, a3.shape, 1)
    for dist in (1, 2, 4):
        if reverse:
            sh = 8 - dist
            valid = pos < 8 - dist
        else:
            sh = dist
            valid = pos >= dist
        a_sh = jnp.where(valid, pltpu.roll(a3, sh, axis=1), 1.0)
        u_sh = jnp.where(valid, pltpu.roll(u3, sh, axis=1), 0.0)
        u3 = a3 * u_sh + u3
        a3 = a3 * a_sh
    return a3, u3


def _lru_kernel(xf_ref, pf_ref, nf_ref, xb_ref, pb_ref, nb_ref, cw_ref, cb_ref, wr_ref, br_ref, wi_ref,
                bi_ref, lam_ref, hf_ref, hb_ref, cf_ref, cbk_ref, af_ref, uf_ref, *, nt):
    i = pl.program_id(1)

    @pl.when(i == 0)
    def _():
        cf_ref[...] = jnp.zeros(cf_ref.shape, F32)
        cbk_ref[...] = jnp.zeros(cbk_ref.shape, F32)

    nblk = af_ref.shape[0]

    a, u = _lru_inputs(xf_ref, pf_ref, nf_ref, i == 0, i == nt - 1, cw_ref, cb_ref, wr_ref, br_ref, wi_ref,
                       bi_ref, lam_ref, 0)
    a3, u3 = _block_scan(a, u, reverse=False)
    af_ref[...] = a3.reshape(af_ref.shape)
    uf_ref[...] = u3.reshape(uf_ref.shape)

    def fwd_step(k, carry):
        ak = af_ref[k]
        uk = uf_ref[k]
        h0 = ak[0:8] * carry + uk[0:8]
        h1 = ak[8:16] * jnp.broadcast_to(h0[7:8, :], h0.shape) + uk[8:16]
        hf_ref[0, pl.ds(pl.multiple_of(k * 16, 16), 16), :] = jnp.concatenate([h0, h1], 0).astype(hf_ref.dtype)
        return jnp.broadcast_to(h1[7:8, :], h1.shape)

    cf_ref[...] = lax.fori_loop(0, nblk, fwd_step, cf_ref[...])

    a, u = _lru_inputs(xb_ref, pb_ref, nb_ref, i == nt - 1, i == 0, cw_ref, cb_ref, wr_ref, br_ref, wi_ref,
                       bi_ref, lam_ref, 1)
    a3, u3 = _block_scan(a, u, reverse=True)
    af_ref[...] = a3.reshape(af_ref.shape)
    uf_ref[...] = u3.reshape(uf_ref.shape)

    def bwd_step(k, carry):
        kk = nblk - 1 - k
        ak = af_ref[kk]
        uk = uf_ref[kk]
        h1 = ak[8:16] * carry + uk[8:16]
        h0 = ak[0:8] * jnp.broadcast_to(h1[0:1, :], h1.shape) + uk[0:8]
        hb_ref[0, pl.ds(pl.multiple_of(kk * 16, 16), 16), :] = jnp.concatenate([h0, h1], 0).astype(hb_ref.dtype)
        return jnp.broadcast_to(h0[0:1, :], h0.shape)

    cbk_ref[...] = lax.fori_loop(0, nblk, bwd_step, cbk_ref[...])


def _rg_lru(proj, conv_w, conv_b, w_rg, b_rg, w_ig, b_ig, lam):
    bsz, s, _ = proj.shape
    d = D_MODEL
    tm = min(256, s)
    nt = s // tm
    hb = tm // HALO
    nhalo = s // HALO

    def tile_specs(tile_of):
        return [
            pl.BlockSpec((1, tm, d), lambda b, i: (b, tile_of(i), COL_XA)),
            pl.BlockSpec((1, HALO, d), lambda b, i: (b, jnp.maximum(tile_of(i) * hb - 1, 0), COL_XA)),
            pl.BlockSpec((1, HALO, d), lambda b, i: (b, jnp.minimum((tile_of(i) + 1) * hb, nhalo - 1), COL_XA)),
        ]

    const2 = lambda b, i: (0, 0)
    const4 = lambda b, i: (0, 0, 0, 0)
    out_sds = jax.ShapeDtypeStruct((bsz, s, d), BF16)
    return pl.pallas_call(
        functools.partial(_lru_kernel, nt=nt),
        grid=(bsz, nt),
        in_specs=tile_specs(lambda i: i) + tile_specs(lambda i: nt - 1 - i) + [
            pl.BlockSpec(conv_w.shape, const2),
            pl.BlockSpec((1, d), const2),
            pl.BlockSpec(w_rg.shape, const4),
            pl.BlockSpec(b_rg.shape, const2),
            pl.BlockSpec(w_ig.shape, const4),
            pl.BlockSpec(b_ig.shape, const2),
            pl.BlockSpec(lam.shape, const2),
        ],
        out_specs=[
            pl.BlockSpec((1, tm, d), lambda b, i: (b, i, 0)),
            pl.BlockSpec((1, tm, d), lambda b, i: (b, nt - 1 - i, 0)),
        ],
        out_shape=[out_sds, out_sds],
        scratch_shapes=[
            pltpu.VMEM((8, d), F32),
            pltpu.VMEM((8, d), F32),
            pltpu.VMEM((tm // 16, 16, d), F32),
            pltpu.VMEM((tm // 16, 16, d), F32),
        ],
        compiler_params=_cparams(("parallel", "arbitrary")),
        name="rg_lru",
    )(proj, proj, proj, proj, proj, proj, conv_w, conv_b.reshape(1, d), w_rg, b_rg, w_ig, b_ig, lam)


def _merge_kernel(hf_ref, hb_ref, ga_ref, yb_ref, xc_ref, bc_ref, cc_ref, gc_ref, xcp_ref, ccp_ref, xcn_ref,
                  ccn_ref, mg0_ref, mg1_ref, mg2_ref, x_ref, mod_ref, cw_ref, wb_ref, wo_ref, o_ref, *, nt):
    i = pl.program_id(1)
    ya = (hf_ref[0].astype(F32) + hb_ref[0].astype(F32)) * _silu(ga_ref[0].astype(F32))

    z = cc_ref[0].astype(F32) * xc_ref[0].astype(F32)
    z_prev = jnp.where(i == 0, 0.0, ccp_ref[0].astype(F32) * xcp_ref[0].astype(F32))
    z_next = jnp.where(i == nt - 1, 0.0, ccn_ref[0].astype(F32) * xcn_ref[0].astype(F32))
    conv = (cw_ref[0:1, :] * _shift_rows(z, z_prev, 1) + cw_ref[1:2, :] * z
            + cw_ref[2:3, :] * _shift_rows(z, z_next, -1))
    yc = bc_ref[0].astype(F32) * conv * _silu(gc_ref[0].astype(F32))

    merged = _sigmoid(mg0_ref[0].astype(F32)) * jnp.dot(ya.astype(BF16), wb_ref[0], preferred_element_type=F32)
    merged += _sigmoid(mg1_ref[0].astype(F32)) * jnp.dot(yb_ref[0], wb_ref[1], preferred_element_type=F32)
    merged += _sigmoid(mg2_ref[0].astype(F32)) * jnp.dot(yc.astype(BF16), wb_ref[2], preferred_element_type=F32)
    out = jnp.dot(merged.astype(BF16), wo_ref[...], preferred_element_type=F32)
    o_ref[0] = x_ref[0] + mod_ref[0, 2:3, :] * out


def _merge(hf, hb, yb, proj, x, mod, conv_c_w, w_branch, w_out):
    bsz, s, d = x.shape
    tm = min(256, s)
    nt = s // tm
    hbk = tm // HALO
    nhalo = s // HALO

    def col(c):
        return pl.BlockSpec((1, tm, d), lambda b, i: (b, i, c))

    def halo_prev(c):
        return pl.BlockSpec((1, HALO, d), lambda b, i: (b, jnp.maximum(i * hbk - 1, 0), c))

    def halo_next(c):
        return pl.BlockSpec((1, HALO, d), lambda b, i: (b, jnp.minimum((i + 1) * hbk, nhalo - 1), c))

    return pl.pallas_call(
        functools.partial(_merge_kernel, nt=nt),
        grid=(bsz, nt),
        in_specs=[
            col(0), col(0), col(COL_GA), col(0), col(COL_XC), col(COL_BC), col(COL_CC), col(COL_GC),
            halo_prev(COL_XC), halo_prev(COL_CC), halo_next(COL_XC), halo_next(COL_CC),
            col(COL_MG), col(COL_MG + 1), col(COL_MG + 2),
            col(0),
            pl.BlockSpec((1, 3, d), lambda b, i: (b, 0, 0)),
            pl.BlockSpec(conv_c_w.shape, lambda b, i: (0, 0)),
            pl.BlockSpec(w_branch.shape, lambda b, i: (0, 0, 0)),
            pl.BlockSpec(w_out.shape, lambda b, i: (0, 0)),
        ],
        out_specs=pl.BlockSpec((1, tm, d), lambda b, i: (b, i, 0)),
        out_shape=jax.ShapeDtypeStruct((bsz, s, d), F32),
        compiler_params=_cparams(("parallel", "parallel")),
        name="merge_outproj",
    )(hf, hb, proj, yb, proj, proj, proj, proj, proj, proj, proj, proj, proj, proj, proj, x, mod,
      conv_c_w, w_branch, w_out)


def _permute_w_in(w):
    d = D_MODEL
    o_q = 2 * d
    o_k = o_q + d
    o_v = o_k + KV_W
    o_gb = o_v + KV_W
    o_xc = o_gb + d
    return jnp.concatenate([w[:, :o_q], w[:, o_q:o_k], w[:, o_gb:o_xc], w[:, o_xc:], w[:, o_k:o_gb]], axis=1)


def _attn_tiles(s):
    tq = min(512, s)
    tk = min(512, s)
    return tq, tk


def kernel(x, c, w_ada, b_ada, norm_g, w_in, conv_a_w, conv_a_b, w_rg, b_rg, w_ig, b_ig, lru_lam, q_norm_g,
           k_norm_g, conv_c_w, w_branch, w_out):
    bsz, s, d = x.shape
    depth = w_ada.shape[0]
    tq, tk = _attn_tiles(s)
    cos_t, sin_t = _rope_tables(s)
    mods = _modulation(c, w_ada, b_ada)
    for l in range(depth):
        mod = mods[l]
        proj = _in_projection(x, mod, norm_g[l], _permute_w_in(w_in[l]).astype(BF16))
        hf, hb = _rg_lru(proj, conv_a_w[l], conv_a_b[l], w_rg[l].astype(BF16), b_rg[l], w_ig[l].astype(BF16),
                         b_ig[l], lru_lam[l])
        qt, kk, vt = _attn_prep(proj, cos_t, sin_t, q_norm_g[l], k_norm_g[l], tk)
        bounded = (_score_bound(q_norm_g[l], k_norm_g[l]) <= SCORE_BOUND_NOSHIFT).astype(jnp.int32).reshape(1)
        yb = _attention(qt, kk, vt, proj, bounded, tq)
        x = _merge(hf, hb, yb, proj, x, mod, conv_c_w[l], w_branch[l].astype(BF16), w_out[l].astype(BF16))
    return x
```

```python
import functools
import math

import numpy as np
import jax
import jax.numpy as jnp
from jax import lax
from jax.experimental import pallas as pl
from jax.experimental.pallas import tpu as pltpu

D_MODEL = 1024
EPS = 1e-6
GRID_W = 64
N_HEADS = 8
N_KV = 2
GROUP = N_HEADS // N_KV
HEAD_DIM = 128
RNN_BLOCKS = 4
RNN_BW = D_MODEL // RNN_BLOCKS
LRU_C = 8.0
ROPE_THETA = 10000.0
N_BRANCH = 3
LOG2E = 1.4426950408889634

COL_XA, COL_GA, COL_Q, COL_GB, COL_XC, COL_BC, COL_CC, COL_GC, COL_MG = 0, 1, 2, 3, 4, 5, 6, 7, 8
COL_KV = 11
N_IN = 11 * D_MODEL + 2 * N_KV * HEAD_DIM
KV_W = N_KV * HEAD_DIM

HALO = 16
VMEM_LIMIT = 56 * 1024 * 1024

F32 = jnp.float32
BF16 = jnp.bfloat16


def _sigmoid(x):
    return 1.0 / (1.0 + jnp.exp(-x))


def _silu(x):
    return x * _sigmoid(x)


def _cparams(sem):
    return pltpu.CompilerParams(dimension_semantics=sem, vmem_limit_bytes=VMEM_LIMIT)


def _mod_kernel(c_ref, w_ref, b_ref, o_ref):
    acc = jnp.dot(c_ref[...], w_ref[0], preferred_element_type=F32, precision=lax.Precision.HIGHEST)
    o_ref[0] = acc + b_ref[0]


def _modulation(c, w_ada, b_ada):
    depth, d, n = w_ada.shape
    bsz = c.shape[0]
    rows = 8
    c8 = jnp.zeros((rows, d), F32).at[:bsz].set(c)
    tn = 1024
    out = pl.pallas_call(
        _mod_kernel,
        grid=(depth, n // tn),
        in_specs=[
            pl.BlockSpec((rows, d), lambda l, j: (0, 0)),
            pl.BlockSpec((1, d, tn), lambda l, j: (l, 0, j)),
            pl.BlockSpec((1, 1, tn), lambda l, j: (l, 0, j)),
        ],
        out_specs=pl.BlockSpec((1, rows, tn), lambda l, j: (l, 0, j)),
        out_shape=jax.ShapeDtypeStruct((depth, rows, n), F32),
        compiler_params=_cparams(("parallel", "parallel")),
        name="adaln_mod",
    )(c8, w_ada, b_ada.reshape(depth, 1, n))
    return out[:, :bsz].reshape(depth, bsz, 3, d)


def _inproj_kernel(x_ref, mod_ref, g_ref, w_ref, o_ref, h_ref):
    @pl.when(pl.program_id(2) == 0)
    def _():
        xv = x_ref[0]
        ms = jnp.mean(xv * xv, axis=-1, keepdims=True)
        y = xv * lax.rsqrt(ms + EPS) * g_ref[...]
        shift = mod_ref[0, 0:1, :]
        scale = mod_ref[0, 1:2, :]
        h_ref[...] = (y * (1.0 + scale) + shift).astype(BF16)

    o_ref[0] = jnp.dot(h_ref[...], w_ref[...], preferred_element_type=F32).astype(BF16)


def _in_projection(x, mod, norm_g, w_in_p):
    bsz, s, d = x.shape
    tm = min(2048, s)
    tn = 512
    return pl.pallas_call(
        _inproj_kernel,
        grid=(bsz, s // tm, N_IN // tn),
        in_specs=[
            pl.BlockSpec((1, tm, d), lambda b, i, j: (b, i, 0)),
            pl.BlockSpec((1, 3, d), lambda b, i, j: (b, 0, 0)),
            pl.BlockSpec((1, d), lambda b, i, j: (0, 0)),
            pl.BlockSpec((d, tn), lambda b, i, j: (0, j)),
        ],
        out_specs=pl.BlockSpec((1, tm, tn), lambda b, i, j: (b, i, j)),
        out_shape=jax.ShapeDtypeStruct((bsz, s, N_IN), BF16),
        scratch_shapes=[pltpu.VMEM((tm, d), BF16)],
        compiler_params=_cparams(("parallel", "parallel", "arbitrary")),
        name="norm_inproj",
    )(x, mod, norm_g.reshape(1, d), w_in_p)


def _rope_tables(s):
    half = HEAD_DIM // 2
    inv = ROPE_THETA ** (-np.arange(0, half, 2, dtype=np.float64) / half)
    rows = s // GRID_W
    ang_r = np.arange(rows, dtype=np.float64)[:, None] * inv
    ang_c = np.arange(GRID_W, dtype=np.float64)[:, None] * inv
    cr, sr = np.cos(ang_r), np.sin(ang_r)
    cc, sc = np.cos(ang_c), np.sin(ang_c)
    cos_r = jnp.asarray(np.concatenate([cr, cr], -1), F32)
    sin_r = jnp.asarray(np.concatenate([-sr, sr], -1), F32)
    cos_c = jnp.asarray(np.concatenate([cc, cc], -1), F32)
    sin_c = jnp.asarray(np.concatenate([-sc, sc], -1), F32)
    cos_t = jnp.concatenate([jnp.repeat(cos_r, GRID_W, axis=0), jnp.tile(cos_c, (rows, 1))], -1)
    sin_t = jnp.concatenate([jnp.repeat(sin_r, GRID_W, axis=0), jnp.tile(sin_c, (rows, 1))], -1)
    return cos_t, sin_t


def _norm_rope(xh, g, cos_t, sin_t, lane_lo):
    ms = jnp.mean(xh * xh, axis=-1, keepdims=True)
    y = xh * lax.rsqrt(ms + EPS) * g
    partner = jnp.where(lane_lo, pltpu.roll(y, HEAD_DIM - 32, axis=1), pltpu.roll(y, 32, axis=1))
    return y * cos_t + partner * sin_t


def _prep_kernel(q_ref, kv_ref, cos_ref, sin_ref, qg_ref, kg_ref, qt_ref, k_ref, vt_ref):
    cos_t = cos_ref[...]
    sin_t = sin_ref[...]
    lane = lax.broadcasted_iota(jnp.int32, cos_t.shape, 1)
    lane_lo = (lane % 64) < 32
    q_scale = (HEAD_DIM ** -0.5) * LOG2E
    for h in range(N_HEADS):
        qh = q_ref[0, :, h * HEAD_DIM:(h + 1) * HEAD_DIM].astype(F32)
        qh = _norm_rope(qh, qg_ref[...], cos_t, sin_t, lane_lo) * q_scale
        qt_ref[0, h] = qh.T.astype(BF16)
    for h in range(N_KV):
        kh = kv_ref[0, :, h * HEAD_DIM:(h + 1) * HEAD_DIM].astype(F32)
        k_ref[0, h, 0] = _norm_rope(kh, kg_ref[...], cos_t, sin_t, lane_lo).astype(BF16)
        vh = kv_ref[0, :, KV_W + h * HEAD_DIM:KV_W + (h + 1) * HEAD_DIM].astype(F32)
        vt_ref[0, h, 0] = vh.T.astype(BF16)


def _attn_prep(proj, cos_t, sin_t, q_g, k_g, tk):
    bsz, s, _ = proj.shape
    nkb = s // tk
    return pl.pallas_call(
        _prep_kernel,
        grid=(bsz, nkb),
        in_specs=[
            pl.BlockSpec((1, tk, D_MODEL), lambda b, i: (b, i, COL_Q)),
            pl.BlockSpec((1, tk, 2 * KV_W), lambda b, i: (b, i, COL_KV * D_MODEL // (2 * KV_W))),
            pl.BlockSpec((tk, HEAD_DIM), lambda b, i: (i, 0)),
            pl.BlockSpec((tk, HEAD_DIM), lambda b, i: (i, 0)),
            pl.BlockSpec((1, HEAD_DIM), lambda b, i: (0, 0)),
            pl.BlockSpec((1, HEAD_DIM), lambda b, i: (0, 0)),
        ],
        out_specs=[
            pl.BlockSpec((1, N_HEADS, HEAD_DIM, tk), lambda b, i: (b, 0, 0, i)),
            pl.BlockSpec((1, N_KV, 1, tk, HEAD_DIM), lambda b, i: (b, 0, i, 0, 0)),
            pl.BlockSpec((1, N_KV, 1, HEAD_DIM, tk), lambda b, i: (b, 0, i, 0, 0)),
        ],
        out_shape=[
            jax.ShapeDtypeStruct((bsz, N_HEADS, HEAD_DIM, s), BF16),
            jax.ShapeDtypeStruct((bsz, N_KV, nkb, tk, HEAD_DIM), BF16),
            jax.ShapeDtypeStruct((bsz, N_KV, nkb, HEAD_DIM, tk), BF16),
        ],
        compiler_params=_cparams(("parallel", "parallel")),
        name="attn_prep",
    )(proj, proj, cos_t, sin_t, q_g.reshape(1, HEAD_DIM), k_g.reshape(1, HEAD_DIM))


SCORE_BOUND_NOSHIFT = 40.0


def _score_bound(q_g, k_g):
    q_scale = (HEAD_DIM ** -0.5) * LOG2E
    return HEAD_DIM * q_scale * jnp.max(jnp.abs(q_g)) * jnp.max(jnp.abs(k_g))


KV_BLOCKS_PER_TRIP = 8


def _largest_divisor(n, cap):
    return max(d for d in range(1, cap + 1) if n % d == 0)


def _attn_kernel(flag_ref, qt_ref, k_ref, vt_ref, gb_ref, o_ref, m_ref, l_ref, acc_ref, s_ref, *, nkb):
    tq = acc_ref.shape[-1]
    l_ref[...] = jnp.zeros(l_ref.shape, F32)
    acc_ref[...] = jnp.zeros(acc_ref.shape, F32)
    bounded = flag_ref[0] != 0

    @pl.when(bounded)
    def _():
        def scores(j, h):
            return jnp.dot(k_ref[0, 0, j], qt_ref[0, h], preferred_element_type=F32)

        s_ref[...] = scores(0, 0)

        group = _largest_divisor(nkb, KV_BLOCKS_PER_TRIP)

        def kv_group(g, carry):
            s_cur = s_ref[...]
            for u in range(group):
                j = g * group + u
                vblk = vt_ref[0, 0, j]
                for h in range(GROUP):
                    s_next = scores(j, h + 1) if h + 1 < GROUP else scores(jnp.minimum(j + 1, nkb - 1), 0)
                    p = jnp.exp2(s_cur)
                    l_ref[h] += jnp.sum(p.reshape(-1, 8, tq), axis=0)
                    acc_ref[h] += jnp.dot(vblk, p.astype(BF16), preferred_element_type=F32)
                    s_cur = s_next
            s_ref[...] = s_cur
            return carry

        lax.fori_loop(0, nkb // group, kv_group, 0)

    @pl.when(jnp.logical_not(bounded))
    def _():
        m_ref[...] = jnp.full(m_ref.shape, -jnp.inf, F32)

        def kv_step(j, carry):
            kblk = k_ref[0, 0, j]
            vblk = vt_ref[0, 0, j]
            for h in range(GROUP):
                s = jnp.dot(kblk, qt_ref[0, h], preferred_element_type=F32)
                m_old = m_ref[h]
                m_new = jnp.maximum(m_old, jnp.max(s, axis=0, keepdims=True))
                alpha = jnp.exp2(m_old - m_new)
                p = jnp.exp2(s - m_new)
                l_ref[h] = alpha * l_ref[h] + jnp.sum(p.reshape(-1, 8, tq), axis=0)
                acc_ref[h] = alpha * acc_ref[h] + jnp.dot(vblk, p.astype(BF16), preferred_element_type=F32)
                m_ref[h] = m_new
            return carry

        lax.fori_loop(0, nkb, kv_step, 0)

    for h in range(GROUP):
        l = jnp.sum(l_ref[h], axis=0, keepdims=True)
        o = acc_ref[h] * (1.0 / l)
        gate = _silu(gb_ref[0, :, h * HEAD_DIM:(h + 1) * HEAD_DIM].astype(F32))
        o_ref[0, :, h * HEAD_DIM:(h + 1) * HEAD_DIM] = (o.T * gate).astype(BF16)


def _attention(qt, kk, vt, proj, bounded_flag, tq):
    bsz, _, _, s = qt.shape
    nkb, tk = kk.shape[2], kk.shape[3]
    gw = GROUP * HEAD_DIM
    grid_spec = pltpu.PrefetchScalarGridSpec(
        num_scalar_prefetch=1,
        grid=(bsz, N_KV, s // tq),
        in_specs=[
            pl.BlockSpec((1, GROUP, HEAD_DIM, tq), lambda b, g, i, f: (b, g, 0, i)),
            pl.BlockSpec((1, 1, nkb, tk, HEAD_DIM), lambda b, g, i, f: (b, g, 0, 0, 0)),
            pl.BlockSpec((1, 1, nkb, HEAD_DIM, tk), lambda b, g, i, f: (b, g, 0, 0, 0)),
            pl.BlockSpec((1, tq, gw), lambda b, g, i, f: (b, i, COL_GB * D_MODEL // gw + g)),
        ],
        out_specs=pl.BlockSpec((1, tq, gw), lambda b, g, i, f: (b, i, g)),
        scratch_shapes=[
            pltpu.VMEM((GROUP, 1, tq), F32),
            pltpu.VMEM((GROUP, 8, tq), F32),
            pltpu.VMEM((GROUP, HEAD_DIM, tq), F32),
            pltpu.VMEM((tk, tq), F32),
        ],
    )
    return pl.pallas_call(
        functools.partial(_attn_kernel, nkb=nkb),
        grid_spec=grid_spec,
        out_shape=jax.ShapeDtypeStruct((bsz, s, D_MODEL), BF16),
        compiler_params=_cparams(("parallel", "parallel", "arbitrary")),
        name="gqa_attention",
    )(bounded_flag, qt, kk, vt, proj)


def _shift_rows(tile, halo, shift):
    t = tile.shape[0]
    row8 = lax.broadcasted_iota(jnp.int32, (8, tile.shape[1]), 0)
    if shift > 0:
        rolled = pltpu.roll(tile, shift, axis=0)
        edge = pltpu.roll(halo, shift, axis=0)[0:8]
        fixed = jnp.where(row8 < shift, edge, rolled[0:8])
        return jnp.concatenate([fixed, rolled[8:]], axis=0)
    k = -shift
    rolled = pltpu.roll(tile, t - k, axis=0)
    edge = pltpu.roll(halo, HALO - k, axis=0)[HALO - 8:HALO]
    fixed = jnp.where(row8 >= 8 - k, edge, rolled[t - 8:t])
    return jnp.concatenate([rolled[:t - 8], fixed], axis=0)


def _lru_inputs(x_ref, prev_ref, next_ref, first, last, cw_ref, cb_ref, wr_ref, br_ref, wi_ref, bi_ref,
                lam_ref, d):
    tile = x_ref[0].astype(F32)
    prev = jnp.where(first, 0.0, prev_ref[0].astype(F32))
    nxt = jnp.where(last, 0.0, next_ref[0].astype(F32))
    ua = (cw_ref[0:1, :] * _shift_rows(tile, prev, 2)
          + cw_ref[1:2, :] * _shift_rows(tile, prev, 1)
          + cw_ref[2:3, :] * tile
          + cw_ref[3:4, :] * _shift_rows(tile, nxt, -1)
          + cb_ref[...])
    ub = ua.astype(BF16)
    r_parts, i_parts = [], []
    for n in range(RNN_BLOCKS):
        blk = ub[:, n * RNN_BW:(n + 1) * RNN_BW]
        r_parts.append(jnp.dot(blk, wr_ref[d, n], preferred_element_type=F32))
        i_parts.append(jnp.dot(blk, wi_ref[d, n], preferred_element_type=F32))
    r = _sigmoid(jnp.concatenate(r_parts, axis=-1) + br_ref[d:d + 1, :])
    ig = _sigmoid(jnp.concatenate(i_parts, axis=-1) + bi_ref[d:d + 1, :])
    nl = -lam_ref[d:d + 1, :]
    softplus = jnp.maximum(nl, 0.0) + jnp.log1p(jnp.exp(-jnp.abs(nl)))
    log_a = (-LRU_C) * r * softplus
    a = jnp.exp(log_a)
    u = jnp.sqrt(-jnp.tanh(log_a) * (1.0 + a * a)) * (ig * ua)
    return a, u


def _block_scan(a, u, reverse):
    t, c = a.shape
    a3 = a.reshape(t // 8, 8, c)
    u3 = u.reshape(t // 8, 8, c)
    pos = lax.broadcasted_iota(jnp.int32, a3.shape, 1)
    for dist in (1, 2, 4):
        if reverse:
            sh = 8 - dist
            valid = pos < 8 - dist
        else:
            sh = dist
            valid = pos >= dist
        a_sh = jnp.where(valid, pltpu.roll(a3, sh, axis=1), 1.0)
        u_sh = jnp.where(valid, pltpu.roll(u3, sh, axis=1), 0.0)
        u3 = a3 * u_sh + u3
        a3 = a3 * a_sh
    return a3, u3


def _lru_kernel(xf_ref, pf_ref, nf_ref, xb_ref, pb_ref, nb_ref, cw_ref, cb_ref, wr_ref, br_ref, wi_ref,
                bi_ref, lam_ref, hf_ref, hb_ref, cf_ref, cbk_ref, af_ref, uf_ref, *, nt):
    i = pl.program_id(1)

    @pl.when(i == 0)
    def _():
        cf_ref[...] = jnp.zeros(cf_ref.shape, F32)
        cbk_ref[...] = jnp.zeros(cbk_ref.shape, F32)

    nblk = af_ref.shape[0]

    a, u = _lru_inputs(xf_ref, pf_ref, nf_ref, i == 0, i == nt - 1, cw_ref, cb_ref, wr_ref, br_ref, wi_ref,
                       bi_ref, lam_ref, 0)
    a3, u3 = _block_scan(a, u, reverse=False)
    af_ref[...] = a3.reshape(af_ref.shape)
    uf_ref[...] = u3.reshape(uf_ref.shape)

    def fwd_step(k, carry):
        ak = af_ref[k]
        uk = uf_ref[k]
        h0 = ak[0:8] * carry + uk[0:8]
        h1 = ak[8:16] * jnp.broadcast_to(h0[7:8, :], h0.shape) + uk[8:16]
        hf_ref[0, pl.ds(pl.multiple_of(k * 16, 16), 16), :] = jnp.concatenate([h0, h1], 0).astype(hf_ref.dtype)
        return jnp.broadcast_to(h1[7:8, :], h1.shape)

    cf_ref[...] = lax.fori_loop(0, nblk, fwd_step, cf_ref[...])

    a, u = _lru_inputs(xb_ref, pb_ref, nb_ref, i == nt - 1, i == 0, cw_ref, cb_ref, wr_ref, br_ref, wi_ref,
                       bi_ref, lam_ref, 1)
    a3, u3 = _block_scan(a, u, reverse=True)
    af_ref[...] = a3.reshape(af_ref.shape)
    uf_ref[...] = u3.reshape(uf_ref.shape)

    def bwd_step(k, carry):
        kk = nblk - 1 - k
        ak = af_ref[kk]
        uk = uf_ref[kk]
        h1 = ak[8:16] * carry + uk[8:16]
        h0 = ak[0:8] * jnp.broadcast_to(h1[0:1, :], h1.shape) + uk[0:8]
        hb_ref[0, pl.ds(pl.multiple_of(kk * 16, 16), 16), :] = jnp.concatenate([h0, h1], 0).astype(hb_ref.dtype)
        return jnp.broadcast_to(h0[0:1, :], h0.shape)

    cbk_ref[...] = lax.fori_loop(0, nblk, bwd_step, cbk_ref[...])


def _rg_lru(proj, conv_w, conv_b, w_rg, b_rg, w_ig, b_ig, lam):
    bsz, s, _ = proj.shape
    d = D_MODEL
    tm = min(256, s)
    nt = s // tm
    hb = tm // HALO
    nhalo = s // HALO

    def tile_specs(tile_of):
        return [
            pl.BlockSpec((1, tm, d), lambda b, i: (b, tile_of(i), COL_XA)),
            pl.BlockSpec((1, HALO, d), lambda b, i: (b, jnp.maximum(tile_of(i) * hb - 1, 0), COL_XA)),
            pl.BlockSpec((1, HALO, d), lambda b, i: (b, jnp.minimum((tile_of(i) + 1) * hb, nhalo - 1), COL_XA)),
        ]

    const2 = lambda b, i: (0, 0)
    const4 = lambda b, i: (0, 0, 0, 0)
    out_sds = jax.ShapeDtypeStruct((bsz, s, d), BF16)
    return pl.pallas_call(
        functools.partial(_lru_kernel, nt=nt),
        grid=(bsz, nt),
        in_specs=tile_specs(lambda i: i) + tile_specs(lambda i: nt - 1 - i) + [
            pl.BlockSpec(conv_w.shape, const2),
            pl.BlockSpec((1, d), const2),
            pl.BlockSpec(w_rg.shape, const4),
            pl.BlockSpec(b_rg.shape, const2),
            pl.BlockSpec(w_ig.shape, const4),
            pl.BlockSpec(b_ig.shape, const2),
            pl.BlockSpec(lam.shape, const2),
        ],
        out_specs=[
            pl.BlockSpec((1, tm, d), lambda b, i: (b, i, 0)),
            pl.BlockSpec((1, tm, d), lambda b, i: (b, nt - 1 - i, 0)),
        ],
        out_shape=[out_sds, out_sds],
        scratch_shapes=[
            pltpu.VMEM((8, d), F32),
            pltpu.VMEM((8, d), F32),
            pltpu.VMEM((tm // 16, 16, d), F32),
            pltpu.VMEM((tm // 16, 16, d), F32),
        ],
        compiler_params=_cparams(("parallel", "arbitrary")),
        name="rg_lru",
    )(proj, proj, proj, proj, proj, proj, conv_w, conv_b.reshape(1, d), w_rg, b_rg, w_ig, b_ig, lam)


def _merge_kernel(hf_ref, hb_ref, ga_ref, yb_ref, xc_ref, bc_ref, cc_ref, gc_ref, xcp_ref, ccp_ref, xcn_ref,
                  ccn_ref, mg0_ref, mg1_ref, mg2_ref, x_ref, mod_ref, cw_ref, wb_ref, wo_ref, o_ref, *, nt):
    i = pl.program_id(1)
    merged = _sigmoid(mg1_ref[0].astype(F32)) * jnp.dot(yb_ref[0], wb_ref[1], preferred_element_type=F32)

    ya = (hf_ref[0].astype(F32) + hb_ref[0].astype(F32)) * _silu(ga_ref[0].astype(F32))
    merged += _sigmoid(mg0_ref[0].astype(F32)) * jnp.dot(ya.astype(BF16), wb_ref[0], preferred_element_type=F32)

    z = cc_ref[0].astype(F32) * xc_ref[0].astype(F32)
    z_prev = jnp.where(i == 0, 0.0, ccp_ref[0].astype(F32) * xcp_ref[0].astype(F32))
    z_next = jnp.where(i == nt - 1, 0.0, ccn_ref[0].astype(F32) * xcn_ref[0].astype(F32))
    conv = (cw_ref[0:1, :] * _shift_rows(z, z_prev, 1) + cw_ref[1:2, :] * z
            + cw_ref[2:3, :] * _shift_rows(z, z_next, -1))
    yc = bc_ref[0].astype(F32) * conv * _silu(gc_ref[0].astype(F32))

    merged += _sigmoid(mg2_ref[0].astype(F32)) * jnp.dot(yc.astype(BF16), wb_ref[2], preferred_element_type=F32)
    out = jnp.dot(merged.astype(BF16), wo_ref[...], preferred_element_type=F32)
    o_ref[0] = x_ref[0] + mod_ref[0, 2:3, :] * out


def _merge(hf, hb, yb, proj, x, mod, conv_c_w, w_branch, w_out):
    bsz, s, d = x.shape
    tm = min(256, s)
    nt = s // tm
    hbk = tm // HALO
    nhalo = s // HALO

    def col(c):
        return pl.BlockSpec((1, tm, d), lambda b, i: (b, i, c))

    def halo_prev(c):
        return pl.BlockSpec((1, HALO, d), lambda b, i: (b, jnp.maximum(i * hbk - 1, 0), c))

    def halo_next(c):
        return pl.BlockSpec((1, HALO, d), lambda b, i: (b, jnp.minimum((i + 1) * hbk, nhalo - 1), c))

    return pl.pallas_call(
        functools.partial(_merge_kernel, nt=nt),
        grid=(bsz, nt),
        in_specs=[
            col(0), col(0), col(COL_GA), col(0), col(COL_XC), col(COL_BC), col(COL_CC), col(COL_GC),
            halo_prev(COL_XC), halo_prev(COL_CC), halo_next(COL_XC), halo_next(COL_CC),
            col(COL_MG), col(COL_MG + 1), col(COL_MG + 2),
            col(0),
            pl.BlockSpec((1, 3, d), lambda b, i: (b, 0, 0)),
            pl.BlockSpec(conv_c_w.shape, lambda b, i: (0, 0)),
            pl.BlockSpec(w_branch.shape, lambda b, i: (0, 0, 0)),
            pl.BlockSpec(w_out.shape, lambda b, i: (0, 0)),
        ],
        out_specs=pl.BlockSpec((1, tm, d), lambda b, i: (b, i, 0)),
        out_shape=jax.ShapeDtypeStruct((bsz, s, d), F32),
        compiler_params=_cparams(("parallel", "parallel")),
        name="merge_outproj",
    )(hf, hb, proj, yb, proj, proj, proj, proj, proj, proj, proj, proj, proj, proj, proj, x, mod,
      conv_c_w, w_branch, w_out)


def _permute_w_in(w):
    d = D_MODEL
    o_q = 2 * d
    o_k = o_q + d
    o_v = o_k + KV_W
    o_gb = o_v + KV_W
    o_xc = o_gb + d
    return jnp.concatenate([w[:, :o_q], w[:, o_q:o_k], w[:, o_gb:o_xc], w[:, o_xc:], w[:, o_k:o_gb]], axis=1)


def _attn_tiles(s):
    tq = min(512, s)
    tk = min(512, s)
    return tq, tk


def kernel(x, c, w_ada, b_ada, norm_g, w_in, conv_a_w, conv_a_b, w_rg, b_rg, w_ig, b_ig, lru_lam, q_norm_g,
           k_norm_g, conv_c_w, w_branch, w_out):
    bsz, s, d = x.shape
    depth = w_ada.shape[0]
    tq, tk = _attn_tiles(s)
    cos_t, sin_t = _rope_tables(s)
    mods = _modulation(c, w_ada, b_ada)
    for l in range(depth):
        mod = mods[l]
        proj = _in_projection(x, mod, norm_g[l], _permute_w_in(w_in[l]).astype(BF16))
        hf, hb = _rg_lru(proj, conv_a_w[l], conv_a_b[l], w_rg[l].astype(BF16), b_rg[l], w_ig[l].astype(BF16),
                         b_ig[l], lru_lam[l])
        qt, kk, vt = _attn_prep(proj, cos_t, sin_t, q_norm_g[l], k_norm_g[l], tk)
        bounded = (_score_bound(q_norm_g[l], k_norm_g[l]) <= SCORE_BOUND_NOSHIFT).astype(jnp.int32).reshape(1)
        yb = _attention(qt, kk, vt, proj, bounded, tq)
        x = _merge(hf, hb, yb, proj, x, mod, conv_c_w[l], w_branch[l].astype(BF16), w_out[l].astype(BF16))
    return x
```

```python
import functools
import math

import numpy as np
import jax
import jax.numpy as jnp
from jax import lax
from jax.experimental import pallas as pl
from jax.experimental.pallas import tpu as pltpu

D_MODEL = 1024
EPS = 1e-6
GRID_W = 64
N_HEADS = 8
N_KV = 2
GROUP = N_HEADS // N_KV
HEAD_DIM = 128
RNN_BLOCKS = 4
RNN_BW = D_MODEL // RNN_BLOCKS
LRU_C = 8.0
ROPE_THETA = 10000.0
N_BRANCH = 3
LOG2E = 1.4426950408889634

COL_XA, COL_GA, COL_Q, COL_GB, COL_XC, COL_BC, COL_CC, COL_GC, COL_MG = 0, 1, 2, 3, 4, 5, 6, 7, 8
COL_KV = 11
N_IN = 11 * D_MODEL + 2 * N_KV * HEAD_DIM
KV_W = N_KV * HEAD_DIM

HALO = 16
VMEM_LIMIT = 56 * 1024 * 1024

F32 = jnp.float32
BF16 = jnp.bfloat16


def _sigmoid(x):
    return 1.0 / (1.0 + jnp.exp2(x * (-LOG2E)))


def _silu(x):
    return x * _sigmoid(x)


def _cparams(sem):
    return pltpu.CompilerParams(dimension_semantics=sem, vmem_limit_bytes=VMEM_LIMIT)


def _mod_kernel(c_ref, w_ref, b_ref, o_ref):
    acc = jnp.dot(c_ref[...], w_ref[0], preferred_element_type=F32, precision=lax.Precision.HIGHEST)
    o_ref[0] = acc + b_ref[0]


def _modulation(c, w_ada, b_ada):
    depth, d, n = w_ada.shape
    bsz = c.shape[0]
    rows = 8
    c8 = jnp.zeros((rows, d), F32).at[:bsz].set(c)
    tn = 1024
    out = pl.pallas_call(
        _mod_kernel,
        grid=(depth, n // tn),
        in_specs=[
            pl.BlockSpec((rows, d), lambda l, j: (0, 0)),
            pl.BlockSpec((1, d, tn), lambda l, j: (l, 0, j)),
            pl.BlockSpec((1, 1, tn), lambda l, j: (l, 0, j)),
        ],
        out_specs=pl.BlockSpec((1, rows, tn), lambda l, j: (l, 0, j)),
        out_shape=jax.ShapeDtypeStruct((depth, rows, n), F32),
        compiler_params=_cparams(("parallel", "parallel")),
        name="adaln_mod",
    )(c8, w_ada, b_ada.reshape(depth, 1, n))
    return out[:, :bsz].reshape(depth, bsz, 3, d)


def _inproj_kernel(x_ref, mod_ref, g_ref, w_ref, o_ref, h_ref):
    @pl.when(pl.program_id(2) == 0)
    def _():
        xv = x_ref[0]
        ms = jnp.mean(xv * xv, axis=-1, keepdims=True)
        y = xv * lax.rsqrt(ms + EPS) * g_ref[...]
        shift = mod_ref[0, 0:1, :]
        scale = mod_ref[0, 1:2, :]
        h_ref[...] = (y * (1.0 + scale) + shift).astype(BF16)

    o_ref[0] = jnp.dot(h_ref[...], w_ref[...], preferred_element_type=F32).astype(BF16)


def _in_projection(x, mod, norm_g, w_in_p):
    bsz, s, d = x.shape
    tm = min(2048, s)
    tn = 512
    return pl.pallas_call(
        _inproj_kernel,
        grid=(bsz, s // tm, N_IN // tn),
        in_specs=[
            pl.BlockSpec((1, tm, d), lambda b, i, j: (b, i, 0)),
            pl.BlockSpec((1, 3, d), lambda b, i, j: (b, 0, 0)),
            pl.BlockSpec((1, d), lambda b, i, j: (0, 0)),
            pl.BlockSpec((d, tn), lambda b, i, j: (0, j)),
        ],
        out_specs=pl.BlockSpec((1, tm, tn), lambda b, i, j: (b, i, j)),
        out_shape=jax.ShapeDtypeStruct((bsz, s, N_IN), BF16),
        scratch_shapes=[pltpu.VMEM((tm, d), BF16)],
        compiler_params=_cparams(("parallel", "parallel", "arbitrary")),
        name="norm_inproj",
    )(x, mod, norm_g.reshape(1, d), w_in_p)


def _rope_tables(s):
    half = HEAD_DIM // 2
    inv = ROPE_THETA ** (-np.arange(0, half, 2, dtype=np.float64) / half)
    rows = s // GRID_W
    ang_r = np.arange(rows, dtype=np.float64)[:, None] * inv
    ang_c = np.arange(GRID_W, dtype=np.float64)[:, None] * inv
    cr, sr = np.cos(ang_r), np.sin(ang_r)
    cc, sc = np.cos(ang_c), np.sin(ang_c)
    cos_r = jnp.asarray(np.concatenate([cr, cr], -1), F32)
    sin_r = jnp.asarray(np.concatenate([-sr, sr], -1), F32)
    cos_c = jnp.asarray(np.concatenate([cc, cc], -1), F32)
    sin_c = jnp.asarray(np.concatenate([-sc, sc], -1), F32)
    cos_t = jnp.concatenate([jnp.repeat(cos_r, GRID_W, axis=0), jnp.tile(cos_c, (rows, 1))], -1)
    sin_t = jnp.concatenate([jnp.repeat(sin_r, GRID_W, axis=0), jnp.tile(sin_c, (rows, 1))], -1)
    return cos_t, sin_t


def _norm_rope(xh, g, cos_t, sin_t, lane_lo):
    ms = jnp.mean(xh * xh, axis=-1, keepdims=True)
    y = xh * lax.rsqrt(ms + EPS) * g
    partner = jnp.where(lane_lo, pltpu.roll(y, HEAD_DIM - 32, axis=1), pltpu.roll(y, 32, axis=1))
    return y * cos_t + partner * sin_t


def _prep_kernel(q_ref, kv_ref, cos_ref, sin_ref, qg_ref, kg_ref, qt_ref, k_ref, vt_ref):
    cos_t = cos_ref[...]
    sin_t = sin_ref[...]
    lane = lax.broadcasted_iota(jnp.int32, cos_t.shape, 1)
    lane_lo = (lane % 64) < 32
    q_scale = (HEAD_DIM ** -0.5) * LOG2E
    for h in range(N_HEADS):
        qh = q_ref[0, :, h * HEAD_DIM:(h + 1) * HEAD_DIM].astype(F32)
        qh = _norm_rope(qh, qg_ref[...], cos_t, sin_t, lane_lo) * q_scale
        qt_ref[0, h] = qh.T.astype(BF16)
    for h in range(N_KV):
        kh = kv_ref[0, :, h * HEAD_DIM:(h + 1) * HEAD_DIM].astype(F32)
        k_ref[0, h, 0] = _norm_rope(kh, kg_ref[...], cos_t, sin_t, lane_lo).astype(BF16)
        vh = kv_ref[0, :, KV_W + h * HEAD_DIM:KV_W + (h + 1) * HEAD_DIM].astype(F32)
        vt_ref[0, h, 0] = vh.T.astype(BF16)


def _attn_prep(proj, cos_t, sin_t, q_g, k_g, tk):
    bsz, s, _ = proj.shape
    nkb = s // tk
    return pl.pallas_call(
        _prep_kernel,
        grid=(bsz, nkb),
        in_specs=[
            pl.BlockSpec((1, tk, D_MODEL), lambda b, i: (b, i, COL_Q)),
            pl.BlockSpec((1, tk, 2 * KV_W), lambda b, i: (b, i, COL_KV * D_MODEL // (2 * KV_W))),
            pl.BlockSpec((tk, HEAD_DIM), lambda b, i: (i, 0)),
            pl.BlockSpec((tk, HEAD_DIM), lambda b, i: (i, 0)),
            pl.BlockSpec((1, HEAD_DIM), lambda b, i: (0, 0)),
            pl.BlockSpec((1, HEAD_DIM), lambda b, i: (0, 0)),
        ],
        out_specs=[
            pl.BlockSpec((1, N_HEADS, HEAD_DIM, tk), lambda b, i: (b, 0, 0, i)),
            pl.BlockSpec((1, N_KV, 1, tk, HEAD_DIM), lambda b, i: (b, 0, i, 0, 0)),
            pl.BlockSpec((1, N_KV, 1, HEAD_DIM, tk), lambda b, i: (b, 0, i, 0, 0)),
        ],
        out_shape=[
            jax.ShapeDtypeStruct((bsz, N_HEADS, HEAD_DIM, s), BF16),
            jax.ShapeDtypeStruct((bsz, N_KV, nkb, tk, HEAD_DIM), BF16),
            jax.ShapeDtypeStruct((bsz, N_KV, nkb, HEAD_DIM, tk), BF16),
        ],
        compiler_params=_cparams(("parallel", "parallel")),
        name="attn_prep",
    )(proj, proj, cos_t, sin_t, q_g.reshape(1, HEAD_DIM), k_g.reshape(1, HEAD_DIM))


SCORE_BOUND_NOSHIFT = 40.0


def _score_bound(q_g, k_g):
    q_scale = (HEAD_DIM ** -0.5) * LOG2E
    return HEAD_DIM * q_scale * jnp.max(jnp.abs(q_g)) * jnp.max(jnp.abs(k_g))


KV_BLOCKS_PER_TRIP = 8


def _largest_divisor(n, cap):
    return max(d for d in range(1, cap + 1) if n % d == 0)


def _attn_kernel(flag_ref, qt_ref, k_ref, vt_ref, gb_ref, o_ref, m_ref, l_ref, acc_ref, s_ref, *, nkb):
    tq = acc_ref.shape[-1]
    l_ref[...] = jnp.zeros(l_ref.shape, F32)
    acc_ref[...] = jnp.zeros(acc_ref.shape, F32)
    bounded = flag_ref[0] != 0

    @pl.when(bounded)
    def _():
        def scores(j, h):
            return jnp.dot(k_ref[0, 0, j], qt_ref[0, h], preferred_element_type=F32)

        s_ref[...] = scores(0, 0)

        group = _largest_divisor(nkb, KV_BLOCKS_PER_TRIP)

        def kv_group(g, carry):
            s_cur = s_ref[...]
            for u in range(group):
                j = g * group + u
                vblk = vt_ref[0, 0, j]
                for h in range(GROUP):
                    s_next = scores(j, h + 1) if h + 1 < GROUP else scores(jnp.minimum(j + 1, nkb - 1), 0)
                    p = jnp.exp2(s_cur)
                    l_ref[h] += jnp.sum(p.reshape(-1, 8, tq), axis=0)
                    acc_ref[h] += jnp.dot(vblk, p.astype(BF16), preferred_element_type=F32)
                    s_cur = s_next
            s_ref[...] = s_cur
            return carry

        lax.fori_loop(0, nkb // group, kv_group, 0)

    @pl.when(jnp.logical_not(bounded))
    def _():
        m_ref[...] = jnp.full(m_ref.shape, -jnp.inf, F32)

        def kv_step(j, carry):
            kblk = k_ref[0, 0, j]
            vblk = vt_ref[0, 0, j]
            for h in range(GROUP):
                s = jnp.dot(kblk, qt_ref[0, h], preferred_element_type=F32)
                m_old = m_ref[h]
                m_new = jnp.maximum(m_old, jnp.max(s, axis=0, keepdims=True))
                alpha = jnp.exp2(m_old - m_new)
                p = jnp.exp2(s - m_new)
                l_ref[h] = alpha * l_ref[h] + jnp.sum(p.reshape(-1, 8, tq), axis=0)
                acc_ref[h] = alpha * acc_ref[h] + jnp.dot(vblk, p.astype(BF16), preferred_element_type=F32)
                m_ref[h] = m_new
            return carry

        lax.fori_loop(0, nkb, kv_step, 0)

    for h in range(GROUP):
        l = jnp.sum(l_ref[h], axis=0, keepdims=True)
        o = acc_ref[h] * (1.0 / l)
        gate = _silu(gb_ref[0, :, h * HEAD_DIM:(h + 1) * HEAD_DIM].astype(F32))
        o_ref[0, :, h * HEAD_DIM:(h + 1) * HEAD_DIM] = (o.T * gate).astype(BF16)


def _attention(qt, kk, vt, proj, bounded_flag, tq):
    bsz, _, _, s = qt.shape
    nkb, tk = kk.shape[2], kk.shape[3]
    gw = GROUP * HEAD_DIM
    grid_spec = pltpu.PrefetchScalarGridSpec(
        num_scalar_prefetch=1,
        grid=(bsz, N_KV, s // tq),
        in_specs=[
            pl.BlockSpec((1, GROUP, HEAD_DIM, tq), lambda b, g, i, f: (b, g, 0, i)),
            pl.BlockSpec((1, 1, nkb, tk, HEAD_DIM), lambda b, g, i, f: (b, g, 0, 0, 0)),
            pl.BlockSpec((1, 1, nkb, HEAD_DIM, tk), lambda b, g, i, f: (b, g, 0, 0, 0)),
            pl.BlockSpec((1, tq, gw), lambda b, g, i, f: (b, i, COL_GB * D_MODEL // gw + g)),
        ],
        out_specs=pl.BlockSpec((1, tq, gw), lambda b, g, i, f: (b, i, g)),
        scratch_shapes=[
            pltpu.VMEM((GROUP, 1, tq), F32),
            pltpu.VMEM((GROUP, 8, tq), F32),
            pltpu.VMEM((GROUP, HEAD_DIM, tq), F32),
            pltpu.VMEM((tk, tq), F32),
        ],
    )
    return pl.pallas_call(
        functools.partial(_attn_kernel, nkb=nkb),
        grid_spec=grid_spec,
        out_shape=jax.ShapeDtypeStruct((bsz, s, D_MODEL), BF16),
        compiler_params=_cparams(("parallel", "parallel", "arbitrary")),
        name="gqa_attention",
    )(bounded_flag, qt, kk, vt, proj)


def _shift_rows(tile, halo, shift):
    t = tile.shape[0]
    row8 = lax.broadcasted_iota(jnp.int32, (8, tile.shape[1]), 0)
    if shift > 0:
        rolled = pltpu.roll(tile, shift, axis=0)
        edge = pltpu.roll(halo, shift, axis=0)[0:8]
        fixed = jnp.where(row8 < shift, edge, rolled[0:8])
        return jnp.concatenate([fixed, rolled[8:]], axis=0)
    k = -shift
    rolled = pltpu.roll(tile, t - k, axis=0)
    edge = pltpu.roll(halo, HALO - k, axis=0)[HALO - 8:HALO]
    fixed = jnp.where(row8 >= 8 - k, edge, rolled[t - 8:t])
    return jnp.concatenate([rolled[:t - 8], fixed], axis=0)


def _lru_conv_gates(x_ref, prev_ref, next_ref, first, last, cw_ref, cb_ref, wr_ref, wi_ref, d,
                    ua_ref, rpre_ref, ipre_ref):
    tile = x_ref[0].astype(F32)
    prev = jnp.where(first, 0.0, prev_ref[0].astype(F32))
    nxt = jnp.where(last, 0.0, next_ref[0].astype(F32))
    ua = (cw_ref[0:1, :] * _shift_rows(tile, prev, 2)
          + cw_ref[1:2, :] * _shift_rows(tile, prev, 1)
          + cw_ref[2:3, :] * tile
          + cw_ref[3:4, :] * _shift_rows(tile, nxt, -1)
          + cb_ref[...])
    ua_ref[d] = ua
    ub = ua.astype(BF16)
    for n in range(RNN_BLOCKS):
        cols = slice(n * RNN_BW, (n + 1) * RNN_BW)
        rpre_ref[d, :, cols] = jnp.dot(ub[:, cols], wr_ref[d, n], preferred_element_type=F32)
        ipre_ref[d, :, cols] = jnp.dot(ub[:, cols], wi_ref[d, n], preferred_element_type=F32)


def _lru_coefficients(ua, r_half, i_half, b_r_half, b_i_half, c_half):
    y = c_half + c_half * jnp.tanh(r_half + b_r_half)
    ua_half = 0.5 * ua
    gated = ua_half + ua_half * jnp.tanh(i_half + b_i_half)
    a = jnp.exp2(y * (-LOG2E))
    x = jnp.tanh(y) * (1.0 + a * a)
    root = jnp.where(x > 0.0, x * lax.rsqrt(x), x)
    return a, root * gated


LANES = 128
N_SLABS = D_MODEL // LANES


def _slab_pitch(t):
    return t + 8 if (t // 8) % 2 == 0 else t


def _lru_kernel(xf_ref, pf_ref, nf_ref, xb_ref, pb_ref, nb_ref, cw_ref, cb_ref, wr_ref, br_ref, wi_ref,
                bi_ref, lam_ref, hf_ref, hb_ref, cf_ref, cbk_ref, sf_ref, sb_ref, ua_ref, rpre_ref, ipre_ref,
                chan_ref, *coef_refs, nt):
    i = pl.program_id(1)
    tm = hf_ref.shape[1]
    pitch = _slab_pitch(tm)
    it = jnp.minimum(i, nt - 1)
    coef_sets = (coef_refs[:4], coef_refs[4:])

    @pl.when(i == 0)
    def _():
        cf_ref[...] = jnp.zeros(cf_ref.shape, F32)
        cbk_ref[...] = jnp.zeros(cbk_ref.shape, F32)
        for ref in coef_sets[1]:
            ref[...] = jnp.zeros(ref.shape, F32)

    _lru_conv_gates(xf_ref, pf_ref, nf_ref, it == 0, it == nt - 1, cw_ref, cb_ref, wr_ref, wi_ref, 0,
                    ua_ref, rpre_ref, ipre_ref)
    _lru_conv_gates(xb_ref, pb_ref, nb_ref, it == nt - 1, it == 0, cw_ref, cb_ref, wr_ref, wi_ref, 1,
                    ua_ref, rpre_ref, ipre_ref)
    nl = -lam_ref[...]
    c_half = (0.5 * LRU_C) * (jnp.maximum(nl, 0.0) + jnp.log1p(jnp.exp(-jnp.abs(nl))))
    for d in range(2):
        for n, row in enumerate((0.5 * br_ref[d:d + 1, :], 0.5 * bi_ref[d:d + 1, :], c_half[d:d + 1, :])):
            chan_ref[3 * d + n] = jnp.broadcast_to(row, chan_ref.shape[1:])

    def step(cur, prev):
        af_ref, uf_ref, ab_ref, ub_ref = prev

        def rows8(g, carry):
            hf, hb = carry
            r0 = pl.multiple_of(g * 8, 8)
            rows = pl.ds(r0, 8)
            for d in range(2):
                a, u = _lru_coefficients(ua_ref[d, rows, :], rpre_ref[d, rows, :], ipre_ref[d, rows, :],
                                         chan_ref[3 * d], chan_ref[3 * d + 1], chan_ref[3 * d + 2])
                for c in range(N_SLABS):
                    dst = pl.ds(c * pitch + r0, 8)
                    cur[2 * d][dst, :] = a[:, c * LANES:(c + 1) * LANES]
                    cur[2 * d + 1][dst, :] = u[:, c * LANES:(c + 1) * LANES]
            for j in range(8):
                k = r0 + j
                tf = pl.ds(k, N_SLABS, stride=pitch)
                tb = pl.ds(tm - 1 - k, N_SLABS, stride=pitch)
                hf = af_ref[tf, :] * hf + uf_ref[tf, :]
                sf_ref[tf, :] = hf
                hb = ab_ref[tb, :] * hb + ub_ref[tb, :]
                sb_ref[tb, :] = hb
            return hf, hb

        hf, hb = lax.fori_loop(0, tm // 8, rows8, (cf_ref[...], cbk_ref[...]), unroll=4)
        keep = i > 0
        cf_ref[...] = jnp.where(keep, hf, cf_ref[...])
        cbk_ref[...] = jnp.where(keep, hb, cbk_ref[...])
        for s_ref, out_ref in ((sf_ref, hf_ref), (sb_ref, hb_ref)):
            for c in range(N_SLABS):
                out_ref[0, :, c * LANES:(c + 1) * LANES] = s_ref[pl.ds(c * pitch, tm), :].astype(out_ref.dtype)

    @pl.when(i % 2 == 0)
    def _():
        step(coef_sets[0], coef_sets[1])

    @pl.when(i % 2 == 1)
    def _():
        step(coef_sets[1], coef_sets[0])


def _rg_lru(proj, conv_w, conv_b, w_rg, b_rg, w_ig, b_ig, lam):
    bsz, s, _ = proj.shape
    d = D_MODEL
    tm = min(256, s)
    nt = s // tm
    hb = tm // HALO
    nhalo = s // HALO

    def tile_specs(tile_of):
        return [
            pl.BlockSpec((1, tm, d), lambda b, i: (b, tile_of(i), COL_XA)),
            pl.BlockSpec((1, HALO, d), lambda b, i: (b, jnp.maximum(tile_of(i) * hb - 1, 0), COL_XA)),
            pl.BlockSpec((1, HALO, d), lambda b, i: (b, jnp.minimum((tile_of(i) + 1) * hb, nhalo - 1), COL_XA)),
        ]

    const2 = lambda b, i: (0, 0)
    const4 = lambda b, i: (0, 0, 0, 0)
    out_sds = jax.ShapeDtypeStruct((bsz, s, d), BF16)
    return pl.pallas_call(
        functools.partial(_lru_kernel, nt=nt),
        grid=(bsz, nt + 1),
        in_specs=tile_specs(lambda i: jnp.minimum(i, nt - 1)) + tile_specs(lambda i: nt - 1 - jnp.minimum(i, nt - 1)) + [
            pl.BlockSpec(conv_w.shape, const2),
            pl.BlockSpec((1, d), const2),
            pl.BlockSpec(w_rg.shape, const4),
            pl.BlockSpec(b_rg.shape, const2),
            pl.BlockSpec(w_ig.shape, const4),
            pl.BlockSpec(b_ig.shape, const2),
            pl.BlockSpec(lam.shape, const2),
        ],
        out_specs=[
            pl.BlockSpec((1, tm, d), lambda b, i: (b, jnp.maximum(i - 1, 0), 0)),
            pl.BlockSpec((1, tm, d), lambda b, i: (b, nt - 1 - jnp.maximum(i - 1, 0), 0)),
        ],
        out_shape=[out_sds, out_sds],
        scratch_shapes=[pltpu.VMEM((N_SLABS, LANES), F32)] * 2
        + [pltpu.VMEM((N_SLABS * _slab_pitch(tm), LANES), F32)] * 2
        + [pltpu.VMEM((2, tm, d), F32)] * 3
        + [pltpu.VMEM((6, 8, d), F32)]
        + [pltpu.VMEM((N_SLABS * _slab_pitch(tm), LANES), F32)] * 8,
        compiler_params=_cparams(("parallel", "arbitrary")),
        name="rg_lru",
    )(proj, proj, proj, proj, proj, proj, conv_w, conv_b.reshape(1, d), w_rg, b_rg, w_ig, b_ig, lam)


def _merge_kernel(hf_ref, hb_ref, ga_ref, yb_ref, xc_ref, bc_ref, cc_ref, gc_ref, xcp_ref, ccp_ref, xcn_ref,
                  ccn_ref, mg0_ref, mg1_ref, mg2_ref, x_ref, mod_ref, cw_ref, wb_ref, wo_ref, o_ref, *, nt):
    i = pl.program_id(1)
    merged = _sigmoid(mg1_ref[0].astype(F32)) * jnp.dot(yb_ref[0], wb_ref[1], preferred_element_type=F32)

    ya = (hf_ref[0].astype(F32) + hb_ref[0].astype(F32)) * _silu(ga_ref[0].astype(F32))
    merged += _sigmoid(mg0_ref[0].astype(F32)) * jnp.dot(ya.astype(BF16), wb_ref[0], preferred_element_type=F32)

    z = cc_ref[0].astype(F32) * xc_ref[0].astype(F32)
    z_prev = jnp.where(i == 0, 0.0, ccp_ref[0].astype(F32) * xcp_ref[0].astype(F32))
    z_next = jnp.where(i == nt - 1, 0.0, ccn_ref[0].astype(F32) * xcn_ref[0].astype(F32))
    conv = (cw_ref[0:1, :] * _shift_rows(z, z_prev, 1) + cw_ref[1:2, :] * z
            + cw_ref[2:3, :] * _shift_rows(z, z_next, -1))
    yc = bc_ref[0].astype(F32) * conv * _silu(gc_ref[0].astype(F32))

    merged += _sigmoid(mg2_ref[0].astype(F32)) * jnp.dot(yc.astype(BF16), wb_ref[2], preferred_element_type=F32)
    out = jnp.dot(merged.astype(BF16), wo_ref[...], preferred_element_type=F32)
    o_ref[0] = x_ref[0] + mod_ref[0, 2:3, :] * out


def _merge(hf, hb, yb, proj, x, mod, conv_c_w, w_branch, w_out):
    bsz, s, d = x.shape
    tm = min(256, s)
    nt = s // tm
    hbk = tm // HALO
    nhalo = s // HALO

    def col(c):
        return pl.BlockSpec((1, tm, d), lambda b, i: (b, i, c))

    def halo_prev(c):
        return pl.BlockSpec((1, HALO, d), lambda b, i: (b, jnp.maximum(i * hbk - 1, 0), c))

    def halo_next(c):
        return pl.BlockSpec((1, HALO, d), lambda b, i: (b, jnp.minimum((i + 1) * hbk, nhalo - 1), c))

    return pl.pallas_call(
        functools.partial(_merge_kernel, nt=nt),
        grid=(bsz, nt),
        in_specs=[
            col(0), col(0), col(COL_GA), col(0), col(COL_XC), col(COL_BC), col(COL_CC), col(COL_GC),
            halo_prev(COL_XC), halo_prev(COL_CC), halo_next(COL_XC), halo_next(COL_CC),
            col(COL_MG), col(COL_MG + 1), col(COL_MG + 2),
            col(0),
            pl.BlockSpec((1, 3, d), lambda b, i: (b, 0, 0)),
            pl.BlockSpec(conv_c_w.shape, lambda b, i: (0, 0)),
            pl.BlockSpec(w_branch.shape, lambda b, i: (0, 0, 0)),
            pl.BlockSpec(w_out.shape, lambda b, i: (0, 0)),
        ],
        out_specs=pl.BlockSpec((1, tm, d), lambda b, i: (b, i, 0)),
        out_shape=jax.ShapeDtypeStruct((bsz, s, d), F32),
        compiler_params=_cparams(("parallel", "parallel")),
        name="merge_outproj",
    )(hf, hb, proj, yb, proj, proj, proj, proj, proj, proj, proj, proj, proj, proj, proj, x, mod,
      conv_c_w, w_branch, w_out)


def _permute_w_in(w):
    d = D_MODEL
    o_q = 2 * d
    o_k = o_q + d
    o_v = o_k + KV_W
    o_gb = o_v + KV_W
    o_xc = o_gb + d
    return jnp.concatenate([w[:, :o_q], w[:, o_q:o_k], w[:, o_gb:o_xc], w[:, o_xc:], w[:, o_k:o_gb]], axis=1)


def _attn_tiles(s):
    tq = min(512, s)
    tk = min(512, s)
    return tq, tk


def kernel(x, c, w_ada, b_ada, norm_g, w_in, conv_a_w, conv_a_b, w_rg, b_rg, w_ig, b_ig, lru_lam, q_norm_g,
           k_norm_g, conv_c_w, w_branch, w_out):
    bsz, s, d = x.shape
    depth = w_ada.shape[0]
    tq, tk = _attn_tiles(s)
    cos_t, sin_t = _rope_tables(s)
    mods = _modulation(c, w_ada, b_ada)
    for l in range(depth):
        mod = mods[l]
        proj = _in_projection(x, mod, norm_g[l], _permute_w_in(w_in[l]).astype(BF16))
        hf, hb = _rg_lru(proj, conv_a_w[l], conv_a_b[l], (0.5 * w_rg[l]).astype(BF16), b_rg[l],
                         (0.5 * w_ig[l]).astype(BF16), b_ig[l], lru_lam[l])
        qt, kk, vt = _attn_prep(proj, cos_t, sin_t, q_norm_g[l], k_norm_g[l], tk)
        bounded = (_score_bound(q_norm_g[l], k_norm_g[l]) <= SCORE_BOUND_NOSHIFT).astype(jnp.int32).reshape(1)
        yb = _attention(qt, kk, vt, proj, bounded, tq)
        x = _merge(hf, hb, yb, proj, x, mod, conv_c_w[l], w_branch[l].astype(BF16), w_out[l].astype(BF16))
    return x
```

```python
import functools
import math

import numpy as np
import jax
import jax.numpy as jnp
from jax import lax
from jax.experimental import pallas as pl
from jax.experimental.pallas import tpu as pltpu

D_MODEL = 1024
EPS = 1e-6
GRID_W = 64
N_HEADS = 8
N_KV = 2
GROUP = N_HEADS // N_KV
HEAD_DIM = 128
RNN_BLOCKS = 4
RNN_BW = D_MODEL // RNN_BLOCKS
LRU_C = 8.0
ROPE_THETA = 10000.0
N_BRANCH = 3
LOG2E = 1.4426950408889634

COL_XA, COL_GA, COL_Q, COL_GB, COL_XC, COL_BC, COL_CC, COL_GC, COL_MG = 0, 1, 2, 3, 4, 5, 6, 7, 8
COL_KV = 11
N_IN = 11 * D_MODEL + 2 * N_KV * HEAD_DIM
KV_W = N_KV * HEAD_DIM

HALO = 16
VMEM_LIMIT = 56 * 1024 * 1024

F32 = jnp.float32
BF16 = jnp.bfloat16


def _sigmoid(x):
    return 1.0 / (1.0 + jnp.exp2(x * (-LOG2E)))


def _silu(x):
    return x * _sigmoid(x)


def _cparams(sem):
    return pltpu.CompilerParams(dimension_semantics=sem, vmem_limit_bytes=VMEM_LIMIT)


def _mod_kernel(c_ref, w_ref, b_ref, o_ref):
    acc = jnp.dot(c_ref[...], w_ref[0], preferred_element_type=F32, precision=lax.Precision.HIGHEST)
    o_ref[0] = acc + b_ref[0]


def _modulation(c, w_ada, b_ada):
    depth, d, n = w_ada.shape
    bsz = c.shape[0]
    rows = 8
    c8 = jnp.zeros((rows, d), F32).at[:bsz].set(c)
    tn = 1024
    out = pl.pallas_call(
        _mod_kernel,
        grid=(depth, n // tn),
        in_specs=[
            pl.BlockSpec((rows, d), lambda l, j: (0, 0)),
            pl.BlockSpec((1, d, tn), lambda l, j: (l, 0, j)),
            pl.BlockSpec((1, 1, tn), lambda l, j: (l, 0, j)),
        ],
        out_specs=pl.BlockSpec((1, rows, tn), lambda l, j: (l, 0, j)),
        out_shape=jax.ShapeDtypeStruct((depth, rows, n), F32),
        compiler_params=_cparams(("parallel", "parallel")),
        name="adaln_mod",
    )(c8, w_ada, b_ada.reshape(depth, 1, n))
    return out[:, :bsz].reshape(depth, bsz, 3, d)


INPROJ_TN = 512


def _inproj_kernel(x_ref, mod_ref, g_ref, w_ref, o_ref):
    xv = x_ref[0]
    ms = jnp.mean(xv * xv, axis=-1, keepdims=True)
    y = xv * lax.rsqrt(ms + EPS) * g_ref[...]
    shift = mod_ref[0, 0:1, :]
    scale = mod_ref[0, 1:2, :]
    h = (y * (1.0 + scale) + shift).astype(BF16)
    for j in range(N_IN // INPROJ_TN):
        cols = slice(j * INPROJ_TN, (j + 1) * INPROJ_TN)
        o_ref[0, :, cols] = jnp.dot(h, w_ref[:, cols], preferred_element_type=F32).astype(BF16)


def _in_projection(x, mod, norm_g, w_in_p):
    bsz, s, d = x.shape
    tm = min(256, s)
    return pl.pallas_call(
        _inproj_kernel,
        grid=(bsz, s // tm),
        in_specs=[
            pl.BlockSpec((1, tm, d), lambda b, i: (b, i, 0)),
            pl.BlockSpec((1, 3, d), lambda b, i: (b, 0, 0)),
            pl.BlockSpec((1, d), lambda b, i: (0, 0)),
            pl.BlockSpec((d, N_IN), lambda b, i: (0, 0), pipeline_mode=pl.Buffered(1)),
        ],
        out_specs=pl.BlockSpec((1, tm, N_IN), lambda b, i: (b, i, 0)),
        out_shape=jax.ShapeDtypeStruct((bsz, s, N_IN), BF16),
        compiler_params=_cparams(("parallel", "parallel")),
        name="norm_inproj",
    )(x, mod, norm_g.reshape(1, d), w_in_p)


def _rope_tables(s):
    half = HEAD_DIM // 2
    inv = ROPE_THETA ** (-np.arange(0, half, 2, dtype=np.float64) / half)
    rows = s // GRID_W
    ang_r = np.arange(rows, dtype=np.float64)[:, None] * inv
    ang_c = np.arange(GRID_W, dtype=np.float64)[:, None] * inv
    cr, sr = np.cos(ang_r), np.sin(ang_r)
    cc, sc = np.cos(ang_c), np.sin(ang_c)
    cos_r = jnp.asarray(np.concatenate([cr, cr], -1), F32)
    sin_r = jnp.asarray(np.concatenate([-sr, sr], -1), F32)
    cos_c = jnp.asarray(np.concatenate([cc, cc], -1), F32)
    sin_c = jnp.asarray(np.concatenate([-sc, sc], -1), F32)
    cos_t = jnp.concatenate([jnp.repeat(cos_r, GRID_W, axis=0), jnp.tile(cos_c, (rows, 1))], -1)
    sin_t = jnp.concatenate([jnp.repeat(sin_r, GRID_W, axis=0), jnp.tile(sin_c, (rows, 1))], -1)
    return cos_t, sin_t


def _norm_rope(xh, g, cos_t, sin_t, lane_lo):
    ms = jnp.mean(xh * xh, axis=-1, keepdims=True)
    y = xh * lax.rsqrt(ms + EPS) * g
    partner = jnp.where(lane_lo, pltpu.roll(y, HEAD_DIM - 32, axis=1), pltpu.roll(y, 32, axis=1))
    return y * cos_t + partner * sin_t


def _prep_kernel(q_ref, kv_ref, cos_ref, sin_ref, qg_ref, kg_ref, qt_ref, k_ref, vt_ref):
    cos_t = cos_ref[...]
    sin_t = sin_ref[...]
    lane = lax.broadcasted_iota(jnp.int32, cos_t.shape, 1)
    lane_lo = (lane % 64) < 32
    q_scale = (HEAD_DIM ** -0.5) * LOG2E
    for h in range(N_HEADS):
        qh = q_ref[0, :, h * HEAD_DIM:(h + 1) * HEAD_DIM].astype(F32)
        qh = _norm_rope(qh, qg_ref[...], cos_t, sin_t, lane_lo) * q_scale
        qt_ref[0, h] = qh.T.astype(BF16)
    for h in range(N_KV):
        kh = kv_ref[0, :, h * HEAD_DIM:(h + 1) * HEAD_DIM].astype(F32)
        k_ref[0, h, 0] = _norm_rope(kh, kg_ref[...], cos_t, sin_t, lane_lo).astype(BF16)
        vh = kv_ref[0, :, KV_W + h * HEAD_DIM:KV_W + (h + 1) * HEAD_DIM].astype(F32)
        vt_ref[0, h, 0] = vh.T.astype(BF16)


def _attn_prep(proj, cos_t, sin_t, q_g, k_g, tk):
    bsz, s, _ = proj.shape
    nkb = s // tk
    return pl.pallas_call(
        _prep_kernel,
        grid=(bsz, nkb),
        in_specs=[
            pl.BlockSpec((1, tk, D_MODEL), lambda b, i: (b, i, COL_Q)),
            pl.BlockSpec((1, tk, 2 * KV_W), lambda b, i: (b, i, COL_KV * D_MODEL // (2 * KV_W))),
            pl.BlockSpec((tk, HEAD_DIM), lambda b, i: (i, 0)),
            pl.BlockSpec((tk, HEAD_DIM), lambda b, i: (i, 0)),
            pl.BlockSpec((1, HEAD_DIM), lambda b, i: (0, 0)),
            pl.BlockSpec((1, HEAD_DIM), lambda b, i: (0, 0)),
        ],
        out_specs=[
            pl.BlockSpec((1, N_HEADS, HEAD_DIM, tk), lambda b, i: (b, 0, 0, i)),
            pl.BlockSpec((1, N_KV, 1, tk, HEAD_DIM), lambda b, i: (b, 0, i, 0, 0)),
            pl.BlockSpec((1, N_KV, 1, HEAD_DIM, tk), lambda b, i: (b, 0, i, 0, 0)),
        ],
        out_shape=[
            jax.ShapeDtypeStruct((bsz, N_HEADS, HEAD_DIM, s), BF16),
            jax.ShapeDtypeStruct((bsz, N_KV, nkb, tk, HEAD_DIM), BF16),
            jax.ShapeDtypeStruct((bsz, N_KV, nkb, HEAD_DIM, tk), BF16),
        ],
        compiler_params=_cparams(("parallel", "parallel")),
        name="attn_prep",
    )(proj, proj, cos_t, sin_t, q_g.reshape(1, HEAD_DIM), k_g.reshape(1, HEAD_DIM))


SCORE_BOUND_NOSHIFT = 40.0


def _score_bound(q_g, k_g):
    q_scale = (HEAD_DIM ** -0.5) * LOG2E
    return HEAD_DIM * q_scale * jnp.max(jnp.abs(q_g)) * jnp.max(jnp.abs(k_g))


KV_BLOCKS_PER_TRIP = 8


def _largest_divisor(n, cap):
    return max(d for d in range(1, cap + 1) if n % d == 0)


def _attn_kernel(flag_ref, qt_ref, k_ref, vt_ref, gb_ref, o_ref, m_ref, l_ref, acc_ref, s_ref, *, nkb):
    tq = acc_ref.shape[-1]
    l_ref[...] = jnp.zeros(l_ref.shape, F32)
    acc_ref[...] = jnp.zeros(acc_ref.shape, F32)
    bounded = flag_ref[0] != 0

    @pl.when(bounded)
    def _():
        def scores(j, h):
            return jnp.dot(k_ref[0, 0, j], qt_ref[0, h], preferred_element_type=F32)

        s_ref[...] = scores(0, 0)

        group = _largest_divisor(nkb, KV_BLOCKS_PER_TRIP)

        def kv_group(g, carry):
            s_cur = s_ref[...]
            for u in range(group):
                j = g * group + u
                vblk = vt_ref[0, 0, j]
                for h in range(GROUP):
                    s_next = scores(j, h + 1) if h + 1 < GROUP else scores(jnp.minimum(j + 1, nkb - 1), 0)
                    p = jnp.exp2(s_cur)
                    l_ref[h] += jnp.sum(p.reshape(-1, 8, tq), axis=0)
                    acc_ref[h] += jnp.dot(vblk, p.astype(BF16), preferred_element_type=F32)
                    s_cur = s_next
            s_ref[...] = s_cur
            return carry

        lax.fori_loop(0, nkb // group, kv_group, 0)

    @pl.when(jnp.logical_not(bounded))
    def _():
        m_ref[...] = jnp.full(m_ref.shape, -jnp.inf, F32)

        def kv_step(j, carry):
            kblk = k_ref[0, 0, j]
            vblk = vt_ref[0, 0, j]
            for h in range(GROUP):
                s = jnp.dot(kblk, qt_ref[0, h], preferred_element_type=F32)
                m_old = m_ref[h]
                m_new = jnp.maximum(m_old, jnp.max(s, axis=0, keepdims=True))
                alpha = jnp.exp2(m_old - m_new)
                p = jnp.exp2(s - m_new)
                l_ref[h] = alpha * l_ref[h] + jnp.sum(p.reshape(-1, 8, tq), axis=0)
                acc_ref[h] = alpha * acc_ref[h] + jnp.dot(vblk, p.astype(BF16), preferred_element_type=F32)
                m_ref[h] = m_new
            return carry

        lax.fori_loop(0, nkb, kv_step, 0)

    for h in range(GROUP):
        l = jnp.sum(l_ref[h], axis=0, keepdims=True)
        o = acc_ref[h] * (1.0 / l)
        gate = _silu(gb_ref[0, :, h * HEAD_DIM:(h + 1) * HEAD_DIM].astype(F32))
        o_ref[0, :, h * HEAD_DIM:(h + 1) * HEAD_DIM] = (o.T * gate).astype(BF16)


def _attention(qt, kk, vt, proj, bounded_flag, tq):
    bsz, _, _, s = qt.shape
    nkb, tk = kk.shape[2], kk.shape[3]
    gw = GROUP * HEAD_DIM
    grid_spec = pltpu.PrefetchScalarGridSpec(
        num_scalar_prefetch=1,
        grid=(bsz, N_KV, s // tq),
        in_specs=[
            pl.BlockSpec((1, GROUP, HEAD_DIM, tq), lambda b, g, i, f: (b, g, 0, i)),
            pl.BlockSpec((1, 1, nkb, tk, HEAD_DIM), lambda b, g, i, f: (b, g, 0, 0, 0)),
            pl.BlockSpec((1, 1, nkb, HEAD_DIM, tk), lambda b, g, i, f: (b, g, 0, 0, 0)),
            pl.BlockSpec((1, tq, gw), lambda b, g, i, f: (b, i, COL_GB * D_MODEL // gw + g)),
        ],
        out_specs=pl.BlockSpec((1, tq, gw), lambda b, g, i, f: (b, i, g)),
        scratch_shapes=[
            pltpu.VMEM((GROUP, 1, tq), F32),
            pltpu.VMEM((GROUP, 8, tq), F32),
            pltpu.VMEM((GROUP, HEAD_DIM, tq), F32),
            pltpu.VMEM((tk, tq), F32),
        ],
    )
    return pl.pallas_call(
        functools.partial(_attn_kernel, nkb=nkb),
        grid_spec=grid_spec,
        out_shape=jax.ShapeDtypeStruct((bsz, s, D_MODEL), BF16),
        compiler_params=_cparams(("parallel", "parallel", "arbitrary")),
        name="gqa_attention",
    )(bounded_flag, qt, kk, vt, proj)


def _shift_rows(tile, halo, shift):
    t = tile.shape[0]
    row8 = lax.broadcasted_iota(jnp.int32, (8, tile.shape[1]), 0)
    if shift > 0:
        rolled = pltpu.roll(tile, shift, axis=0)
        edge = pltpu.roll(halo, shift, axis=0)[0:8]
        fixed = jnp.where(row8 < shift, edge, rolled[0:8])
        return jnp.concatenate([fixed, rolled[8:]], axis=0)
    k = -shift
    rolled = pltpu.roll(tile, t - k, axis=0)
    edge = pltpu.roll(halo, HALO - k, axis=0)[HALO - 8:HALO]
    fixed = jnp.where(row8 >= 8 - k, edge, rolled[t - 8:t])
    return jnp.concatenate([rolled[:t - 8], fixed], axis=0)


def _lru_conv_gates(x_ref, prev_ref, next_ref, first, last, cw_ref, cb_ref, wr_ref, wi_ref, d,
                    ua_ref, rpre_ref, ipre_ref):
    tile = x_ref[0].astype(F32)
    prev = jnp.where(first, 0.0, prev_ref[0].astype(F32))
    nxt = jnp.where(last, 0.0, next_ref[0].astype(F32))
    ua = (cw_ref[0:1, :] * _shift_rows(tile, prev, 2)
          + cw_ref[1:2, :] * _shift_rows(tile, prev, 1)
          + cw_ref[2:3, :] * tile
          + cw_ref[3:4, :] * _shift_rows(tile, nxt, -1)
          + cb_ref[...])
    ua_ref[d] = ua
    ub = ua.astype(BF16)
    for n in range(RNN_BLOCKS):
        cols = slice(n * RNN_BW, (n + 1) * RNN_BW)
        rpre_ref[d, :, cols] = jnp.dot(ub[:, cols], wr_ref[d, n], preferred_element_type=F32)
        ipre_ref[d, :, cols] = jnp.dot(ub[:, cols], wi_ref[d, n], preferred_element_type=F32)


def _lru_coefficients(ua, r_half, i_half, b_r_half, b_i_half, c_half):
    y = c_half + c_half * jnp.tanh(r_half + b_r_half)
    ua_half = 0.5 * ua
    gated = ua_half + ua_half * jnp.tanh(i_half + b_i_half)
    a = jnp.exp2(y * (-LOG2E))
    x = jnp.tanh(y) * (1.0 + a * a)
    root = jnp.where(x > 0.0, x * lax.rsqrt(x), x)
    return a, root * gated


LANES = 128
N_SLABS = D_MODEL // LANES


def _slab_pitch(t):
    return t + 8 if (t // 8) % 2 == 0 else t


def _lru_kernel(xf_ref, pf_ref, nf_ref, xb_ref, pb_ref, nb_ref, cw_ref, cb_ref, wr_ref, br_ref, wi_ref,
                bi_ref, lam_ref, hf_ref, hb_ref, cf_ref, cbk_ref, sf_ref, sb_ref, ua_ref, rpre_ref, ipre_ref,
                chan_ref, *coef_refs, nt):
    i = pl.program_id(1)
    tm = hf_ref.shape[1]
    pitch = _slab_pitch(tm)
    it = jnp.minimum(i, nt - 1)
    coef_sets = (coef_refs[:4], coef_refs[4:])

    @pl.when(i == 0)
    def _():
        cf_ref[...] = jnp.zeros(cf_ref.shape, F32)
        cbk_ref[...] = jnp.zeros(cbk_ref.shape, F32)
        for ref in coef_sets[1]:
            ref[...] = jnp.zeros(ref.shape, F32)

    _lru_conv_gates(xf_ref, pf_ref, nf_ref, it == 0, it == nt - 1, cw_ref, cb_ref, wr_ref, wi_ref, 0,
                    ua_ref, rpre_ref, ipre_ref)
    _lru_conv_gates(xb_ref, pb_ref, nb_ref, it == nt - 1, it == 0, cw_ref, cb_ref, wr_ref, wi_ref, 1,
                    ua_ref, rpre_ref, ipre_ref)
    nl = -lam_ref[...]
    c_half = (0.5 * LRU_C) * (jnp.maximum(nl, 0.0) + jnp.log1p(jnp.exp(-jnp.abs(nl))))
    for d in range(2):
        for n, row in enumerate((0.5 * br_ref[d:d + 1, :], 0.5 * bi_ref[d:d + 1, :], c_half[d:d + 1, :])):
            chan_ref[3 * d + n] = jnp.broadcast_to(row, chan_ref.shape[1:])

    def step(cur, prev):
        af_ref, uf_ref, ab_ref, ub_ref = prev

        def rows8(g, carry):
            hf, hb = carry
            r0 = pl.multiple_of(g * 8, 8)
            rows = pl.ds(r0, 8)
            for d in range(2):
                a, u = _lru_coefficients(ua_ref[d, rows, :], rpre_ref[d, rows, :], ipre_ref[d, rows, :],
                                         chan_ref[3 * d], chan_ref[3 * d + 1], chan_ref[3 * d + 2])
                for c in range(N_SLABS):
                    dst = pl.ds(c * pitch + r0, 8)
                    cur[2 * d][dst, :] = a[:, c * LANES:(c + 1) * LANES]
                    cur[2 * d + 1][dst, :] = u[:, c * LANES:(c + 1) * LANES]
            for j in range(8):
                k = r0 + j
                tf = pl.ds(k, N_SLABS, stride=pitch)
                tb = pl.ds(tm - 1 - k, N_SLABS, stride=pitch)
                hf = af_ref[tf, :] * hf + uf_ref[tf, :]
                sf_ref[tf, :] = hf
                hb = ab_ref[tb, :] * hb + ub_ref[tb, :]
                sb_ref[tb, :] = hb
            return hf, hb

        hf, hb = lax.fori_loop(0, tm // 8, rows8, (cf_ref[...], cbk_ref[...]), unroll=4)
        keep = i > 0
        cf_ref[...] = jnp.where(keep, hf, cf_ref[...])
        cbk_ref[...] = jnp.where(keep, hb, cbk_ref[...])
        for s_ref, out_ref in ((sf_ref, hf_ref), (sb_ref, hb_ref)):
            for c in range(N_SLABS):
                out_ref[0, :, c * LANES:(c + 1) * LANES] = s_ref[pl.ds(c * pitch, tm), :].astype(out_ref.dtype)

    @pl.when(i % 2 == 0)
    def _():
        step(coef_sets[0], coef_sets[1])

    @pl.when(i % 2 == 1)
    def _():
        step(coef_sets[1], coef_sets[0])


def _rg_lru(proj, conv_w, conv_b, w_rg, b_rg, w_ig, b_ig, lam):
    bsz, s, _ = proj.shape
    d = D_MODEL
    tm = min(256, s)
    nt = s // tm
    hb = tm // HALO
    nhalo = s // HALO

    def tile_specs(tile_of):
        return [
            pl.BlockSpec((1, tm, d), lambda b, i: (b, tile_of(i), COL_XA)),
            pl.BlockSpec((1, HALO, d), lambda b, i: (b, jnp.maximum(tile_of(i) * hb - 1, 0), COL_XA)),
            pl.BlockSpec((1, HALO, d), lambda b, i: (b, jnp.minimum((tile_of(i) + 1) * hb, nhalo - 1), COL_XA)),
        ]

    const2 = lambda b, i: (0, 0)
    const4 = lambda b, i: (0, 0, 0, 0)
    out_sds = jax.ShapeDtypeStruct((bsz, s, d), BF16)
    return pl.pallas_call(
        functools.partial(_lru_kernel, nt=nt),
        grid=(bsz, nt + 1),
        in_specs=tile_specs(lambda i: jnp.minimum(i, nt - 1)) + tile_specs(lambda i: nt - 1 - jnp.minimum(i, nt - 1)) + [
            pl.BlockSpec(conv_w.shape, const2),
            pl.BlockSpec((1, d), const2),
            pl.BlockSpec(w_rg.shape, const4),
            pl.BlockSpec(b_rg.shape, const2),
            pl.BlockSpec(w_ig.shape, const4),
            pl.BlockSpec(b_ig.shape, const2),
            pl.BlockSpec(lam.shape, const2),
        ],
        out_specs=[
            pl.BlockSpec((1, tm, d), lambda b, i: (b, jnp.maximum(i - 1, 0), 0)),
            pl.BlockSpec((1, tm, d), lambda b, i: (b, nt - 1 - jnp.maximum(i - 1, 0), 0)),
        ],
        out_shape=[out_sds, out_sds],
        scratch_shapes=[pltpu.VMEM((N_SLABS, LANES), F32)] * 2
        + [pltpu.VMEM((N_SLABS * _slab_pitch(tm), LANES), F32)] * 2
        + [pltpu.VMEM((2, tm, d), F32)] * 3
        + [pltpu.VMEM((6, 8, d), F32)]
        + [pltpu.VMEM((N_SLABS * _slab_pitch(tm), LANES), F32)] * 8,
        compiler_params=_cparams(("parallel", "arbitrary")),
        name="rg_lru",
    )(proj, proj, proj, proj, proj, proj, conv_w, conv_b.reshape(1, d), w_rg, b_rg, w_ig, b_ig, lam)


def _merge_kernel(hf_ref, hb_ref, ga_ref, yb_ref, xc_ref, bc_ref, cc_ref, gc_ref, xcp_ref, ccp_ref, xcn_ref,
                  ccn_ref, mg0_ref, mg1_ref, mg2_ref, x_ref, mod_ref, cw_ref, wb_ref, wo_ref, o_ref, *, nt):
    i = pl.program_id(1)
    merged = _sigmoid(mg1_ref[0].astype(F32)) * jnp.dot(yb_ref[0], wb_ref[1], preferred_element_type=F32)

    ya = (hf_ref[0].astype(F32) + hb_ref[0].astype(F32)) * _silu(ga_ref[0].astype(F32))
    merged += _sigmoid(mg0_ref[0].astype(F32)) * jnp.dot(ya.astype(BF16), wb_ref[0], preferred_element_type=F32)

    z = cc_ref[0].astype(F32) * xc_ref[0].astype(F32)
    z_prev = jnp.where(i == 0, 0.0, ccp_ref[0].astype(F32) * xcp_ref[0].astype(F32))
    z_next = jnp.where(i == nt - 1, 0.0, ccn_ref[0].astype(F32) * xcn_ref[0].astype(F32))
    conv = (cw_ref[0:1, :] * _shift_rows(z, z_prev, 1) + cw_ref[1:2, :] * z
            + cw_ref[2:3, :] * _shift_rows(z, z_next, -1))
    yc = bc_ref[0].astype(F32) * conv * _silu(gc_ref[0].astype(F32))

    merged += _sigmoid(mg2_ref[0].astype(F32)) * jnp.dot(yc.astype(BF16), wb_ref[2], preferred_element_type=F32)
    out = jnp.dot(merged.astype(BF16), wo_ref[...], preferred_element_type=F32)
    o_ref[0] = x_ref[0] + mod_ref[0, 2:3, :] * out


def _merge(hf, hb, yb, proj, x, mod, conv_c_w, w_branch, w_out):
    bsz, s, d = x.shape
    tm = min(256, s)
    nt = s // tm
    hbk = tm // HALO
    nhalo = s // HALO

    def col(c):
        return pl.BlockSpec((1, tm, d), lambda b, i: (b, i, c))

    def halo_prev(c):
        return pl.BlockSpec((1, HALO, d), lambda b, i: (b, jnp.maximum(i * hbk - 1, 0), c))

    def halo_next(c):
        return pl.BlockSpec((1, HALO, d), lambda b, i: (b, jnp.minimum((i + 1) * hbk, nhalo - 1), c))

    return pl.pallas_call(
        functools.partial(_merge_kernel, nt=nt),
        grid=(bsz, nt),
        in_specs=[
            col(0), col(0), col(COL_GA), col(0), col(COL_XC), col(COL_BC), col(COL_CC), col(COL_GC),
            halo_prev(COL_XC), halo_prev(COL_CC), halo_next(COL_XC), halo_next(COL_CC),
            col(COL_MG), col(COL_MG + 1), col(COL_MG + 2),
            col(0),
            pl.BlockSpec((1, 3, d), lambda b, i: (b, 0, 0)),
            pl.BlockSpec(conv_c_w.shape, lambda b, i: (0, 0)),
            pl.BlockSpec(w_branch.shape, lambda b, i: (0, 0, 0)),
            pl.BlockSpec(w_out.shape, lambda b, i: (0, 0)),
        ],
        out_specs=pl.BlockSpec((1, tm, d), lambda b, i: (b, i, 0)),
        out_shape=jax.ShapeDtypeStruct((bsz, s, d), F32),
        compiler_params=_cparams(("parallel", "parallel")),
        name="merge_outproj",
    )(hf, hb, proj, yb, proj, proj, proj, proj, proj, proj, proj, proj, proj, proj, proj, x, mod,
      conv_c_w, w_branch, w_out)


def _permute_w_in(w):
    d = D_MODEL
    o_q = 2 * d
    o_k = o_q + d
    o_v = o_k + KV_W
    o_gb = o_v + KV_W
    o_xc = o_gb + d
    return jnp.concatenate([w[:, :o_q], w[:, o_q:o_k], w[:, o_gb:o_xc], w[:, o_xc:], w[:, o_k:o_gb]], axis=1)


def _attn_tiles(s):
    tq = min(512, s)
    tk = min(512, s)
    return tq, tk


def kernel(x, c, w_ada, b_ada, norm_g, w_in, conv_a_w, conv_a_b, w_rg, b_rg, w_ig, b_ig, lru_lam, q_norm_g,
           k_norm_g, conv_c_w, w_branch, w_out):
    bsz, s, d = x.shape
    depth = w_ada.shape[0]
    tq, tk = _attn_tiles(s)
    cos_t, sin_t = _rope_tables(s)
    mods = _modulation(c, w_ada, b_ada)
    for l in range(depth):
        mod = mods[l]
        proj = _in_projection(x, mod, norm_g[l], _permute_w_in(w_in[l]).astype(BF16))
        hf, hb = _rg_lru(proj, conv_a_w[l], conv_a_b[l], (0.5 * w_rg[l]).astype(BF16), b_rg[l],
                         (0.5 * w_ig[l]).astype(BF16), b_ig[l], lru_lam[l])
        qt, kk, vt = _attn_prep(proj, cos_t, sin_t, q_norm_g[l], k_norm_g[l], tk)
        bounded = (_score_bound(q_norm_g[l], k_norm_g[l]) <= SCORE_BOUND_NOSHIFT).astype(jnp.int32).reshape(1)
        yb = _attention(qt, kk, vt, proj, bounded, tq)
        x = _merge(hf, hb, yb, proj, x, mod, conv_c_w[l], w_branch[l].astype(BF16), w_out[l].astype(BF16))
    return x
```

```python
import functools
import math

import numpy as np
import jax
import jax.numpy as jnp
from jax import lax
from jax.experimental import pallas as pl
from jax.experimental.pallas import tpu as pltpu

D_MODEL = 1024
EPS = 1e-6
GRID_W = 64
N_HEADS = 8
N_KV = 2
GROUP = N_HEADS // N_KV
HEAD_DIM = 128
RNN_BLOCKS = 4
RNN_BW = D_MODEL // RNN_BLOCKS
LRU_C = 8.0
ROPE_THETA = 10000.0
N_BRANCH = 3
LOG2E = 1.4426950408889634

COL_XA, COL_GA, COL_Q, COL_GB, COL_XC, COL_BC, COL_CC, COL_GC, COL_MG = 0, 1, 2, 3, 4, 5, 6, 7, 8
COL_KV = 11
N_IN = 11 * D_MODEL + 2 * N_KV * HEAD_DIM
KV_W = N_KV * HEAD_DIM

HALO = 16
VMEM_LIMIT = 56 * 1024 * 1024

F32 = jnp.float32
BF16 = jnp.bfloat16


def _sigmoid(x):
    return 1.0 / (1.0 + jnp.exp2(x * (-LOG2E)))


def _silu(x):
    return x * _sigmoid(x)


def _cparams(sem):
    return pltpu.CompilerParams(dimension_semantics=sem, vmem_limit_bytes=VMEM_LIMIT)


def _mod_kernel(c_ref, w_ref, b_ref, o_ref):
    acc = jnp.dot(c_ref[...], w_ref[0], preferred_element_type=F32, precision=lax.Precision.HIGHEST)
    o_ref[0] = acc + b_ref[0]


def _modulation(c, w_ada, b_ada):
    depth, d, n = w_ada.shape
    bsz = c.shape[0]
    rows = 8
    c8 = jnp.zeros((rows, d), F32).at[:bsz].set(c)
    tn = 1024
    out = pl.pallas_call(
        _mod_kernel,
        grid=(depth, n // tn),
        in_specs=[
            pl.BlockSpec((rows, d), lambda l, j: (0, 0)),
            pl.BlockSpec((1, d, tn), lambda l, j: (l, 0, j)),
            pl.BlockSpec((1, 1, tn), lambda l, j: (l, 0, j)),
        ],
        out_specs=pl.BlockSpec((1, rows, tn), lambda l, j: (l, 0, j)),
        out_shape=jax.ShapeDtypeStruct((depth, rows, n), F32),
        compiler_params=_cparams(("parallel", "parallel")),
        name="adaln_mod",
    )(c8, w_ada, b_ada.reshape(depth, 1, n))
    return out[:, :bsz].reshape(depth, bsz, 3, d)


INPROJ_TN = 512


def _inproj_kernel(x_ref, mod_ref, g_ref, w_ref, o_ref):
    xv = x_ref[0]
    ms = jnp.mean(xv * xv, axis=-1, keepdims=True)
    y = xv * lax.rsqrt(ms + EPS) * g_ref[...]
    shift = mod_ref[0, 0:1, :]
    scale = mod_ref[0, 1:2, :]
    h = (y * (1.0 + scale) + shift).astype(BF16)
    for j in range(N_IN // INPROJ_TN):
        cols = slice(j * INPROJ_TN, (j + 1) * INPROJ_TN)
        acc = jnp.dot(h, w_ref[:, cols], preferred_element_type=F32)
        col = j * INPROJ_TN // D_MODEL
        if col in (COL_GA, COL_GB, COL_GC):
            acc = _silu(acc)
        elif COL_MG <= col < COL_MG + N_BRANCH:
            acc = _sigmoid(acc)
        o_ref[0, :, cols] = acc.astype(BF16)


def _in_projection(x, mod, norm_g, w_in_p):
    bsz, s, d = x.shape
    tm = min(256, s)
    return pl.pallas_call(
        _inproj_kernel,
        grid=(bsz, s // tm),
        in_specs=[
            pl.BlockSpec((1, tm, d), lambda b, i: (b, i, 0)),
            pl.BlockSpec((1, 3, d), lambda b, i: (b, 0, 0)),
            pl.BlockSpec((1, d), lambda b, i: (0, 0)),
            pl.BlockSpec((d, N_IN), lambda b, i: (0, 0), pipeline_mode=pl.Buffered(1)),
        ],
        out_specs=pl.BlockSpec((1, tm, N_IN), lambda b, i: (b, i, 0)),
        out_shape=jax.ShapeDtypeStruct((bsz, s, N_IN), BF16),
        compiler_params=_cparams(("parallel", "parallel")),
        name="norm_inproj",
    )(x, mod, norm_g.reshape(1, d), w_in_p)


def _rope_tables(s):
    half = HEAD_DIM // 2
    inv = ROPE_THETA ** (-np.arange(0, half, 2, dtype=np.float64) / half)
    rows = s // GRID_W
    ang_r = np.arange(rows, dtype=np.float64)[:, None] * inv
    ang_c = np.arange(GRID_W, dtype=np.float64)[:, None] * inv
    cr, sr = np.cos(ang_r), np.sin(ang_r)
    cc, sc = np.cos(ang_c), np.sin(ang_c)
    cos_r = jnp.asarray(np.concatenate([cr, cr], -1), F32)
    sin_r = jnp.asarray(np.concatenate([-sr, sr], -1), F32)
    cos_c = jnp.asarray(np.concatenate([cc, cc], -1), F32)
    sin_c = jnp.asarray(np.concatenate([-sc, sc], -1), F32)
    cos_t = jnp.concatenate([jnp.repeat(cos_r, GRID_W, axis=0), jnp.tile(cos_c, (rows, 1))], -1)
    sin_t = jnp.concatenate([jnp.repeat(sin_r, GRID_W, axis=0), jnp.tile(sin_c, (rows, 1))], -1)
    return cos_t, sin_t


def _norm_rope(xh, g, cos_t, sin_t, lane_lo):
    ms = jnp.mean(xh * xh, axis=-1, keepdims=True)
    y = xh * lax.rsqrt(ms + EPS) * g
    partner = jnp.where(lane_lo, pltpu.roll(y, HEAD_DIM - 32, axis=1), pltpu.roll(y, 32, axis=1))
    return y * cos_t + partner * sin_t


def _prep_kernel(q_ref, kv_ref, cos_ref, sin_ref, qg_ref, kg_ref, qt_ref, k_ref, vt_ref):
    cos_t = cos_ref[...]
    sin_t = sin_ref[...]
    lane = lax.broadcasted_iota(jnp.int32, cos_t.shape, 1)
    lane_lo = (lane % 64) < 32
    q_scale = (HEAD_DIM ** -0.5) * LOG2E
    for h in range(N_HEADS):
        qh = q_ref[0, :, h * HEAD_DIM:(h + 1) * HEAD_DIM].astype(F32)
        qh = _norm_rope(qh, qg_ref[...], cos_t, sin_t, lane_lo) * q_scale
        qt_ref[0, h] = qh.T.astype(BF16)
    for h in range(N_KV):
        kh = kv_ref[0, :, h * HEAD_DIM:(h + 1) * HEAD_DIM].astype(F32)
        k_ref[0, h, 0] = _norm_rope(kh, kg_ref[...], cos_t, sin_t, lane_lo).astype(BF16)
        vh = kv_ref[0, :, KV_W + h * HEAD_DIM:KV_W + (h + 1) * HEAD_DIM].astype(F32)
        vt_ref[0, h, 0] = vh.T.astype(BF16)


def _attn_prep(proj, cos_t, sin_t, q_g, k_g, tk):
    bsz, s, _ = proj.shape
    nkb = s // tk
    return pl.pallas_call(
        _prep_kernel,
        grid=(bsz, nkb),
        in_specs=[
            pl.BlockSpec((1, tk, D_MODEL), lambda b, i: (b, i, COL_Q)),
            pl.BlockSpec((1, tk, 2 * KV_W), lambda b, i: (b, i, COL_KV * D_MODEL // (2 * KV_W))),
            pl.BlockSpec((tk, HEAD_DIM), lambda b, i: (i, 0)),
            pl.BlockSpec((tk, HEAD_DIM), lambda b, i: (i, 0)),
            pl.BlockSpec((1, HEAD_DIM), lambda b, i: (0, 0)),
            pl.BlockSpec((1, HEAD_DIM), lambda b, i: (0, 0)),
        ],
        out_specs=[
            pl.BlockSpec((1, N_HEADS, HEAD_DIM, tk), lambda b, i: (b, 0, 0, i)),
            pl.BlockSpec((1, N_KV, 1, tk, HEAD_DIM), lambda b, i: (b, 0, i, 0, 0)),
            pl.BlockSpec((1, N_KV, 1, HEAD_DIM, tk), lambda b, i: (b, 0, i, 0, 0)),
        ],
        out_shape=[
            jax.ShapeDtypeStruct((bsz, N_HEADS, HEAD_DIM, s), BF16),
            jax.ShapeDtypeStruct((bsz, N_KV, nkb, tk, HEAD_DIM), BF16),
            jax.ShapeDtypeStruct((bsz, N_KV, nkb, HEAD_DIM, tk), BF16),
        ],
        compiler_params=_cparams(("parallel", "parallel")),
        name="attn_prep",
    )(proj, proj, cos_t, sin_t, q_g.reshape(1, HEAD_DIM), k_g.reshape(1, HEAD_DIM))


SCORE_BOUND_NOSHIFT = 40.0


def _score_bound(q_g, k_g):
    q_scale = (HEAD_DIM ** -0.5) * LOG2E
    return HEAD_DIM * q_scale * jnp.max(jnp.abs(q_g)) * jnp.max(jnp.abs(k_g))


KV_BLOCKS_PER_TRIP = 8


def _largest_divisor(n, cap):
    return max(d for d in range(1, cap + 1) if n % d == 0)


def _attn_kernel(flag_ref, qt_ref, k_ref, vt_ref, gb_ref, o_ref, m_ref, l_ref, acc_ref, s_ref, *, nkb):
    tq = acc_ref.shape[-1]
    l_ref[...] = jnp.zeros(l_ref.shape, F32)
    acc_ref[...] = jnp.zeros(acc_ref.shape, F32)
    bounded = flag_ref[0] != 0

    @pl.when(bounded)
    def _():
        def scores(j, h):
            return jnp.dot(k_ref[0, 0, j], qt_ref[0, h], preferred_element_type=F32)

        s_ref[...] = scores(0, 0)

        group = _largest_divisor(nkb, KV_BLOCKS_PER_TRIP)

        def kv_group(g, carry):
            s_cur = s_ref[...]
            for u in range(group):
                j = g * group + u
                vblk = vt_ref[0, 0, j]
                for h in range(GROUP):
                    s_next = scores(j, h + 1) if h + 1 < GROUP else scores(jnp.minimum(j + 1, nkb - 1), 0)
                    p = jnp.exp2(s_cur)
                    l_ref[h] += jnp.sum(p.reshape(-1, 8, tq), axis=0)
                    acc_ref[h] += jnp.dot(vblk, p.astype(BF16), preferred_element_type=F32)
                    s_cur = s_next
            s_ref[...] = s_cur
            return carry

        lax.fori_loop(0, nkb // group, kv_group, 0)

    @pl.when(jnp.logical_not(bounded))
    def _():
        m_ref[...] = jnp.full(m_ref.shape, -jnp.inf, F32)

        def kv_step(j, carry):
            kblk = k_ref[0, 0, j]
            vblk = vt_ref[0, 0, j]
            for h in range(GROUP):
                s = jnp.dot(kblk, qt_ref[0, h], preferred_element_type=F32)
                m_old = m_ref[h]
                m_new = jnp.maximum(m_old, jnp.max(s, axis=0, keepdims=True))
                alpha = jnp.exp2(m_old - m_new)
                p = jnp.exp2(s - m_new)
                l_ref[h] = alpha * l_ref[h] + jnp.sum(p.reshape(-1, 8, tq), axis=0)
                acc_ref[h] = alpha * acc_ref[h] + jnp.dot(vblk, p.astype(BF16), preferred_element_type=F32)
                m_ref[h] = m_new
            return carry

        lax.fori_loop(0, nkb, kv_step, 0)

    for h in range(GROUP):
        l = jnp.sum(l_ref[h], axis=0, keepdims=True)
        o = acc_ref[h] * (1.0 / l)
        gate = gb_ref[0, :, h * HEAD_DIM:(h + 1) * HEAD_DIM].astype(F32)
        o_ref[0, :, h * HEAD_DIM:(h + 1) * HEAD_DIM] = (o.T * gate).astype(BF16)


def _attention(qt, kk, vt, proj, bounded_flag, tq):
    bsz, _, _, s = qt.shape
    nkb, tk = kk.shape[2], kk.shape[3]
    gw = GROUP * HEAD_DIM
    grid_spec = pltpu.PrefetchScalarGridSpec(
        num_scalar_prefetch=1,
        grid=(bsz, N_KV, s // tq),
        in_specs=[
            pl.BlockSpec((1, GROUP, HEAD_DIM, tq), lambda b, g, i, f: (b, g, 0, i)),
            pl.BlockSpec((1, 1, nkb, tk, HEAD_DIM), lambda b, g, i, f: (b, g, 0, 0, 0)),
            pl.BlockSpec((1, 1, nkb, HEAD_DIM, tk), lambda b, g, i, f: (b, g, 0, 0, 0)),
            pl.BlockSpec((1, tq, gw), lambda b, g, i, f: (b, i, COL_GB * D_MODEL // gw + g)),
        ],
        out_specs=pl.BlockSpec((1, tq, gw), lambda b, g, i, f: (b, i, g)),
        scratch_shapes=[
            pltpu.VMEM((GROUP, 1, tq), F32),
            pltpu.VMEM((GROUP, 8, tq), F32),
            pltpu.VMEM((GROUP, HEAD_DIM, tq), F32),
            pltpu.VMEM((tk, tq), F32),
        ],
    )
    return pl.pallas_call(
        functools.partial(_attn_kernel, nkb=nkb),
        grid_spec=grid_spec,
        out_shape=jax.ShapeDtypeStruct((bsz, s, D_MODEL), BF16),
        compiler_params=_cparams(("parallel", "parallel", "arbitrary")),
        name="gqa_attention",
    )(bounded_flag, qt, kk, vt, proj)


def _shift_rows(tile, halo, shift):
    t = tile.shape[0]
    row8 = lax.broadcasted_iota(jnp.int32, (8, tile.shape[1]), 0)
    if shift > 0:
        rolled = pltpu.roll(tile, shift, axis=0)
        edge = pltpu.roll(halo, shift, axis=0)[0:8]
        fixed = jnp.where(row8 < shift, edge, rolled[0:8])
        return jnp.concatenate([fixed, rolled[8:]], axis=0)
    k = -shift
    rolled = pltpu.roll(tile, t - k, axis=0)
    edge = pltpu.roll(halo, HALO - k, axis=0)[HALO - 8:HALO]
    fixed = jnp.where(row8 >= 8 - k, edge, rolled[t - 8:t])
    return jnp.concatenate([rolled[:t - 8], fixed], axis=0)


def _lru_conv_gates(x_ref, prev_ref, next_ref, first, last, cw_ref, cb_ref, wr_ref, wi_ref, d,
                    ua_ref, rpre_ref, ipre_ref):
    tile = x_ref[0].astype(F32)
    prev = jnp.where(first, 0.0, prev_ref[0].astype(F32))
    nxt = jnp.where(last, 0.0, next_ref[0].astype(F32))
    ua = (cw_ref[0:1, :] * _shift_rows(tile, prev, 2)
          + cw_ref[1:2, :] * _shift_rows(tile, prev, 1)
          + cw_ref[2:3, :] * tile
          + cw_ref[3:4, :] * _shift_rows(tile, nxt, -1)
          + cb_ref[...])
    ua_ref[d] = ua
    ub = ua.astype(BF16)
    for n in range(RNN_BLOCKS):
        cols = slice(n * RNN_BW, (n + 1) * RNN_BW)
        rpre_ref[d, :, cols] = jnp.dot(ub[:, cols], wr_ref[d, n], preferred_element_type=F32)
        ipre_ref[d, :, cols] = jnp.dot(ub[:, cols], wi_ref[d, n], preferred_element_type=F32)


def _lru_coefficients(ua, r_half, i_half, b_r_half, b_i_half, c_half):
    y = c_half + c_half * jnp.tanh(r_half + b_r_half)
    ua_half = 0.5 * ua
    gated = ua_half + ua_half * jnp.tanh(i_half + b_i_half)
    a = jnp.exp2(y * (-LOG2E))
    x = jnp.tanh(y) * (1.0 + a * a)
    root = jnp.where(x > 0.0, x * lax.rsqrt(x), x)
    return a, root * gated


LANES = 128
N_SLABS = D_MODEL // LANES


def _slab_pitch(t):
    return t + 8 if (t // 8) % 2 == 0 else t


def _lru_kernel(xf_ref, pf_ref, nf_ref, xb_ref, pb_ref, nb_ref, cw_ref, cb_ref, wr_ref, br_ref, wi_ref,
                bi_ref, lam_ref, hf_ref, hb_ref, cf_ref, cbk_ref, sf_ref, sb_ref, ua_ref, rpre_ref, ipre_ref,
                chan_ref, *coef_refs, nt):
    i = pl.program_id(1)
    tm = hf_ref.shape[1]
    pitch = _slab_pitch(tm)
    it = jnp.minimum(i, nt - 1)
    coef_sets = (coef_refs[:4], coef_refs[4:])

    @pl.when(i == 0)
    def _():
        cf_ref[...] = jnp.zeros(cf_ref.shape, F32)
        cbk_ref[...] = jnp.zeros(cbk_ref.shape, F32)
        for ref in coef_sets[1]:
            ref[...] = jnp.zeros(ref.shape, F32)

    _lru_conv_gates(xf_ref, pf_ref, nf_ref, it == 0, it == nt - 1, cw_ref, cb_ref, wr_ref, wi_ref, 0,
                    ua_ref, rpre_ref, ipre_ref)
    _lru_conv_gates(xb_ref, pb_ref, nb_ref, it == nt - 1, it == 0, cw_ref, cb_ref, wr_ref, wi_ref, 1,
                    ua_ref, rpre_ref, ipre_ref)
    nl = -lam_ref[...]
    c_half = (0.5 * LRU_C) * (jnp.maximum(nl, 0.0) + jnp.log1p(jnp.exp(-jnp.abs(nl))))
    for d in range(2):
        for n, row in enumerate((0.5 * br_ref[d:d + 1, :], 0.5 * bi_ref[d:d + 1, :], c_half[d:d + 1, :])):
            chan_ref[3 * d + n] = jnp.broadcast_to(row, chan_ref.shape[1:])

    def step(cur, prev):
        af_ref, uf_ref, ab_ref, ub_ref = prev

        def rows8(g, carry):
            hf, hb = carry
            r0 = pl.multiple_of(g * 8, 8)
            rows = pl.ds(r0, 8)
            for d in range(2):
                a, u = _lru_coefficients(ua_ref[d, rows, :], rpre_ref[d, rows, :], ipre_ref[d, rows, :],
                                         chan_ref[3 * d], chan_ref[3 * d + 1], chan_ref[3 * d + 2])
                for c in range(N_SLABS):
                    dst = pl.ds(c * pitch + r0, 8)
                    cur[2 * d][dst, :] = a[:, c * LANES:(c + 1) * LANES]
                    cur[2 * d + 1][dst, :] = u[:, c * LANES:(c + 1) * LANES]
            for j in range(8):
                k = r0 + j
                tf = pl.ds(k, N_SLABS, stride=pitch)
                tb = pl.ds(tm - 1 - k, N_SLABS, stride=pitch)
                hf = af_ref[tf, :] * hf + uf_ref[tf, :]
                sf_ref[tf, :] = hf
                hb = ab_ref[tb, :] * hb + ub_ref[tb, :]
                sb_ref[tb, :] = hb
            return hf, hb

        hf, hb = lax.fori_loop(0, tm // 8, rows8, (cf_ref[...], cbk_ref[...]), unroll=4)
        keep = i > 0
        cf_ref[...] = jnp.where(keep, hf, cf_ref[...])
        cbk_ref[...] = jnp.where(keep, hb, cbk_ref[...])
        for s_ref, out_ref in ((sf_ref, hf_ref), (sb_ref, hb_ref)):
            for c in range(N_SLABS):
                out_ref[0, :, c * LANES:(c + 1) * LANES] = s_ref[pl.ds(c * pitch, tm), :].astype(out_ref.dtype)

    @pl.when(i % 2 == 0)
    def _():
        step(coef_sets[0], coef_sets[1])

    @pl.when(i % 2 == 1)
    def _():
        step(coef_sets[1], coef_sets[0])


def _rg_lru(proj, conv_w, conv_b, w_rg, b_rg, w_ig, b_ig, lam):
    bsz, s, _ = proj.shape
    d = D_MODEL
    tm = min(256, s)
    nt = s // tm
    hb = tm // HALO
    nhalo = s // HALO

    def tile_specs(tile_of):
        return [
            pl.BlockSpec((1, tm, d), lambda b, i: (b, tile_of(i), COL_XA)),
            pl.BlockSpec((1, HALO, d), lambda b, i: (b, jnp.maximum(tile_of(i) * hb - 1, 0), COL_XA)),
            pl.BlockSpec((1, HALO, d), lambda b, i: (b, jnp.minimum((tile_of(i) + 1) * hb, nhalo - 1), COL_XA)),
        ]

    const2 = lambda b, i: (0, 0)
    const4 = lambda b, i: (0, 0, 0, 0)
    out_sds = jax.ShapeDtypeStruct((bsz, s, d), BF16)
    return pl.pallas_call(
        functools.partial(_lru_kernel, nt=nt),
        grid=(bsz, nt + 1),
        in_specs=tile_specs(lambda i: jnp.minimum(i, nt - 1)) + tile_specs(lambda i: nt - 1 - jnp.minimum(i, nt - 1)) + [
            pl.BlockSpec(conv_w.shape, const2),
            pl.BlockSpec((1, d), const2),
            pl.BlockSpec(w_rg.shape, const4),
            pl.BlockSpec(b_rg.shape, const2),
            pl.BlockSpec(w_ig.shape, const4),
            pl.BlockSpec(b_ig.shape, const2),
            pl.BlockSpec(lam.shape, const2),
        ],
        out_specs=[
            pl.BlockSpec((1, tm, d), lambda b, i: (b, jnp.maximum(i - 1, 0), 0)),
            pl.BlockSpec((1, tm, d), lambda b, i: (b, nt - 1 - jnp.maximum(i - 1, 0), 0)),
        ],
        out_shape=[out_sds, out_sds],
        scratch_shapes=[pltpu.VMEM((N_SLABS, LANES), F32)] * 2
        + [pltpu.VMEM((N_SLABS * _slab_pitch(tm), LANES), F32)] * 2
        + [pltpu.VMEM((2, tm, d), F32)] * 3
        + [pltpu.VMEM((6, 8, d), F32)]
        + [pltpu.VMEM((N_SLABS * _slab_pitch(tm), LANES), F32)] * 8,
        compiler_params=_cparams(("parallel", "arbitrary")),
        name="rg_lru",
    )(proj, proj, proj, proj, proj, proj, conv_w, conv_b.reshape(1, d), w_rg, b_rg, w_ig, b_ig, lam)


def _merge_kernel(hf_ref, hb_ref, ga_ref, yb_ref, xc_ref, bc_ref, cc_ref, gc_ref, xcp_ref, ccp_ref, xcn_ref,
                  ccn_ref, mg0_ref, mg1_ref, mg2_ref, x_ref, mod_ref, cw_ref, wb_ref, wo_ref, o_ref, *, nt):
    i = pl.program_id(1)
    merged = mg1_ref[0].astype(F32) * jnp.dot(yb_ref[0], wb_ref[1], preferred_element_type=F32)

    ya = (hf_ref[0].astype(F32) + hb_ref[0].astype(F32)) * ga_ref[0].astype(F32)
    merged += mg0_ref[0].astype(F32) * jnp.dot(ya.astype(BF16), wb_ref[0], preferred_element_type=F32)

    z = cc_ref[0].astype(F32) * xc_ref[0].astype(F32)
    z_prev = jnp.where(i == 0, 0.0, ccp_ref[0].astype(F32) * xcp_ref[0].astype(F32))
    z_next = jnp.where(i == nt - 1, 0.0, ccn_ref[0].astype(F32) * xcn_ref[0].astype(F32))
    conv = (cw_ref[0:1, :] * _shift_rows(z, z_prev, 1) + cw_ref[1:2, :] * z
            + cw_ref[2:3, :] * _shift_rows(z, z_next, -1))
    yc = bc_ref[0].astype(F32) * conv * gc_ref[0].astype(F32)

    merged += mg2_ref[0].astype(F32) * jnp.dot(yc.astype(BF16), wb_ref[2], preferred_element_type=F32)
    out = jnp.dot(merged.astype(BF16), wo_ref[...], preferred_element_type=F32)
    o_ref[0] = x_ref[0] + mod_ref[0, 2:3, :] * out


def _merge(hf, hb, yb, proj, x, mod, conv_c_w, w_branch, w_out):
    bsz, s, d = x.shape
    tm = min(256, s)
    nt = s // tm
    hbk = tm // HALO
    nhalo = s // HALO

    def col(c):
        return pl.BlockSpec((1, tm, d), lambda b, i: (b, i, c))

    def halo_prev(c):
        return pl.BlockSpec((1, HALO, d), lambda b, i: (b, jnp.maximum(i * hbk - 1, 0), c))

    def halo_next(c):
        return pl.BlockSpec((1, HALO, d), lambda b, i: (b, jnp.minimum((i + 1) * hbk, nhalo - 1), c))

    return pl.pallas_call(
        functools.partial(_merge_kernel, nt=nt),
        grid=(bsz, nt),
        in_specs=[
            col(0), col(0), col(COL_GA), col(0), col(COL_XC), col(COL_BC), col(COL_CC), col(COL_GC),
            halo_prev(COL_XC), halo_prev(COL_CC), halo_next(COL_XC), halo_next(COL_CC),
            col(COL_MG), col(COL_MG + 1), col(COL_MG + 2),
            col(0),
            pl.BlockSpec((1, 3, d), lambda b, i: (b, 0, 0)),
            pl.BlockSpec(conv_c_w.shape, lambda b, i: (0, 0)),
            pl.BlockSpec(w_branch.shape, lambda b, i: (0, 0, 0)),
            pl.BlockSpec(w_out.shape, lambda b, i: (0, 0)),
        ],
        out_specs=pl.BlockSpec((1, tm, d), lambda b, i: (b, i, 0)),
        out_shape=jax.ShapeDtypeStruct((bsz, s, d), F32),
        compiler_params=_cparams(("parallel", "parallel")),
        name="merge_outproj",
    )(hf, hb, proj, yb, proj, proj, proj, proj, proj, proj, proj, proj, proj, proj, proj, x, mod,
      conv_c_w, w_branch, w_out)


def _permute_w_in(w):
    d = D_MODEL
    o_q = 2 * d
    o_k = o_q + d
    o_v = o_k + KV_W
    o_gb = o_v + KV_W
    o_xc = o_gb + d
    return jnp.concatenate([w[:, :o_q], w[:, o_q:o_k], w[:, o_gb:o_xc], w[:, o_xc:], w[:, o_k:o_gb]], axis=1)


def _attn_tiles(s):
    tq = min(512, s)
    tk = min(512, s)
    return tq, tk


def kernel(x, c, w_ada, b_ada, norm_g, w_in, conv_a_w, conv_a_b, w_rg, b_rg, w_ig, b_ig, lru_lam, q_norm_g,
           k_norm_g, conv_c_w, w_branch, w_out):
    bsz, s, d = x.shape
    depth = w_ada.shape[0]
    tq, tk = _attn_tiles(s)
    cos_t, sin_t = _rope_tables(s)
    mods = _modulation(c, w_ada, b_ada)
    for l in range(depth):
        mod = mods[l]
        proj = _in_projection(x, mod, norm_g[l], _permute_w_in(w_in[l]).astype(BF16))
        hf, hb = _rg_lru(proj, conv_a_w[l], conv_a_b[l], (0.5 * w_rg[l]).astype(BF16), b_rg[l],
                         (0.5 * w_ig[l]).astype(BF16), b_ig[l], lru_lam[l])
        qt, kk, vt = _attn_prep(proj, cos_t, sin_t, q_norm_g[l], k_norm_g[l], tk)
        bounded = (_score_bound(q_norm_g[l], k_norm_g[l]) <= SCORE_BOUND_NOSHIFT).astype(jnp.int32).reshape(1)
        yb = _attention(qt, kk, vt, proj, bounded, tq)
        x = _merge(hf, hb, yb, proj, x, mod, conv_c_w[l], w_branch[l].astype(BF16), w_out[l].astype(BF16))
    return x
```

```python
import functools
import math

import numpy as np
import jax
import jax.numpy as jnp
from jax import lax
from jax.experimental import pallas as pl
from jax.experimental.pallas import tpu as pltpu

D_MODEL = 1024
EPS = 1e-6
GRID_W = 64
N_HEADS = 8
N_KV = 2
GROUP = N_HEADS // N_KV
HEAD_DIM = 128
RNN_BLOCKS = 4
RNN_BW = D_MODEL // RNN_BLOCKS
LRU_C = 8.0
ROPE_THETA = 10000.0
N_BRANCH = 3
LOG2E = 1.4426950408889634

COL_XA, COL_GA, COL_GB, COL_XC, COL_BC, COL_CC, COL_GC, COL_MG = 0, 1, 2, 3, 4, 5, 6, 7
KV_W = N_KV * HEAD_DIM
QKV_W = D_MODEL + 2 * KV_W
N_STORED = 10 * D_MODEL
N_IN = QKV_W + N_STORED

HALO = 16
VMEM_LIMIT = 56 * 1024 * 1024

F32 = jnp.float32
BF16 = jnp.bfloat16


def _sigmoid(x):
    return 1.0 / (1.0 + jnp.exp2(x * (-LOG2E)))


def _silu(x):
    return x * _sigmoid(x)


def _cparams(sem):
    return pltpu.CompilerParams(dimension_semantics=sem, vmem_limit_bytes=VMEM_LIMIT)


def _mod_kernel(c_ref, w_ref, b_ref, o_ref):
    acc = jnp.dot(c_ref[...], w_ref[0], preferred_element_type=F32, precision=lax.Precision.HIGHEST)
    o_ref[0] = acc + b_ref[0]


def _modulation(c, w_ada, b_ada):
    depth, d, n = w_ada.shape
    bsz = c.shape[0]
    rows = 8
    c8 = jnp.zeros((rows, d), F32).at[:bsz].set(c)
    tn = 1024
    out = pl.pallas_call(
        _mod_kernel,
        grid=(depth, n // tn),
        in_specs=[
            pl.BlockSpec((rows, d), lambda l, j: (0, 0)),
            pl.BlockSpec((1, d, tn), lambda l, j: (l, 0, j)),
            pl.BlockSpec((1, 1, tn), lambda l, j: (l, 0, j)),
        ],
        out_specs=pl.BlockSpec((1, rows, tn), lambda l, j: (l, 0, j)),
        out_shape=jax.ShapeDtypeStruct((depth, rows, n), F32),
        compiler_params=_cparams(("parallel", "parallel")),
        name="adaln_mod",
    )(c8, w_ada, b_ada.reshape(depth, 1, n))
    return out[:, :bsz].reshape(depth, bsz, 3, d)


INPROJ_TN = 512


def _rope_tables(s):
    half = HEAD_DIM // 2
    inv = ROPE_THETA ** (-np.arange(0, half, 2, dtype=np.float64) / half)
    rows = s // GRID_W
    ang_r = np.arange(rows, dtype=np.float64)[:, None] * inv
    ang_c = np.arange(GRID_W, dtype=np.float64)[:, None] * inv
    zr = np.zeros((rows, half))
    zc = np.zeros((GRID_W, half))
    row_cos = np.concatenate([np.cos(ang_r), np.cos(ang_r), zr], -1)
    row_sin = np.concatenate([-np.sin(ang_r), np.sin(ang_r), zr], -1)
    col_cos = np.concatenate([zc, np.cos(ang_c), np.cos(ang_c)], -1)
    col_sin = np.concatenate([zc, -np.sin(ang_c), np.sin(ang_c)], -1)
    bcast = lambda t: jnp.broadcast_to(jnp.asarray(t, F32)[:, None, :], (rows, 8, HEAD_DIM))
    return bcast(row_cos), bcast(row_sin), jnp.asarray(col_cos, F32), jnp.asarray(col_sin, F32)


def _norm_rope(xh, g, cos_t, sin_t, lane_lo):
    ms = jnp.mean(xh * xh, axis=-1, keepdims=True)
    y = xh * lax.rsqrt(ms + EPS) * g
    partner = jnp.where(lane_lo, pltpu.roll(y, HEAD_DIM - 32, axis=1), pltpu.roll(y, 32, axis=1))
    return y * cos_t + partner * sin_t


def _inproj_kernel(x_ref, mod_ref, g_ref, w_ref, rcos_ref, rsin_ref, ccos_ref, csin_ref, qg_ref, kg_ref,
                   o_ref, qt_ref, k_ref, vt_ref):
    tm = x_ref.shape[1]
    xv = x_ref[0]
    ms = jnp.mean(xv * xv, axis=-1, keepdims=True)
    y = xv * lax.rsqrt(ms + EPS) * g_ref[...]
    shift = mod_ref[0, 0:1, :]
    scale = mod_ref[0, 1:2, :]
    h = (y * (1.0 + scale) + shift).astype(BF16)

    reps = GRID_W // 8
    cos_t = jnp.concatenate([jnp.tile(rcos_ref[r], (reps, 1)) + ccos_ref[...] for r in range(tm // GRID_W)], 0)
    sin_t = jnp.concatenate([jnp.tile(rsin_ref[r], (reps, 1)) + csin_ref[...] for r in range(tm // GRID_W)], 0)
    lane = lax.broadcasted_iota(jnp.int32, cos_t.shape, 1)
    lane_lo = (lane % 64) < 32
    q_scale = (HEAD_DIM ** -0.5) * LOG2E

    for j in range(N_IN // INPROJ_TN):
        start = j * INPROJ_TN
        acc = jnp.dot(h, w_ref[:, start:start + INPROJ_TN], preferred_element_type=F32)
        if start < D_MODEL:
            for hh in range(INPROJ_TN // HEAD_DIM):
                qh = _norm_rope(acc[:, hh * HEAD_DIM:(hh + 1) * HEAD_DIM], qg_ref[...], cos_t, sin_t, lane_lo)
                qt_ref[0, start // HEAD_DIM + hh] = (qh * q_scale).T.astype(BF16)
        elif start < QKV_W:
            for hh in range(N_KV):
                kh = acc[:, hh * HEAD_DIM:(hh + 1) * HEAD_DIM]
                k_ref[0, hh, 0] = _norm_rope(kh, kg_ref[...], cos_t, sin_t, lane_lo).astype(BF16)
                vt_ref[0, hh, 0] = acc[:, KV_W + hh * HEAD_DIM:KV_W + (hh + 1) * HEAD_DIM].T.astype(BF16)
        else:
            ocol = start - QKV_W
            col = ocol // D_MODEL
            if col in (COL_GA, COL_GB, COL_GC):
                acc = _silu(acc)
            elif COL_MG <= col < COL_MG + N_BRANCH:
                acc = _sigmoid(acc)
            o_ref[0, :, ocol:ocol + INPROJ_TN] = acc.astype(BF16)


def _in_projection(x, mod, norm_g, w_in_p, tables, q_g, k_g, tk):
    bsz, s, d = x.shape
    tm = min(256, s)
    assert INPROJ_TN == 2 * KV_W and D_MODEL % INPROJ_TN == 0 and tm % GRID_W == 0 and tk % tm == 0
    nkb = s // tk
    sub = tk // tm
    rcos, rsin, ccos, csin = tables
    row_spec = pl.BlockSpec((tm // GRID_W, 8, HEAD_DIM), lambda b, i: (i, 0, 0))
    col_spec = pl.BlockSpec((GRID_W, HEAD_DIM), lambda b, i: (0, 0))
    gain_spec = pl.BlockSpec((1, HEAD_DIM), lambda b, i: (0, 0))
    return pl.pallas_call(
        _inproj_kernel,
        grid=(bsz, s // tm),
        in_specs=[
            pl.BlockSpec((1, tm, d), lambda b, i: (b, i, 0)),
            pl.BlockSpec((1, 3, d), lambda b, i: (b, 0, 0)),
            pl.BlockSpec((1, d), lambda b, i: (0, 0)),
            pl.BlockSpec((d, N_IN), lambda b, i: (0, 0), pipeline_mode=pl.Buffered(1)),
            row_spec, row_spec, col_spec, col_spec, gain_spec, gain_spec,
        ],
        out_specs=[
            pl.BlockSpec((1, tm, N_STORED), lambda b, i: (b, i, 0)),
            pl.BlockSpec((1, N_HEADS, HEAD_DIM, tm), lambda b, i: (b, 0, 0, i)),
            pl.BlockSpec((1, N_KV, 1, tm, HEAD_DIM), lambda b, i: (b, 0, i // sub, i % sub, 0)),
            pl.BlockSpec((1, N_KV, 1, HEAD_DIM, tm), lambda b, i: (b, 0, i // sub, 0, i % sub)),
        ],
        out_shape=[
            jax.ShapeDtypeStruct((bsz, s, N_STORED), BF16),
            jax.ShapeDtypeStruct((bsz, N_HEADS, HEAD_DIM, s), BF16),
            jax.ShapeDtypeStruct((bsz, N_KV, nkb, tk, HEAD_DIM), BF16),
            jax.ShapeDtypeStruct((bsz, N_KV, nkb, HEAD_DIM, tk), BF16),
        ],
        compiler_params=_cparams(("parallel", "parallel")),
        name="norm_inproj",
    )(x, mod, norm_g.reshape(1, d), w_in_p, rcos, rsin, ccos, csin, q_g.reshape(1, HEAD_DIM),
      k_g.reshape(1, HEAD_DIM))


SCORE_BOUND_NOSHIFT = 40.0


def _score_bound(q_g, k_g):
    q_scale = (HEAD_DIM ** -0.5) * LOG2E
    return HEAD_DIM * q_scale * jnp.max(jnp.abs(q_g)) * jnp.max(jnp.abs(k_g))


KV_BLOCKS_PER_TRIP = 8


def _largest_divisor(n, cap):
    return max(d for d in range(1, cap + 1) if n % d == 0)


def _attn_kernel(flag_ref, qt_ref, k_ref, vt_ref, gb_ref, o_ref, m_ref, l_ref, acc_ref, s_ref, *, nkb):
    tq = acc_ref.shape[-1]
    l_ref[...] = jnp.zeros(l_ref.shape, F32)
    acc_ref[...] = jnp.zeros(acc_ref.shape, F32)
    bounded = flag_ref[0] != 0

    @pl.when(bounded)
    def _():
        def scores(j, h):
            return jnp.dot(k_ref[0, 0, j], qt_ref[0, h], preferred_element_type=F32)

        s_ref[...] = scores(0, 0)

        group = _largest_divisor(nkb, KV_BLOCKS_PER_TRIP)

        def kv_group(g, carry):
            s_cur = s_ref[...]
            for u in range(group):
                j = g * group + u
                vblk = vt_ref[0, 0, j]
                for h in range(GROUP):
                    s_next = scores(j, h + 1) if h + 1 < GROUP else scores(jnp.minimum(j + 1, nkb - 1), 0)
                    p = jnp.exp2(s_cur)
                    l_ref[h] += jnp.sum(p.reshape(-1, 8, tq), axis=0)
                    acc_ref[h] += jnp.dot(vblk, p.astype(BF16), preferred_element_type=F32)
                    s_cur = s_next
            s_ref[...] = s_cur
            return carry

        lax.fori_loop(0, nkb // group, kv_group, 0)

    @pl.when(jnp.logical_not(bounded))
    def _():
        m_ref[...] = jnp.full(m_ref.shape, -jnp.inf, F32)

        def kv_step(j, carry):
            kblk = k_ref[0, 0, j]
            vblk = vt_ref[0, 0, j]
            for h in range(GROUP):
                s = jnp.dot(kblk, qt_ref[0, h], preferred_element_type=F32)
                m_old = m_ref[h]
                m_new = jnp.maximum(m_old, jnp.max(s, axis=0, keepdims=True))
                alpha = jnp.exp2(m_old - m_new)
                p = jnp.exp2(s - m_new)
                l_ref[h] = alpha * l_ref[h] + jnp.sum(p.reshape(-1, 8, tq), axis=0)
                acc_ref[h] = alpha * acc_ref[h] + jnp.dot(vblk, p.astype(BF16), preferred_element_type=F32)
                m_ref[h] = m_new
            return carry

        lax.fori_loop(0, nkb, kv_step, 0)

    for h in range(GROUP):
        l = jnp.sum(l_ref[h], axis=0, keepdims=True)
        o = acc_ref[h] * (1.0 / l)
        gate = gb_ref[0, :, h * HEAD_DIM:(h + 1) * HEAD_DIM].astype(F32)
        o_ref[0, :, h * HEAD_DIM:(h + 1) * HEAD_DIM] = (o.T * gate).astype(BF16)


def _attention(qt, kk, vt, proj, bounded_flag, tq):
    bsz, _, _, s = qt.shape
    nkb, tk = kk.shape[2], kk.shape[3]
    gw = GROUP * HEAD_DIM
    grid_spec = pltpu.PrefetchScalarGridSpec(
        num_scalar_prefetch=1,
        grid=(bsz, N_KV, s // tq),
        in_specs=[
            pl.BlockSpec((1, GROUP, HEAD_DIM, tq), lambda b, g, i, f: (b, g, 0, i)),
            pl.BlockSpec((1, 1, nkb, tk, HEAD_DIM), lambda b, g, i, f: (b, g, 0, 0, 0)),
            pl.BlockSpec((1, 1, nkb, HEAD_DIM, tk), lambda b, g, i, f: (b, g, 0, 0, 0)),
            pl.BlockSpec((1, tq, gw), lambda b, g, i, f: (b, i, COL_GB * D_MODEL // gw + g)),
        ],
        out_specs=pl.BlockSpec((1, tq, gw), lambda b, g, i, f: (b, i, g)),
        scratch_shapes=[
            pltpu.VMEM((GROUP, 1, tq), F32),
            pltpu.VMEM((GROUP, 8, tq), F32),
            pltpu.VMEM((GROUP, HEAD_DIM, tq), F32),
            pltpu.VMEM((tk, tq), F32),
        ],
    )
    return pl.pallas_call(
        functools.partial(_attn_kernel, nkb=nkb),
        grid_spec=grid_spec,
        out_shape=jax.ShapeDtypeStruct((bsz, s, D_MODEL), BF16),
        compiler_params=_cparams(("parallel", "parallel", "arbitrary")),
        name="gqa_attention",
    )(bounded_flag, qt, kk, vt, proj)


def _shift_rows(tile, halo, shift):
    t = tile.shape[0]
    row8 = lax.broadcasted_iota(jnp.int32, (8, tile.shape[1]), 0)
    if shift > 0:
        rolled = pltpu.roll(tile, shift, axis=0)
        edge = pltpu.roll(halo, shift, axis=0)[0:8]
        fixed = jnp.where(row8 < shift, edge, rolled[0:8])
        return jnp.concatenate([fixed, rolled[8:]], axis=0)
    k = -shift
    rolled = pltpu.roll(tile, t - k, axis=0)
    edge = pltpu.roll(halo, HALO - k, axis=0)[HALO - 8:HALO]
    fixed = jnp.where(row8 >= 8 - k, edge, rolled[t - 8:t])
    return jnp.concatenate([rolled[:t - 8], fixed], axis=0)


def _lru_conv_gates(x_ref, prev_ref, next_ref, first, last, cw_ref, cb_ref, wr_ref, wi_ref, d,
                    ua_ref, rpre_ref, ipre_ref):
    tile = x_ref[0].astype(F32)
    prev = jnp.where(first, 0.0, prev_ref[0].astype(F32))
    nxt = jnp.where(last, 0.0, next_ref[0].astype(F32))
    ua = (cw_ref[0:1, :] * _shift_rows(tile, prev, 2)
          + cw_ref[1:2, :] * _shift_rows(tile, prev, 1)
          + cw_ref[2:3, :] * tile
          + cw_ref[3:4, :] * _shift_rows(tile, nxt, -1)
          + cb_ref[...])
    ua_ref[d] = ua
    ub = ua.astype(BF16)
    for n in range(RNN_BLOCKS):
        cols = slice(n * RNN_BW, (n + 1) * RNN_BW)
        rpre_ref[d, :, cols] = jnp.dot(ub[:, cols], wr_ref[d, n], preferred_element_type=F32)
        ipre_ref[d, :, cols] = jnp.dot(ub[:, cols], wi_ref[d, n], preferred_element_type=F32)


def _lru_coefficients(ua, r_half, i_half, b_r_half, b_i_half, c_half):
    y = c_half + c_half * jnp.tanh(r_half + b_r_half)
    ua_half = 0.5 * ua
    gated = ua_half + ua_half * jnp.tanh(i_half + b_i_half)
    a = jnp.exp2(y * (-LOG2E))
    x = jnp.tanh(y) * (1.0 + a * a)
    root = jnp.where(x > 0.0, x * lax.rsqrt(x), x)
    return a, root * gated


LANES = 128
N_SLABS = D_MODEL // LANES


def _slab_pitch(t):
    return t + 8 if (t // 8) % 2 == 0 else t


def _lru_kernel(xf_ref, pf_ref, nf_ref, xb_ref, pb_ref, nb_ref, cw_ref, cb_ref, wr_ref, br_ref, wi_ref,
                bi_ref, lam_ref, hf_ref, hb_ref, cf_ref, cbk_ref, sf_ref, sb_ref, ua_ref, rpre_ref, ipre_ref,
                chan_ref, *coef_refs, nt):
    i = pl.program_id(1)
    tm = hf_ref.shape[1]
    pitch = _slab_pitch(tm)
    it = jnp.minimum(i, nt - 1)
    coef_sets = (coef_refs[:4], coef_refs[4:])

    @pl.when(i == 0)
    def _():
        cf_ref[...] = jnp.zeros(cf_ref.shape, F32)
        cbk_ref[...] = jnp.zeros(cbk_ref.shape, F32)
        for ref in coef_sets[1]:
            ref[...] = jnp.zeros(ref.shape, F32)

    _lru_conv_gates(xf_ref, pf_ref, nf_ref, it == 0, it == nt - 1, cw_ref, cb_ref, wr_ref, wi_ref, 0,
                    ua_ref, rpre_ref, ipre_ref)
    _lru_conv_gates(xb_ref, pb_ref, nb_ref, it == nt - 1, it == 0, cw_ref, cb_ref, wr_ref, wi_ref, 1,
                    ua_ref, rpre_ref, ipre_ref)
    nl = -lam_ref[...]
    c_half = (0.5 * LRU_C) * (jnp.maximum(nl, 0.0) + jnp.log1p(jnp.exp(-jnp.abs(nl))))
    for d in range(2):
        for n, row in enumerate((0.5 * br_ref[d:d + 1, :], 0.5 * bi_ref[d:d + 1, :], c_half[d:d + 1, :])):
            chan_ref[3 * d + n] = jnp.broadcast_to(row, chan_ref.shape[1:])

    def step(cur, prev):
        af_ref, uf_ref, ab_ref, ub_ref = prev

        def rows8(g, carry):
            hf, hb = carry
            r0 = pl.multiple_of(g * 8, 8)
            rows = pl.ds(r0, 8)
            for d in range(2):
                a, u = _lru_coefficients(ua_ref[d, rows, :], rpre_ref[d, rows, :], ipre_ref[d, rows, :],
                                         chan_ref[3 * d], chan_ref[3 * d + 1], chan_ref[3 * d + 2])
                for c in range(N_SLABS):
                    dst = pl.ds(c * pitch + r0, 8)
                    cur[2 * d][dst, :] = a[:, c * LANES:(c + 1) * LANES]
                    cur[2 * d + 1][dst, :] = u[:, c * LANES:(c + 1) * LANES]
            for j in range(8):
                k = r0 + j
                tf = pl.ds(k, N_SLABS, stride=pitch)
                tb = pl.ds(tm - 1 - k, N_SLABS, stride=pitch)
                hf = af_ref[tf, :] * hf + uf_ref[tf, :]
                sf_ref[tf, :] = hf
                hb = ab_ref[tb, :] * hb + ub_ref[tb, :]
                sb_ref[tb, :] = hb
            return hf, hb

        hf, hb = lax.fori_loop(0, tm // 8, rows8, (cf_ref[...], cbk_ref[...]), unroll=4)
        keep = i > 0
        cf_ref[...] = jnp.where(keep, hf, cf_ref[...])
        cbk_ref[...] = jnp.where(keep, hb, cbk_ref[...])
        for s_ref, out_ref in ((sf_ref, hf_ref), (sb_ref, hb_ref)):
            for c in range(N_SLABS):
                out_ref[0, :, c * LANES:(c + 1) * LANES] = s_ref[pl.ds(c * pitch, tm), :].astype(out_ref.dtype)

    @pl.when(i % 2 == 0)
    def _():
        step(coef_sets[0], coef_sets[1])

    @pl.when(i % 2 == 1)
    def _():
        step(coef_sets[1], coef_sets[0])


def _rg_lru(proj, conv_w, conv_b, w_rg, b_rg, w_ig, b_ig, lam):
    bsz, s, _ = proj.shape
    d = D_MODEL
    tm = min(256, s)
    nt = s // tm
    hb = tm // HALO
    nhalo = s // HALO

    def tile_specs(tile_of):
        return [
            pl.BlockSpec((1, tm, d), lambda b, i: (b, tile_of(i), COL_XA)),
            pl.BlockSpec((1, HALO, d), lambda b, i: (b, jnp.maximum(tile_of(i) * hb - 1, 0), COL_XA)),
            pl.BlockSpec((1, HALO, d), lambda b, i: (b, jnp.minimum((tile_of(i) + 1) * hb, nhalo - 1), COL_XA)),
        ]

    const2 = lambda b, i: (0, 0)
    const4 = lambda b, i: (0, 0, 0, 0)
    out_sds = jax.ShapeDtypeStruct((bsz, s, d), BF16)
    return pl.pallas_call(
        functools.partial(_lru_kernel, nt=nt),
        grid=(bsz, nt + 1),
        in_specs=tile_specs(lambda i: jnp.minimum(i, nt - 1)) + tile_specs(lambda i: nt - 1 - jnp.minimum(i, nt - 1)) + [
            pl.BlockSpec(conv_w.shape, const2),
            pl.BlockSpec((1, d), const2),
            pl.BlockSpec(w_rg.shape, const4),
            pl.BlockSpec(b_rg.shape, const2),
            pl.BlockSpec(w_ig.shape, const4),
            pl.BlockSpec(b_ig.shape, const2),
            pl.BlockSpec(lam.shape, const2),
        ],
        out_specs=[
            pl.BlockSpec((1, tm, d), lambda b, i: (b, jnp.maximum(i - 1, 0), 0)),
            pl.BlockSpec((1, tm, d), lambda b, i: (b, nt - 1 - jnp.maximum(i - 1, 0), 0)),
        ],
        out_shape=[out_sds, out_sds],
        scratch_shapes=[pltpu.VMEM((N_SLABS, LANES), F32)] * 2
        + [pltpu.VMEM((N_SLABS * _slab_pitch(tm), LANES), F32)] * 2
        + [pltpu.VMEM((2, tm, d), F32)] * 3
        + [pltpu.VMEM((6, 8, d), F32)]
        + [pltpu.VMEM((N_SLABS * _slab_pitch(tm), LANES), F32)] * 8,
        compiler_params=_cparams(("parallel", "arbitrary")),
        name="rg_lru",
    )(proj, proj, proj, proj, proj, proj, conv_w, conv_b.reshape(1, d), w_rg, b_rg, w_ig, b_ig, lam)


def _merge_kernel(hf_ref, hb_ref, ga_ref, yb_ref, xc_ref, bc_ref, cc_ref, gc_ref, xcp_ref, ccp_ref, xcn_ref,
                  ccn_ref, mg0_ref, mg1_ref, mg2_ref, x_ref, mod_ref, cw_ref, wb_ref, wo_ref, o_ref, *, nt):
    i = pl.program_id(1)
    merged = mg1_ref[0].astype(F32) * jnp.dot(yb_ref[0], wb_ref[1], preferred_element_type=F32)

    ya = (hf_ref[0].astype(F32) + hb_ref[0].astype(F32)) * ga_ref[0].astype(F32)
    merged += mg0_ref[0].astype(F32) * jnp.dot(ya.astype(BF16), wb_ref[0], preferred_element_type=F32)

    z = cc_ref[0].astype(F32) * xc_ref[0].astype(F32)
    z_prev = jnp.where(i == 0, 0.0, ccp_ref[0].astype(F32) * xcp_ref[0].astype(F32))
    z_next = jnp.where(i == nt - 1, 0.0, ccn_ref[0].astype(F32) * xcn_ref[0].astype(F32))
    conv = (cw_ref[0:1, :] * _shift_rows(z, z_prev, 1) + cw_ref[1:2, :] * z
            + cw_ref[2:3, :] * _shift_rows(z, z_next, -1))
    yc = bc_ref[0].astype(F32) * conv * gc_ref[0].astype(F32)

    merged += mg2_ref[0].astype(F32) * jnp.dot(yc.astype(BF16), wb_ref[2], preferred_element_type=F32)
    out = jnp.dot(merged.astype(BF16), wo_ref[...], preferred_element_type=F32)
    o_ref[0] = x_ref[0] + mod_ref[0, 2:3, :] * out


def _merge(hf, hb, yb, proj, x, mod, conv_c_w, w_branch, w_out):
    bsz, s, d = x.shape
    tm = min(256, s)
    nt = s // tm
    hbk = tm // HALO
    nhalo = s // HALO

    def col(c):
        return pl.BlockSpec((1, tm, d), lambda b, i: (b, i, c))

    def halo_prev(c):
        return pl.BlockSpec((1, HALO, d), lambda b, i: (b, jnp.maximum(i * hbk - 1, 0), c))

    def halo_next(c):
        return pl.BlockSpec((1, HALO, d), lambda b, i: (b, jnp.minimum((i + 1) * hbk, nhalo - 1), c))

    return pl.pallas_call(
        functools.partial(_merge_kernel, nt=nt),
        grid=(bsz, nt),
        in_specs=[
            col(0), col(0), col(COL_GA), col(0), col(COL_XC), col(COL_BC), col(COL_CC), col(COL_GC),
            halo_prev(COL_XC), halo_prev(COL_CC), halo_next(COL_XC), halo_next(COL_CC),
            col(COL_MG), col(COL_MG + 1), col(COL_MG + 2),
            col(0),
            pl.BlockSpec((1, 3, d), lambda b, i: (b, 0, 0)),
            pl.BlockSpec(conv_c_w.shape, lambda b, i: (0, 0)),
            pl.BlockSpec(w_branch.shape, lambda b, i: (0, 0, 0)),
            pl.BlockSpec(w_out.shape, lambda b, i: (0, 0)),
        ],
        out_specs=pl.BlockSpec((1, tm, d), lambda b, i: (b, i, 0)),
        out_shape=jax.ShapeDtypeStruct((bsz, s, d), F32),
        compiler_params=_cparams(("parallel", "parallel")),
        name="merge_outproj",
    )(hf, hb, proj, yb, proj, proj, proj, proj, proj, proj, proj, proj, proj, proj, proj, x, mod,
      conv_c_w, w_branch, w_out)


def _permute_w_in(w):
    o_q = 2 * D_MODEL
    o_gb = o_q + QKV_W
    return jnp.concatenate([w[:, o_q:o_gb], w[:, :o_q], w[:, o_gb:]], axis=1)


def _attn_tiles(s):
    tq = min(512, s)
    tk = min(512, s)
    return tq, tk


def kernel(x, c, w_ada, b_ada, norm_g, w_in, conv_a_w, conv_a_b, w_rg, b_rg, w_ig, b_ig, lru_lam, q_norm_g,
           k_norm_g, conv_c_w, w_branch, w_out):
    bsz, s, d = x.shape
    depth = w_ada.shape[0]
    tq, tk = _attn_tiles(s)
    tables = _rope_tables(s)
    mods = _modulation(c, w_ada, b_ada)
    for l in range(depth):
        mod = mods[l]
        proj, qt, kk, vt = _in_projection(x, mod, norm_g[l], _permute_w_in(w_in[l]).astype(BF16), tables,
                                          q_norm_g[l], k_norm_g[l], tk)
        hf, hb = _rg_lru(proj, conv_a_w[l], conv_a_b[l], (0.5 * w_rg[l]).astype(BF16), b_rg[l],
                         (0.5 * w_ig[l]).astype(BF16), b_ig[l], lru_lam[l])
        bounded = (_score_bound(q_norm_g[l], k_norm_g[l]) <= SCORE_BOUND_NOSHIFT).astype(jnp.int32).reshape(1)
        yb = _attention(qt, kk, vt, proj, bounded, tq)
        x = _merge(hf, hb, yb, proj, x, mod, conv_c_w[l], w_branch[l].astype(BF16), w_out[l].astype(BF16))
    return x
```

```python
import functools
import math

import numpy as np
import jax
import jax.numpy as jnp
from jax import lax
from jax.experimental import pallas as pl
from jax.experimental.pallas import tpu as pltpu

D_MODEL = 1024
EPS = 1e-6
GRID_W = 64
N_HEADS = 8
N_KV = 2
GROUP = N_HEADS // N_KV
HEAD_DIM = 128
RNN_BLOCKS = 4
RNN_BW = D_MODEL // RNN_BLOCKS
LRU_C = 8.0
ROPE_THETA = 10000.0
N_BRANCH = 3
LOG2E = 1.4426950408889634

KV_W = N_KV * HEAD_DIM
QKV_W = D_MODEL + 2 * KV_W
N_IN = 11 * D_MODEL + 2 * KV_W
INPROJ_TN = 512

COL_UA, COL_GA, COL_GB, COL_YC, COL_MG = 0, 1, 2, 3, 4
N_STORED = (COL_MG + N_BRANCH) * D_MODEL

_SRC = dict(xa=0, ga=D_MODEL, q=2 * D_MODEL, kv=3 * D_MODEL, gb=3 * D_MODEL + 2 * KV_W)
_SRC.update(xc=_SRC["gb"] + D_MODEL, bc=_SRC["gb"] + 2 * D_MODEL, cc=_SRC["gb"] + 3 * D_MODEL,
            gc=_SRC["gb"] + 4 * D_MODEL, mg=_SRC["gb"] + 5 * D_MODEL)
INPROJ_TILES = (
    [("q", _SRC["q"] + t * INPROJ_TN, t) for t in range(2)] + [("kv", _SRC["kv"], 0)]
    + [("xa", _SRC["xa"] + t * INPROJ_TN, t) for t in range(2)]
    + [("ga", _SRC["ga"] + t * INPROJ_TN, t) for t in range(2)]
    + [("gb", _SRC["gb"] + t * INPROJ_TN, t) for t in range(2)]
    + [(k, _SRC[k] + t * INPROJ_TN, t) for t in range(2) for k in ("xc", "cc", "bc", "gc")]
    + [("mg", _SRC["mg"] + t * INPROJ_TN, t) for t in range(2 * N_BRANCH)]
)

HALO = 16
VMEM_LIMIT = 56 * 1024 * 1024

F32 = jnp.float32
BF16 = jnp.bfloat16


def _sigmoid(x):
    return 1.0 / (1.0 + jnp.exp2(x * (-LOG2E)))


def _silu(x):
    return x * _sigmoid(x)


def _cparams(sem):
    return pltpu.CompilerParams(dimension_semantics=sem, vmem_limit_bytes=VMEM_LIMIT)


def _mod_kernel(c_ref, w_ref, b_ref, o_ref):
    acc = jnp.dot(c_ref[...], w_ref[0], preferred_element_type=F32, precision=lax.Precision.HIGHEST)
    o_ref[0] = acc + b_ref[0]


def _modulation(c, w_ada, b_ada):
    depth, d, n = w_ada.shape
    bsz = c.shape[0]
    rows = 8
    c8 = jnp.zeros((rows, d), F32).at[:bsz].set(c)
    tn = 1024
    out = pl.pallas_call(
        _mod_kernel,
        grid=(depth, n // tn),
        in_specs=[
            pl.BlockSpec((rows, d), lambda l, j: (0, 0)),
            pl.BlockSpec((1, d, tn), lambda l, j: (l, 0, j)),
            pl.BlockSpec((1, 1, tn), lambda l, j: (l, 0, j)),
        ],
        out_specs=pl.BlockSpec((1, rows, tn), lambda l, j: (l, 0, j)),
        out_shape=jax.ShapeDtypeStruct((depth, rows, n), F32),
        compiler_params=_cparams(("parallel", "parallel")),
        name="adaln_mod",
    )(c8, w_ada, b_ada.reshape(depth, 1, n))
    return out[:, :bsz].reshape(depth, bsz, 3, d)


def _rope_tables(s):
    half = HEAD_DIM // 2
    inv = ROPE_THETA ** (-np.arange(0, half, 2, dtype=np.float64) / half)
    rows = s // GRID_W
    ang_r = np.arange(rows, dtype=np.float64)[:, None] * inv
    ang_c = np.arange(GRID_W, dtype=np.float64)[:, None] * inv
    zr = np.zeros((rows, half))
    zc = np.zeros((GRID_W, half))
    row_cos = np.concatenate([np.cos(ang_r), np.cos(ang_r), zr], -1)
    row_sin = np.concatenate([-np.sin(ang_r), np.sin(ang_r), zr], -1)
    col_cos = np.concatenate([zc, np.cos(ang_c), np.cos(ang_c)], -1)
    col_sin = np.concatenate([zc, -np.sin(ang_c), np.sin(ang_c)], -1)
    bcast = lambda t: jnp.broadcast_to(jnp.asarray(t, F32)[:, None, :], (rows, 8, HEAD_DIM))
    return bcast(row_cos), bcast(row_sin), jnp.asarray(col_cos, F32), jnp.asarray(col_sin, F32)


def _norm_rope(xh, g, cos_t, sin_t, lane_lo):
    ms = jnp.mean(xh * xh, axis=-1, keepdims=True)
    y = xh * lax.rsqrt(ms + EPS) * g
    partner = jnp.where(lane_lo, pltpu.roll(y, HEAD_DIM - 32, axis=1), pltpu.roll(y, 32, axis=1))
    return y * cos_t + partner * sin_t


def _shift_rows(tile, halo, shift):
    t = tile.shape[0]
    row8 = lax.broadcasted_iota(jnp.int32, (8, tile.shape[1]), 0)
    if shift > 0:
        rolled = pltpu.roll(tile, shift, axis=0)
        edge = pltpu.roll(halo, shift, axis=0)[0:8]
        fixed = jnp.where(row8 < shift, edge, rolled[0:8])
        return jnp.concatenate([fixed, rolled[8:]], axis=0)
    k = -shift
    rolled = pltpu.roll(tile, t - k, axis=0)
    edge = pltpu.roll(halo, HALO - k, axis=0)[HALO - 8:HALO]
    fixed = jnp.where(row8 >= 8 - k, edge, rolled[t - 8:t])
    return jnp.concatenate([rolled[:t - 8], fixed], axis=0)


def _inproj_kernel(x_ref, xp_ref, xn_ref, mod_ref, g_ref, w_ref, rcos_ref, rsin_ref, ccos_ref, csin_ref, qg_ref,
                   kg_ref, caw_ref, cab_ref, ccw_ref, o_ref, qt_ref, k_ref, vt_ref, *, nt):
    i = pl.program_id(1)
    tm = x_ref.shape[1]
    xv = jnp.concatenate([xp_ref[0], x_ref[0], xn_ref[0]], axis=0)
    ms = jnp.mean(xv * xv, axis=-1, keepdims=True)
    y = xv * lax.rsqrt(ms + EPS) * g_ref[...]
    shift = mod_ref[0, 0:1, :]
    scale = mod_ref[0, 1:2, :]
    h_ext = (y * (1.0 + scale) + shift).astype(BF16)
    h = h_ext[HALO:HALO + tm]

    def with_halos(acc_ext):
        prev = jnp.where(i == 0, 0.0, acc_ext[0:HALO])
        nxt = jnp.where(i == nt - 1, 0.0, acc_ext[HALO + tm:2 * HALO + tm])
        return acc_ext[HALO:HALO + tm], prev, nxt

    reps = GRID_W // 8
    cos_t = jnp.concatenate([jnp.tile(rcos_ref[r], (reps, 1)) + ccos_ref[...] for r in range(tm // GRID_W)], 0)
    sin_t = jnp.concatenate([jnp.tile(rsin_ref[r], (reps, 1)) + csin_ref[...] for r in range(tm // GRID_W)], 0)
    lane = lax.broadcasted_iota(jnp.int32, cos_t.shape, 1)
    lane_lo = (lane % 64) < 32
    q_scale = (HEAD_DIM ** -0.5) * LOG2E

    held = {}
    for j, (kind, _, half) in enumerate(INPROJ_TILES):
        w_tile = w_ref[:, j * INPROJ_TN:(j + 1) * INPROJ_TN]
        lanes = slice(half * INPROJ_TN, (half + 1) * INPROJ_TN)
        if kind in ("xa", "xc", "cc"):
            acc = jnp.dot(h_ext, w_tile, preferred_element_type=F32)
        else:
            acc = jnp.dot(h, w_tile, preferred_element_type=F32)

        def store(col, val):
            o_ref[0, :, col * D_MODEL + half * INPROJ_TN:col * D_MODEL + (half + 1) * INPROJ_TN] = val.astype(BF16)

        if kind == "q":
            for hh in range(INPROJ_TN // HEAD_DIM):
                qh = _norm_rope(acc[:, hh * HEAD_DIM:(hh + 1) * HEAD_DIM], qg_ref[...], cos_t, sin_t, lane_lo)
                qt_ref[0, half * (INPROJ_TN // HEAD_DIM) + hh] = (qh * q_scale).T.astype(BF16)
        elif kind == "kv":
            for hh in range(N_KV):
                kh = acc[:, hh * HEAD_DIM:(hh + 1) * HEAD_DIM]
                k_ref[0, hh, 0] = _norm_rope(kh, kg_ref[...], cos_t, sin_t, lane_lo).astype(BF16)
                vt_ref[0, hh, 0] = acc[:, KV_W + hh * HEAD_DIM:KV_W + (hh + 1) * HEAD_DIM].T.astype(BF16)
        elif kind == "xa":
            tile, prev, nxt = with_halos(acc)
            cw = caw_ref[:, lanes]
            store(COL_UA, cw[0:1] * _shift_rows(tile, prev, 2) + cw[1:2] * _shift_rows(tile, prev, 1)
                  + cw[2:3] * tile + cw[3:4] * _shift_rows(tile, nxt, -1) + cab_ref[:, lanes])
        elif kind == "xc":
            held["xc"] = acc
        elif kind == "cc":
            tile, prev, nxt = with_halos(acc * held.pop("xc"))
            cw = ccw_ref[:, lanes]
            held["conv"] = cw[0:1] * _shift_rows(tile, prev, 1) + cw[1:2] * tile + cw[2:3] * _shift_rows(tile, nxt, -1)
        elif kind == "bc":
            held["conv"] = acc * held["conv"]
        elif kind == "gc":
            store(COL_YC, held.pop("conv") * _silu(acc))
        elif kind == "ga":
            store(COL_GA, _silu(acc))
        elif kind == "gb":
            store(COL_GB, _silu(acc))
        else:
            o_ref[0, :, COL_MG * D_MODEL + half * INPROJ_TN:COL_MG * D_MODEL + (half + 1) * INPROJ_TN] = (
                _sigmoid(acc).astype(BF16))


def _in_projection(x, mod, norm_g, w_in_p, tables, q_g, k_g, conv_a_w, conv_a_b, conv_c_w, tk):
    bsz, s, d = x.shape
    tm = min(256, s)
    assert INPROJ_TN == 2 * KV_W and D_MODEL == 2 * INPROJ_TN and tm % GRID_W == 0 and tk % tm == 0
    nt = s // tm
    nkb = s // tk
    sub = tk // tm
    hbk = tm // HALO
    rcos, rsin, ccos, csin = tables
    row_spec = pl.BlockSpec((tm // GRID_W, 8, HEAD_DIM), lambda b, i: (i, 0, 0))
    col_spec = pl.BlockSpec((GRID_W, HEAD_DIM), lambda b, i: (0, 0))
    gain_spec = pl.BlockSpec((1, HEAD_DIM), lambda b, i: (0, 0))
    const2 = lambda b, i: (0, 0)
    return pl.pallas_call(
        functools.partial(_inproj_kernel, nt=nt),
        grid=(bsz, nt),
        in_specs=[
            pl.BlockSpec((1, tm, d), lambda b, i: (b, i, 0)),
            pl.BlockSpec((1, HALO, d), lambda b, i: (b, jnp.maximum(i * hbk - 1, 0), 0)),
            pl.BlockSpec((1, HALO, d), lambda b, i: (b, jnp.minimum((i + 1) * hbk, s // HALO - 1), 0)),
            pl.BlockSpec((1, 3, d), lambda b, i: (b, 0, 0)),
            pl.BlockSpec((1, d), const2),
            pl.BlockSpec((d, N_IN), const2, pipeline_mode=pl.Buffered(1)),
            row_spec, row_spec, col_spec, col_spec, gain_spec, gain_spec,
            pl.BlockSpec(conv_a_w.shape, const2),
            pl.BlockSpec((1, d), const2),
            pl.BlockSpec(conv_c_w.shape, const2),
        ],
        out_specs=[
            pl.BlockSpec((1, tm, N_STORED), lambda b, i: (b, i, 0)),
            pl.BlockSpec((1, N_HEADS, HEAD_DIM, tm), lambda b, i: (b, 0, 0, i)),
            pl.BlockSpec((1, N_KV, 1, tm, HEAD_DIM), lambda b, i: (b, 0, i // sub, i % sub, 0)),
            pl.BlockSpec((1, N_KV, 1, HEAD_DIM, tm), lambda b, i: (b, 0, i // sub, 0, i % sub)),
        ],
        out_shape=[
            jax.ShapeDtypeStruct((bsz, s, N_STORED), BF16),
            jax.ShapeDtypeStruct((bsz, N_HEADS, HEAD_DIM, s), BF16),
            jax.ShapeDtypeStruct((bsz, N_KV, nkb, tk, HEAD_DIM), BF16),
            jax.ShapeDtypeStruct((bsz, N_KV, nkb, HEAD_DIM, tk), BF16),
        ],
        compiler_params=_cparams(("parallel", "parallel")),
        name="norm_inproj",
    )(x, x, x, mod, norm_g.reshape(1, d), w_in_p, rcos, rsin, ccos, csin, q_g.reshape(1, HEAD_DIM),
      k_g.reshape(1, HEAD_DIM), conv_a_w, conv_a_b.reshape(1, d), conv_c_w)


SCORE_BOUND_NOSHIFT = 40.0


def _score_bound(q_g, k_g):
    q_scale = (HEAD_DIM ** -0.5) * LOG2E
    return HEAD_DIM * q_scale * jnp.max(jnp.abs(q_g)) * jnp.max(jnp.abs(k_g))


KV_BLOCKS_PER_TRIP = 8


def _largest_divisor(n, cap):
    return max(d for d in range(1, cap + 1) if n % d == 0)


def _attn_kernel(flag_ref, qt_ref, k_ref, vt_ref, gb_ref, o_ref, m_ref, l_ref, acc_ref, s_ref, *, nkb):
    tq = acc_ref.shape[-1]
    l_ref[...] = jnp.zeros(l_ref.shape, F32)
    acc_ref[...] = jnp.zeros(acc_ref.shape, F32)
    bounded = flag_ref[0] != 0

    @pl.when(bounded)
    def _():
        def scores(j, h):
            return jnp.dot(k_ref[0, 0, j], qt_ref[0, h], preferred_element_type=F32)

        s_ref[...] = scores(0, 0)

        group = _largest_divisor(nkb, KV_BLOCKS_PER_TRIP)

        def kv_group(g, carry):
            s_cur = s_ref[...]
            for u in range(group):
                j = g * group + u
                vblk = vt_ref[0, 0, j]
                for h in range(GROUP):
                    s_next = scores(j, h + 1) if h + 1 < GROUP else scores(jnp.minimum(j + 1, nkb - 1), 0)
                    p = jnp.exp2(s_cur)
                    l_ref[h] += jnp.sum(p.reshape(-1, 8, tq), axis=0)
                    acc_ref[h] += jnp.dot(vblk, p.astype(BF16), preferred_element_type=F32)
                    s_cur = s_next
            s_ref[...] = s_cur
            return carry

        lax.fori_loop(0, nkb // group, kv_group, 0)

    @pl.when(jnp.logical_not(bounded))
    def _():
        m_ref[...] = jnp.full(m_ref.shape, -jnp.inf, F32)

        def kv_step(j, carry):
            kblk = k_ref[0, 0, j]
            vblk = vt_ref[0, 0, j]
            for h in range(GROUP):
                s = jnp.dot(kblk, qt_ref[0, h], preferred_element_type=F32)
                m_old = m_ref[h]
                m_new = jnp.maximum(m_old, jnp.max(s, axis=0, keepdims=True))
                alpha = jnp.exp2(m_old - m_new)
                p = jnp.exp2(s - m_new)
                l_ref[h] = alpha * l_ref[h] + jnp.sum(p.reshape(-1, 8, tq), axis=0)
                acc_ref[h] = alpha * acc_ref[h] + jnp.dot(vblk, p.astype(BF16), preferred_element_type=F32)
                m_ref[h] = m_new
            return carry

        lax.fori_loop(0, nkb, kv_step, 0)

    for h in range(GROUP):
        l = jnp.sum(l_ref[h], axis=0, keepdims=True)
        o = acc_ref[h] * (1.0 / l)
        gate = gb_ref[0, :, h * HEAD_DIM:(h + 1) * HEAD_DIM].astype(F32)
        o_ref[0, :, h * HEAD_DIM:(h + 1) * HEAD_DIM] = (o.T * gate).astype(BF16)


def _attention(qt, kk, vt, proj, bounded_flag, tq):
    bsz, _, _, s = qt.shape
    nkb, tk = kk.shape[2], kk.shape[3]
    gw = GROUP * HEAD_DIM
    grid_spec = pltpu.PrefetchScalarGridSpec(
        num_scalar_prefetch=1,
        grid=(bsz, N_KV, s // tq),
        in_specs=[
            pl.BlockSpec((1, GROUP, HEAD_DIM, tq), lambda b, g, i, f: (b, g, 0, i)),
            pl.BlockSpec((1, 1, nkb, tk, HEAD_DIM), lambda b, g, i, f: (b, g, 0, 0, 0)),
            pl.BlockSpec((1, 1, nkb, HEAD_DIM, tk), lambda b, g, i, f: (b, g, 0, 0, 0)),
            pl.BlockSpec((1, tq, gw), lambda b, g, i, f: (b, i, COL_GB * D_MODEL // gw + g)),
        ],
        out_specs=pl.BlockSpec((1, tq, gw), lambda b, g, i, f: (b, i, g)),
        scratch_shapes=[
            pltpu.VMEM((GROUP, 1, tq), F32),
            pltpu.VMEM((GROUP, 8, tq), F32),
            pltpu.VMEM((GROUP, HEAD_DIM, tq), F32),
            pltpu.VMEM((tk, tq), F32),
        ],
    )
    return pl.pallas_call(
        functools.partial(_attn_kernel, nkb=nkb),
        grid_spec=grid_spec,
        out_shape=jax.ShapeDtypeStruct((bsz, s, D_MODEL), BF16),
        compiler_params=_cparams(("parallel", "parallel", "arbitrary")),
        name="gqa_attention",
    )(bounded_flag, qt, kk, vt, proj)


def _lru_gates(x_ref, wr_ref, wi_ref, d, ua_ref, rpre_ref, ipre_ref):
    ub = x_ref[0]
    ua_ref[d] = ub.astype(F32)
    for n in range(RNN_BLOCKS):
        cols = slice(n * RNN_BW, (n + 1) * RNN_BW)
        rpre_ref[d, :, cols] = jnp.dot(ub[:, cols], wr_ref[d, n], preferred_element_type=F32)
        ipre_ref[d, :, cols] = jnp.dot(ub[:, cols], wi_ref[d, n], preferred_element_type=F32)


def _lru_coefficients(ua, r_half, i_half, b_r_half, b_i_half, c_half):
    y = c_half + c_half * jnp.tanh(r_half + b_r_half)
    ua_half = 0.5 * ua
    gated = ua_half + ua_half * jnp.tanh(i_half + b_i_half)
    a = jnp.exp2(y * (-LOG2E))
    x = jnp.tanh(y) * (1.0 + a * a)
    root = jnp.where(x > 0.0, x * lax.rsqrt(x), x)
    return a, root * gated


LANES = 128
N_SLABS = D_MODEL // LANES


def _slab_pitch(t):
    return t + 8 if (t // 8) % 2 == 0 else t


def _lru_kernel(xf_ref, xb_ref, wr_ref, br_ref, wi_ref, bi_ref, lam_ref, hf_ref, hb_ref, cf_ref, cbk_ref,
                sf_ref, sb_ref, ua_ref, rpre_ref, ipre_ref, chan_ref, *coef_refs, nt):
    i = pl.program_id(1)
    tm = hf_ref.shape[1]
    pitch = _slab_pitch(tm)
    coef_sets = (coef_refs[:4], coef_refs[4:])

    @pl.when(i == 0)
    def _():
        cf_ref[...] = jnp.zeros(cf_ref.shape, F32)
        cbk_ref[...] = jnp.zeros(cbk_ref.shape, F32)
        for ref in coef_sets[1]:
            ref[...] = jnp.zeros(ref.shape, F32)

    _lru_gates(xf_ref, wr_ref, wi_ref, 0, ua_ref, rpre_ref, ipre_ref)
    _lru_gates(xb_ref, wr_ref, wi_ref, 1, ua_ref, rpre_ref, ipre_ref)
    nl = -lam_ref[...]
    c_half = (0.5 * LRU_C) * (jnp.maximum(nl, 0.0) + jnp.log1p(jnp.exp(-jnp.abs(nl))))
    for d in range(2):
        for n, row in enumerate((0.5 * br_ref[d:d + 1, :], 0.5 * bi_ref[d:d + 1, :], c_half[d:d + 1, :])):
            chan_ref[3 * d + n] = jnp.broadcast_to(row, chan_ref.shape[1:])

    def step(cur, prev):
        af_ref, uf_ref, ab_ref, ub_ref = prev

        def rows8(g, carry):
            hf, hb = carry
            r0 = pl.multiple_of(g * 8, 8)
            rows = pl.ds(r0, 8)
            for d in range(2):
                a, u = _lru_coefficients(ua_ref[d, rows, :], rpre_ref[d, rows, :], ipre_ref[d, rows, :],
                                         chan_ref[3 * d], chan_ref[3 * d + 1], chan_ref[3 * d + 2])
                for c in range(N_SLABS):
                    dst = pl.ds(c * pitch + r0, 8)
                    cur[2 * d][dst, :] = a[:, c * LANES:(c + 1) * LANES]
                    cur[2 * d + 1][dst, :] = u[:, c * LANES:(c + 1) * LANES]
            for j in range(8):
                k = r0 + j
                tf = pl.ds(k, N_SLABS, stride=pitch)
                tb = pl.ds(tm - 1 - k, N_SLABS, stride=pitch)
                hf = af_ref[tf, :] * hf + uf_ref[tf, :]
                sf_ref[tf, :] = hf
                hb = ab_ref[tb, :] * hb + ub_ref[tb, :]
                sb_ref[tb, :] = hb
            return hf, hb

        hf, hb = lax.fori_loop(0, tm // 8, rows8, (cf_ref[...], cbk_ref[...]), unroll=4)
        keep = i > 0
        cf_ref[...] = jnp.where(keep, hf, cf_ref[...])
        cbk_ref[...] = jnp.where(keep, hb, cbk_ref[...])
        for s_ref, out_ref in ((sf_ref, hf_ref), (sb_ref, hb_ref)):
            for c in range(N_SLABS):
                out_ref[0, :, c * LANES:(c + 1) * LANES] = s_ref[pl.ds(c * pitch, tm), :].astype(out_ref.dtype)

    @pl.when(i % 2 == 0)
    def _():
        step(coef_sets[0], coef_sets[1])

    @pl.when(i % 2 == 1)
    def _():
        step(coef_sets[1], coef_sets[0])


def _rg_lru(proj, w_rg, b_rg, w_ig, b_ig, lam):
    bsz, s, _ = proj.shape
    d = D_MODEL
    tm = min(256, s)
    nt = s // tm
    const2 = lambda b, i: (0, 0)
    const4 = lambda b, i: (0, 0, 0, 0)
    out_sds = jax.ShapeDtypeStruct((bsz, s, d), BF16)
    return pl.pallas_call(
        functools.partial(_lru_kernel, nt=nt),
        grid=(bsz, nt + 1),
        in_specs=[
            pl.BlockSpec((1, tm, d), lambda b, i: (b, jnp.minimum(i, nt - 1), COL_UA)),
            pl.BlockSpec((1, tm, d), lambda b, i: (b, nt - 1 - jnp.minimum(i, nt - 1), COL_UA)),
            pl.BlockSpec(w_rg.shape, const4),
            pl.BlockSpec(b_rg.shape, const2),
            pl.BlockSpec(w_ig.shape, const4),
            pl.BlockSpec(b_ig.shape, const2),
            pl.BlockSpec(lam.shape, const2),
        ],
        out_specs=[
            pl.BlockSpec((1, tm, d), lambda b, i: (b, jnp.maximum(i - 1, 0), 0)),
            pl.BlockSpec((1, tm, d), lambda b, i: (b, nt - 1 - jnp.maximum(i - 1, 0), 0)),
        ],
        out_shape=[out_sds, out_sds],
        scratch_shapes=[pltpu.VMEM((N_SLABS, LANES), F32)] * 2
        + [pltpu.VMEM((N_SLABS * _slab_pitch(tm), LANES), F32)] * 2
        + [pltpu.VMEM((2, tm, d), F32)] * 3
        + [pltpu.VMEM((6, 8, d), F32)]
        + [pltpu.VMEM((N_SLABS * _slab_pitch(tm), LANES), F32)] * 8,
        compiler_params=_cparams(("parallel", "arbitrary")),
        name="rg_lru",
    )(proj, proj, w_rg, b_rg, w_ig, b_ig, lam)


def _merge_kernel(hf_ref, hb_ref, ga_ref, yb_ref, yc_ref, mg0_ref, mg1_ref, mg2_ref, x_ref, mod_ref, wb_ref,
                  wo_ref, o_ref):
    merged = mg1_ref[0].astype(F32) * jnp.dot(yb_ref[0], wb_ref[1], preferred_element_type=F32)
    merged += mg2_ref[0].astype(F32) * jnp.dot(yc_ref[0], wb_ref[2], preferred_element_type=F32)
    ya = (hf_ref[0].astype(F32) + hb_ref[0].astype(F32)) * ga_ref[0].astype(F32)
    merged += mg0_ref[0].astype(F32) * jnp.dot(ya.astype(BF16), wb_ref[0], preferred_element_type=F32)
    out = jnp.dot(merged.astype(BF16), wo_ref[...], preferred_element_type=F32)
    o_ref[0] = x_ref[0] + mod_ref[0, 2:3, :] * out


def _merge(hf, hb, yb, proj, x, mod, w_branch, w_out):
    bsz, s, d = x.shape
    tm = min(256, s)

    def col(c):
        return pl.BlockSpec((1, tm, d), lambda b, i: (b, i, c))

    return pl.pallas_call(
        _merge_kernel,
        grid=(bsz, s // tm),
        in_specs=[
            col(0), col(0), col(COL_GA), col(0), col(COL_YC), col(COL_MG), col(COL_MG + 1), col(COL_MG + 2),
            col(0),
            pl.BlockSpec((1, 3, d), lambda b, i: (b, 0, 0)),
            pl.BlockSpec(w_branch.shape, lambda b, i: (0, 0, 0)),
            pl.BlockSpec(w_out.shape, lambda b, i: (0, 0)),
        ],
        out_specs=pl.BlockSpec((1, tm, d), lambda b, i: (b, i, 0)),
        out_shape=jax.ShapeDtypeStruct((bsz, s, d), F32),
        compiler_params=_cparams(("parallel", "parallel")),
        name="merge_outproj",
    )(hf, hb, proj, yb, proj, proj, proj, proj, x, mod, w_branch, w_out)


def _permute_w_in(w):
    return jnp.concatenate([w[:, off:off + INPROJ_TN] for _, off, _ in INPROJ_TILES], axis=1)


def _attn_tiles(s):
    tq = min(512, s)
    tk = min(512, s)
    return tq, tk


def kernel(x, c, w_ada, b_ada, norm_g, w_in, conv_a_w, conv_a_b, w_rg, b_rg, w_ig, b_ig, lru_lam, q_norm_g,
           k_norm_g, conv_c_w, w_branch, w_out):
    bsz, s, d = x.shape
    depth = w_ada.shape[0]
    tq, tk = _attn_tiles(s)
    tables = _rope_tables(s)
    mods = _modulation(c, w_ada, b_ada)
    for l in range(depth):
        mod = mods[l]
        proj, qt, kk, vt = _in_projection(x, mod, norm_g[l], _permute_w_in(w_in[l]).astype(BF16), tables,
                                          q_norm_g[l], k_norm_g[l], conv_a_w[l], conv_a_b[l], conv_c_w[l], tk)
        hf, hb = _rg_lru(proj, (0.5 * w_rg[l]).astype(BF16), b_rg[l], (0.5 * w_ig[l]).astype(BF16), b_ig[l],
                         lru_lam[l])
        bounded = (_score_bound(q_norm_g[l], k_norm_g[l]) <= SCORE_BOUND_NOSHIFT).astype(jnp.int32).reshape(1)
        yb = _attention(qt, kk, vt, proj, bounded, tq)
        x = _merge(hf, hb, yb, proj, x, mod, w_branch[l].astype(BF16), w_out[l].astype(BF16))
    return x
```

```python
import functools
import math

import numpy as np
import jax
import jax.numpy as jnp
from jax import lax
from jax.experimental import pallas as pl
from jax.experimental.pallas import tpu as pltpu

D_MODEL = 1024
EPS = 1e-6
GRID_W = 64
N_HEADS = 8
N_KV = 2
GROUP = N_HEADS // N_KV
HEAD_DIM = 128
RNN_BLOCKS = 4
RNN_BW = D_MODEL // RNN_BLOCKS
LRU_C = 8.0
ROPE_THETA = 10000.0
N_BRANCH = 3
LOG2E = 1.4426950408889634

KV_W = N_KV * HEAD_DIM
QKV_W = D_MODEL + 2 * KV_W
N_IN = 11 * D_MODEL + 2 * KV_W
INPROJ_TN = 512

COL_UA, COL_GA, COL_GB, COL_YC, COL_MG = 0, 1, 2, 3, 4
N_STORED = (COL_MG + N_BRANCH) * D_MODEL

_SRC = dict(xa=0, ga=D_MODEL, q=2 * D_MODEL, kv=3 * D_MODEL, gb=3 * D_MODEL + 2 * KV_W)
_SRC.update(xc=_SRC["gb"] + D_MODEL, bc=_SRC["gb"] + 2 * D_MODEL, cc=_SRC["gb"] + 3 * D_MODEL,
            gc=_SRC["gb"] + 4 * D_MODEL, mg=_SRC["gb"] + 5 * D_MODEL)
_TILE_ORDER = (("q", 0), ("mg", 0), ("q", 1), ("mg", 1), ("kv", 0), ("mg", 2), ("xa", 0), ("mg", 3),
               ("xa", 1), ("mg", 4), ("xc", 0), ("cc", 0), ("ga", 0), ("bc", 0), ("gc", 0), ("mg", 5),
               ("xc", 1), ("cc", 1), ("ga", 1), ("bc", 1), ("gc", 1), ("gb", 0), ("gb", 1))
INPROJ_TILES = tuple((k, _SRC[k] + t * INPROJ_TN, t) for k, t in _TILE_ORDER)
assert len(INPROJ_TILES) * INPROJ_TN == N_IN

HALO = 16
VMEM_LIMIT = 56 * 1024 * 1024

F32 = jnp.float32
BF16 = jnp.bfloat16


def _sigmoid(x):
    return 1.0 / (1.0 + jnp.exp2(x * (-LOG2E)))


def _silu(x):
    return x * _sigmoid(x)


def _cparams(sem):
    return pltpu.CompilerParams(dimension_semantics=sem, vmem_limit_bytes=VMEM_LIMIT)


def _mod_kernel(c_ref, w_ref, b_ref, o_ref):
    acc = jnp.dot(c_ref[...], w_ref[0], preferred_element_type=F32, precision=lax.Precision.HIGHEST)
    o_ref[0] = acc + b_ref[0]


def _modulation(c, w_ada, b_ada):
    depth, d, n = w_ada.shape
    bsz = c.shape[0]
    rows = 8
    c8 = jnp.zeros((rows, d), F32).at[:bsz].set(c)
    tn = 1024
    out = pl.pallas_call(
        _mod_kernel,
        grid=(depth, n // tn),
        in_specs=[
            pl.BlockSpec((rows, d), lambda l, j: (0, 0)),
            pl.BlockSpec((1, d, tn), lambda l, j: (l, 0, j)),
            pl.BlockSpec((1, 1, tn), lambda l, j: (l, 0, j)),
        ],
        out_specs=pl.BlockSpec((1, rows, tn), lambda l, j: (l, 0, j)),
        out_shape=jax.ShapeDtypeStruct((depth, rows, n), F32),
        compiler_params=_cparams(("parallel", "parallel")),
        name="adaln_mod",
    )(c8, w_ada, b_ada.reshape(depth, 1, n))
    return out[:, :bsz].reshape(depth, bsz, 3, d)


def _rope_tables(s):
    half = HEAD_DIM // 2
    inv = ROPE_THETA ** (-np.arange(0, half, 2, dtype=np.float64) / half)
    rows = s // GRID_W
    ang_r = np.arange(rows, dtype=np.float64)[:, None] * inv
    ang_c = np.arange(GRID_W, dtype=np.float64)[:, None] * inv
    zr = np.zeros((rows, half))
    zc = np.zeros((GRID_W, half))
    row_cos = np.concatenate([np.cos(ang_r), np.cos(ang_r), zr], -1)
    row_sin = np.concatenate([-np.sin(ang_r), np.sin(ang_r), zr], -1)
    col_cos = np.concatenate([zc, np.cos(ang_c), np.cos(ang_c)], -1)
    col_sin = np.concatenate([zc, -np.sin(ang_c), np.sin(ang_c)], -1)
    bcast = lambda t: jnp.broadcast_to(jnp.asarray(t, F32)[:, None, :], (rows, 8, HEAD_DIM))
    return bcast(row_cos), bcast(row_sin), jnp.asarray(col_cos, F32), jnp.asarray(col_sin, F32)


def _norm_rope(xh, g, cos_t, sin_t, lane_lo):
    ms = jnp.mean(xh * xh, axis=-1, keepdims=True)
    y = xh * lax.rsqrt(ms + EPS) * g
    partner = jnp.where(lane_lo, pltpu.roll(y, HEAD_DIM - 32, axis=1), pltpu.roll(y, 32, axis=1))
    return y * cos_t + partner * sin_t


def _shift_rows(tile, halo, shift):
    t = tile.shape[0]
    row8 = lax.broadcasted_iota(jnp.int32, (8, tile.shape[1]), 0)
    if shift > 0:
        rolled = pltpu.roll(tile, shift, axis=0)
        edge = pltpu.roll(halo, shift, axis=0)[0:8]
        fixed = jnp.where(row8 < shift, edge, rolled[0:8])
        return jnp.concatenate([fixed, rolled[8:]], axis=0)
    k = -shift
    rolled = pltpu.roll(tile, t - k, axis=0)
    edge = pltpu.roll(halo, HALO - k, axis=0)[HALO - 8:HALO]
    fixed = jnp.where(row8 >= 8 - k, edge, rolled[t - 8:t])
    return jnp.concatenate([rolled[:t - 8], fixed], axis=0)


def _inproj_kernel(x_ref, xp_ref, xn_ref, mod_ref, g_ref, w_ref, rcos_ref, rsin_ref, ccos_ref, csin_ref, qg_ref,
                   kg_ref, caw_ref, cab_ref, ccw_ref, o_ref, qt_ref, k_ref, vt_ref, *, nt):
    i = pl.program_id(1)
    tm = x_ref.shape[1]
    xv = jnp.concatenate([xp_ref[0], x_ref[0], xn_ref[0]], axis=0)
    ms = jnp.mean(xv * xv, axis=-1, keepdims=True)
    y = xv * lax.rsqrt(ms + EPS) * g_ref[...]
    shift = mod_ref[0, 0:1, :]
    scale = mod_ref[0, 1:2, :]
    h_ext = (y * (1.0 + scale) + shift).astype(BF16)
    h = h_ext[HALO:HALO + tm]

    def with_halos(acc_ext):
        prev = jnp.where(i == 0, 0.0, acc_ext[0:HALO])
        nxt = jnp.where(i == nt - 1, 0.0, acc_ext[HALO + tm:2 * HALO + tm])
        return acc_ext[HALO:HALO + tm], prev, nxt

    reps = GRID_W // 8
    cos_t = jnp.concatenate([jnp.tile(rcos_ref[r], (reps, 1)) + ccos_ref[...] for r in range(tm // GRID_W)], 0)
    sin_t = jnp.concatenate([jnp.tile(rsin_ref[r], (reps, 1)) + csin_ref[...] for r in range(tm // GRID_W)], 0)
    lane = lax.broadcasted_iota(jnp.int32, cos_t.shape, 1)
    lane_lo = (lane % 64) < 32
    q_scale = (HEAD_DIM ** -0.5) * LOG2E

    def project(j):
        lhs = h_ext if INPROJ_TILES[j][0] in ("xa", "xc", "cc") else h
        return jnp.dot(lhs, w_ref[:, j * INPROJ_TN:(j + 1) * INPROJ_TN], preferred_element_type=F32)

    held = {}
    acc_next = project(0)
    for j, (kind, _, half) in enumerate(INPROJ_TILES):
        lanes = slice(half * INPROJ_TN, (half + 1) * INPROJ_TN)
        acc = acc_next
        if j + 1 < len(INPROJ_TILES):
            acc_next = project(j + 1)

        def store(col, val):
            o_ref[0, :, col * D_MODEL + half * INPROJ_TN:col * D_MODEL + (half + 1) * INPROJ_TN] = val.astype(BF16)

        if kind == "q":
            for hh in range(INPROJ_TN // HEAD_DIM):
                qh = _norm_rope(acc[:, hh * HEAD_DIM:(hh + 1) * HEAD_DIM], qg_ref[...], cos_t, sin_t, lane_lo)
                qt_ref[0, half * (INPROJ_TN // HEAD_DIM) + hh] = (qh * q_scale).T.astype(BF16)
        elif kind == "kv":
            for hh in range(N_KV):
                kh = acc[:, hh * HEAD_DIM:(hh + 1) * HEAD_DIM]
                k_ref[0, hh, 0] = _norm_rope(kh, kg_ref[...], cos_t, sin_t, lane_lo).astype(BF16)
                vt_ref[0, hh, 0] = acc[:, KV_W + hh * HEAD_DIM:KV_W + (hh + 1) * HEAD_DIM].T.astype(BF16)
        elif kind == "xa":
            tile, prev, nxt = with_halos(acc)
            cw = caw_ref[:, lanes]
            store(COL_UA, cw[0:1] * _shift_rows(tile, prev, 2) + cw[1:2] * _shift_rows(tile, prev, 1)
                  + cw[2:3] * tile + cw[3:4] * _shift_rows(tile, nxt, -1) + cab_ref[:, lanes])
        elif kind == "xc":
            held["xc"] = acc
        elif kind == "cc":
            tile, prev, nxt = with_halos(acc * held.pop("xc"))
            cw = ccw_ref[:, lanes]
            held["conv"] = cw[0:1] * _shift_rows(tile, prev, 1) + cw[1:2] * tile + cw[2:3] * _shift_rows(tile, nxt, -1)
        elif kind == "bc":
            held["conv"] = acc * held["conv"]
        elif kind == "gc":
            store(COL_YC, held.pop("conv") * _silu(acc))
        elif kind == "ga":
            store(COL_GA, _silu(acc))
        elif kind == "gb":
            store(COL_GB, _silu(acc))
        else:
            o_ref[0, :, COL_MG * D_MODEL + half * INPROJ_TN:COL_MG * D_MODEL + (half + 1) * INPROJ_TN] = (
                _sigmoid(acc).astype(BF16))


def _in_projection(x, mod, norm_g, w_in_p, tables, q_g, k_g, conv_a_w, conv_a_b, conv_c_w, tk):
    bsz, s, d = x.shape
    tm = min(256, s)
    assert INPROJ_TN == 2 * KV_W and D_MODEL == 2 * INPROJ_TN and tm % GRID_W == 0 and tk % tm == 0
    nt = s // tm
    nkb = s // tk
    sub = tk // tm
    hbk = tm // HALO
    rcos, rsin, ccos, csin = tables
    row_spec = pl.BlockSpec((tm // GRID_W, 8, HEAD_DIM), lambda b, i: (i, 0, 0))
    col_spec = pl.BlockSpec((GRID_W, HEAD_DIM), lambda b, i: (0, 0))
    gain_spec = pl.BlockSpec((1, HEAD_DIM), lambda b, i: (0, 0))
    const2 = lambda b, i: (0, 0)
    return pl.pallas_call(
        functools.partial(_inproj_kernel, nt=nt),
        grid=(bsz, nt),
        in_specs=[
            pl.BlockSpec((1, tm, d), lambda b, i: (b, i, 0)),
            pl.BlockSpec((1, HALO, d), lambda b, i: (b, jnp.maximum(i * hbk - 1, 0), 0)),
            pl.BlockSpec((1, HALO, d), lambda b, i: (b, jnp.minimum((i + 1) * hbk, s // HALO - 1), 0)),
            pl.BlockSpec((1, 3, d), lambda b, i: (b, 0, 0)),
            pl.BlockSpec((1, d), const2),
            pl.BlockSpec((d, N_IN), const2, pipeline_mode=pl.Buffered(1)),
            row_spec, row_spec, col_spec, col_spec, gain_spec, gain_spec,
            pl.BlockSpec(conv_a_w.shape, const2),
            pl.BlockSpec((1, d), const2),
            pl.BlockSpec(conv_c_w.shape, const2),
        ],
        out_specs=[
            pl.BlockSpec((1, tm, N_STORED), lambda b, i: (b, i, 0)),
            pl.BlockSpec((1, N_HEADS, HEAD_DIM, tm), lambda b, i: (b, 0, 0, i)),
            pl.BlockSpec((1, N_KV, 1, tm, HEAD_DIM), lambda b, i: (b, 0, i // sub, i % sub, 0)),
            pl.BlockSpec((1, N_KV, 1, HEAD_DIM, tm), lambda b, i: (b, 0, i // sub, 0, i % sub)),
        ],
        out_shape=[
            jax.ShapeDtypeStruct((bsz, s, N_STORED), BF16),
            jax.ShapeDtypeStruct((bsz, N_HEADS, HEAD_DIM, s), BF16),
            jax.ShapeDtypeStruct((bsz, N_KV, nkb, tk, HEAD_DIM), BF16),
            jax.ShapeDtypeStruct((bsz, N_KV, nkb, HEAD_DIM, tk), BF16),
        ],
        compiler_params=_cparams(("parallel", "parallel")),
        name="norm_inproj",
    )(x, x, x, mod, norm_g.reshape(1, d), w_in_p, rcos, rsin, ccos, csin, q_g.reshape(1, HEAD_DIM),
      k_g.reshape(1, HEAD_DIM), conv_a_w, conv_a_b.reshape(1, d), conv_c_w)


SCORE_BOUND_NOSHIFT = 40.0


def _score_bound(q_g, k_g):
    q_scale = (HEAD_DIM ** -0.5) * LOG2E
    return HEAD_DIM * q_scale * jnp.max(jnp.abs(q_g)) * jnp.max(jnp.abs(k_g))


KV_BLOCKS_PER_TRIP = 8


def _largest_divisor(n, cap):
    return max(d for d in range(1, cap + 1) if n % d == 0)


def _attn_kernel(flag_ref, qt_ref, k_ref, vt_ref, gb_ref, o_ref, m_ref, l_ref, acc_ref, s_ref, *, nkb):
    tq = acc_ref.shape[-1]
    l_ref[...] = jnp.zeros(l_ref.shape, F32)
    acc_ref[...] = jnp.zeros(acc_ref.shape, F32)
    bounded = flag_ref[0] != 0

    @pl.when(bounded)
    def _():
        def scores(j, h):
            return jnp.dot(k_ref[0, 0, j], qt_ref[0, h], preferred_element_type=F32)

        s_ref[...] = scores(0, 0)

        group = _largest_divisor(nkb, KV_BLOCKS_PER_TRIP)

        def kv_group(g, carry):
            s_cur = s_ref[...]
            for u in range(group):
                j = g * group + u
                vblk = vt_ref[0, 0, j]
                for h in range(GROUP):
                    s_next = scores(j, h + 1) if h + 1 < GROUP else scores(jnp.minimum(j + 1, nkb - 1), 0)
                    p = jnp.exp2(s_cur)
                    l_ref[h] += jnp.sum(p.reshape(-1, 8, tq), axis=0)
                    acc_ref[h] += jnp.dot(vblk, p.astype(BF16), preferred_element_type=F32)
                    s_cur = s_next
            s_ref[...] = s_cur
            return carry

        lax.fori_loop(0, nkb // group, kv_group, 0)

    @pl.when(jnp.logical_not(bounded))
    def _():
        m_ref[...] = jnp.full(m_ref.shape, -jnp.inf, F32)

        def kv_step(j, carry):
            kblk = k_ref[0, 0, j]
            vblk = vt_ref[0, 0, j]
            for h in range(GROUP):
                s = jnp.dot(kblk, qt_ref[0, h], preferred_element_type=F32)
                m_old = m_ref[h]
                m_new = jnp.maximum(m_old, jnp.max(s, axis=0, keepdims=True))
                alpha = jnp.exp2(m_old - m_new)
                p = jnp.exp2(s - m_new)
                l_ref[h] = alpha * l_ref[h] + jnp.sum(p.reshape(-1, 8, tq), axis=0)
                acc_ref[h] = alpha * acc_ref[h] + jnp.dot(vblk, p.astype(BF16), preferred_element_type=F32)
                m_ref[h] = m_new
            return carry

        lax.fori_loop(0, nkb, kv_step, 0)

    for h in range(GROUP):
        l = jnp.sum(l_ref[h], axis=0, keepdims=True)
        o = acc_ref[h] * (1.0 / l)
        gate = gb_ref[0, :, h * HEAD_DIM:(h + 1) * HEAD_DIM].astype(F32)
        o_ref[0, :, h * HEAD_DIM:(h + 1) * HEAD_DIM] = (o.T * gate).astype(BF16)


def _attention(qt, kk, vt, proj, bounded_flag, tq):
    bsz, _, _, s = qt.shape
    nkb, tk = kk.shape[2], kk.shape[3]
    gw = GROUP * HEAD_DIM
    grid_spec = pltpu.PrefetchScalarGridSpec(
        num_scalar_prefetch=1,
        grid=(bsz, N_KV, s // tq),
        in_specs=[
            pl.BlockSpec((1, GROUP, HEAD_DIM, tq), lambda b, g, i, f: (b, g, 0, i)),
            pl.BlockSpec((1, 1, nkb, tk, HEAD_DIM), lambda b, g, i, f: (b, g, 0, 0, 0)),
            pl.BlockSpec((1, 1, nkb, HEAD_DIM, tk), lambda b, g, i, f: (b, g, 0, 0, 0)),
            pl.BlockSpec((1, tq, gw), lambda b, g, i, f: (b, i, COL_GB * D_MODEL // gw + g)),
        ],
        out_specs=pl.BlockSpec((1, tq, gw), lambda b, g, i, f: (b, i, g)),
        scratch_shapes=[
            pltpu.VMEM((GROUP, 1, tq), F32),
            pltpu.VMEM((GROUP, 8, tq), F32),
            pltpu.VMEM((GROUP, HEAD_DIM, tq), F32),
            pltpu.VMEM((tk, tq), F32),
        ],
    )
    return pl.pallas_call(
        functools.partial(_attn_kernel, nkb=nkb),
        grid_spec=grid_spec,
        out_shape=jax.ShapeDtypeStruct((bsz, s, D_MODEL), BF16),
        compiler_params=_cparams(("parallel", "parallel", "arbitrary")),
        name="gqa_attention",
    )(bounded_flag, qt, kk, vt, proj)


def _lru_gates(x_ref, wr_ref, wi_ref, d, ua_ref, rpre_ref, ipre_ref):
    ub = x_ref[0]
    ua_ref[d] = ub.astype(F32)
    for n in range(RNN_BLOCKS):
        cols = slice(n * RNN_BW, (n + 1) * RNN_BW)
        rpre_ref[d, :, cols] = jnp.dot(ub[:, cols], wr_ref[d, n], preferred_element_type=F32)
        ipre_ref[d, :, cols] = jnp.dot(ub[:, cols], wi_ref[d, n], preferred_element_type=F32)


def _lru_coefficients(ua, r_half, i_half, b_r_half, b_i_half, c_half):
    y = c_half + c_half * jnp.tanh(r_half + b_r_half)
    ua_half = 0.5 * ua
    gated = ua_half + ua_half * jnp.tanh(i_half + b_i_half)
    a = jnp.exp2(y * (-LOG2E))
    x = jnp.tanh(y) * (1.0 + a * a)
    root = jnp.where(x > 0.0, x * lax.rsqrt(x), x)
    return a, root * gated


LANES = 128
N_SLABS = D_MODEL // LANES


def _slab_pitch(t):
    return t + 8 if (t // 8) % 2 == 0 else t


def _lru_kernel(xf_ref, xb_ref, wr_ref, br_ref, wi_ref, bi_ref, lam_ref, hf_ref, hb_ref, cf_ref, cbk_ref,
                sf_ref, sb_ref, ua_ref, rpre_ref, ipre_ref, chan_ref, *coef_refs, nt):
    i = pl.program_id(1)
    tm = hf_ref.shape[1]
    pitch = _slab_pitch(tm)
    coef_sets = (coef_refs[:4], coef_refs[4:])

    @pl.when(i == 0)
    def _():
        cf_ref[...] = jnp.zeros(cf_ref.shape, F32)
        cbk_ref[...] = jnp.zeros(cbk_ref.shape, F32)
        for ref in coef_sets[1]:
            ref[...] = jnp.zeros(ref.shape, F32)

    _lru_gates(xf_ref, wr_ref, wi_ref, 0, ua_ref, rpre_ref, ipre_ref)
    _lru_gates(xb_ref, wr_ref, wi_ref, 1, ua_ref, rpre_ref, ipre_ref)
    nl = -lam_ref[...]
    c_half = (0.5 * LRU_C) * (jnp.maximum(nl, 0.0) + jnp.log1p(jnp.exp(-jnp.abs(nl))))
    for d in range(2):
        for n, row in enumerate((0.5 * br_ref[d:d + 1, :], 0.5 * bi_ref[d:d + 1, :], c_half[d:d + 1, :])):
            chan_ref[3 * d + n] = jnp.broadcast_to(row, chan_ref.shape[1:])

    def step(cur, prev):
        af_ref, uf_ref, ab_ref, ub_ref = prev

        def rows8(g, carry):
            hf, hb = carry
            r0 = pl.multiple_of(g * 8, 8)
            rows = pl.ds(r0, 8)
            for d in range(2):
                a, u = _lru_coefficients(ua_ref[d, rows, :], rpre_ref[d, rows, :], ipre_ref[d, rows, :],
                                         chan_ref[3 * d], chan_ref[3 * d + 1], chan_ref[3 * d + 2])
                for c in range(N_SLABS):
                    dst = pl.ds(c * pitch + r0, 8)
                    cur[2 * d][dst, :] = a[:, c * LANES:(c + 1) * LANES]
                    cur[2 * d + 1][dst, :] = u[:, c * LANES:(c + 1) * LANES]
            for j in range(8):
                k = r0 + j
                tf = pl.ds(k, N_SLABS, stride=pitch)
                tb = pl.ds(tm - 1 - k, N_SLABS, stride=pitch)
                hf = af_ref[tf, :] * hf + uf_ref[tf, :]
                sf_ref[tf, :] = hf
                hb = ab_ref[tb, :] * hb + ub_ref[tb, :]
                sb_ref[tb, :] = hb
            return hf, hb

        hf, hb = lax.fori_loop(0, tm // 8, rows8, (cf_ref[...], cbk_ref[...]), unroll=4)
        keep = i > 0
        cf_ref[...] = jnp.where(keep, hf, cf_ref[...])
        cbk_ref[...] = jnp.where(keep, hb, cbk_ref[...])
        for s_ref, out_ref in ((sf_ref, hf_ref), (sb_ref, hb_ref)):
            for c in range(N_SLABS):
                out_ref[0, :, c * LANES:(c + 1) * LANES] = s_ref[pl.ds(c * pitch, tm), :].astype(out_ref.dtype)

    @pl.when(i % 2 == 0)
    def _():
        step(coef_sets[0], coef_sets[1])

    @pl.when(i % 2 == 1)
    def _():
        step(coef_sets[1], coef_sets[0])


def _rg_lru(proj, w_rg, b_rg, w_ig, b_ig, lam):
    bsz, s, _ = proj.shape
    d = D_MODEL
    tm = min(256, s)
    nt = s // tm
    const2 = lambda b, i: (0, 0)
    const4 = lambda b, i: (0, 0, 0, 0)
    out_sds = jax.ShapeDtypeStruct((bsz, s, d), BF16)
    return pl.pallas_call(
        functools.partial(_lru_kernel, nt=nt),
        grid=(bsz, nt + 1),
        in_specs=[
            pl.BlockSpec((1, tm, d), lambda b, i: (b, jnp.minimum(i, nt - 1), COL_UA)),
            pl.BlockSpec((1, tm, d), lambda b, i: (b, nt - 1 - jnp.minimum(i, nt - 1), COL_UA)),
            pl.BlockSpec(w_rg.shape, const4),
            pl.BlockSpec(b_rg.shape, const2),
            pl.BlockSpec(w_ig.shape, const4),
            pl.BlockSpec(b_ig.shape, const2),
            pl.BlockSpec(lam.shape, const2),
        ],
        out_specs=[
            pl.BlockSpec((1, tm, d), lambda b, i: (b, jnp.maximum(i - 1, 0), 0)),
            pl.BlockSpec((1, tm, d), lambda b, i: (b, nt - 1 - jnp.maximum(i - 1, 0), 0)),
        ],
        out_shape=[out_sds, out_sds],
        scratch_shapes=[pltpu.VMEM((N_SLABS, LANES), F32)] * 2
        + [pltpu.VMEM((N_SLABS * _slab_pitch(tm), LANES), F32)] * 2
        + [pltpu.VMEM((2, tm, d), F32)] * 3
        + [pltpu.VMEM((6, 8, d), F32)]
        + [pltpu.VMEM((N_SLABS * _slab_pitch(tm), LANES), F32)] * 8,
        compiler_params=_cparams(("parallel", "arbitrary")),
        name="rg_lru",
    )(proj, proj, w_rg, b_rg, w_ig, b_ig, lam)


def _merge_kernel(hf_ref, hb_ref, ga_ref, yb_ref, yc_ref, mg0_ref, mg1_ref, mg2_ref, x_ref, mod_ref, wb_ref,
                  wo_ref, o_ref):
    merged = mg1_ref[0].astype(F32) * jnp.dot(yb_ref[0], wb_ref[1], preferred_element_type=F32)
    merged += mg2_ref[0].astype(F32) * jnp.dot(yc_ref[0], wb_ref[2], preferred_element_type=F32)
    ya = (hf_ref[0].astype(F32) + hb_ref[0].astype(F32)) * ga_ref[0].astype(F32)
    merged += mg0_ref[0].astype(F32) * jnp.dot(ya.astype(BF16), wb_ref[0], preferred_element_type=F32)
    out = jnp.dot(merged.astype(BF16), wo_ref[...], preferred_element_type=F32)
    o_ref[0] = x_ref[0] + mod_ref[0, 2:3, :] * out


def _merge(hf, hb, yb, proj, x, mod, w_branch, w_out):
    bsz, s, d = x.shape
    tm = min(512, s)

    def col(c):
        return pl.BlockSpec((1, tm, d), lambda b, i: (b, i, c))

    return pl.pallas_call(
        _merge_kernel,
        grid=(bsz, s // tm),
        in_specs=[
            col(0), col(0), col(COL_GA), col(0), col(COL_YC), col(COL_MG), col(COL_MG + 1), col(COL_MG + 2),
            col(0),
            pl.BlockSpec((1, 3, d), lambda b, i: (b, 0, 0)),
            pl.BlockSpec(w_branch.shape, lambda b, i: (0, 0, 0)),
            pl.BlockSpec(w_out.shape, lambda b, i: (0, 0)),
        ],
        out_specs=pl.BlockSpec((1, tm, d), lambda b, i: (b, i, 0)),
        out_shape=jax.ShapeDtypeStruct((bsz, s, d), F32),
        compiler_params=_cparams(("parallel", "parallel")),
        name="merge_outproj",
    )(hf, hb, proj, yb, proj, proj, proj, proj, x, mod, w_branch, w_out)


def _permute_w_in(w):
    return jnp.concatenate([w[:, off:off + INPROJ_TN] for _, off, _ in INPROJ_TILES], axis=1)


def _attn_tiles(s):
    tq = min(512, s)
    tk = min(512, s)
    return tq, tk


def kernel(x, c, w_ada, b_ada, norm_g, w_in, conv_a_w, conv_a_b, w_rg, b_rg, w_ig, b_ig, lru_lam, q_norm_g,
           k_norm_g, conv_c_w, w_branch, w_out):
    bsz, s, d = x.shape
    depth = w_ada.shape[0]
    tq, tk = _attn_tiles(s)
    tables = _rope_tables(s)
    mods = _modulation(c, w_ada, b_ada)
    for l in range(depth):
        mod = mods[l]
        proj, qt, kk, vt = _in_projection(x, mod, norm_g[l], _permute_w_in(w_in[l]).astype(BF16), tables,
                                          q_norm_g[l], k_norm_g[l], conv_a_w[l], conv_a_b[l], conv_c_w[l], tk)
        hf, hb = _rg_lru(proj, (0.5 * w_rg[l]).astype(BF16), b_rg[l], (0.5 * w_ig[l]).astype(BF16), b_ig[l],
                         lru_lam[l])
        bounded = (_score_bound(q_norm_g[l], k_norm_g[l]) <= SCORE_BOUND_NOSHIFT).astype(jnp.int32).reshape(1)
        yb = _attention(qt, kk, vt, proj, bounded, tq)
        x = _merge(hf, hb, yb, proj, x, mod, w_branch[l].astype(BF16), w_out[l].astype(BF16))
    return x
```

```python
import functools
import math

import numpy as np
import jax
import jax.numpy as jnp
from jax import lax
from jax.experimental import pallas as pl
from jax.experimental.pallas import tpu as pltpu

D_MODEL = 1024
EPS = 1e-6
GRID_W = 64
N_HEADS = 8
N_KV = 2
GROUP = N_HEADS // N_KV
HEAD_DIM = 128
RNN_BLOCKS = 4
RNN_BW = D_MODEL // RNN_BLOCKS
LRU_C = 8.0
ROPE_THETA = 10000.0
N_BRANCH = 3
LOG2E = 1.4426950408889634

KV_W = N_KV * HEAD_DIM
QKV_W = D_MODEL + 2 * KV_W
N_IN = 11 * D_MODEL + 2 * KV_W
INPROJ_TN = 512

COL_UA, COL_GA, COL_GB, COL_YC, COL_MG = 0, 1, 2, 3, 4
N_STORED = (COL_MG + N_BRANCH) * D_MODEL

_SRC = dict(xa=0, ga=D_MODEL, q=2 * D_MODEL, kv=3 * D_MODEL, gb=3 * D_MODEL + 2 * KV_W)
_SRC.update(xc=_SRC["gb"] + D_MODEL, bc=_SRC["gb"] + 2 * D_MODEL, cc=_SRC["gb"] + 3 * D_MODEL,
            gc=_SRC["gb"] + 4 * D_MODEL, mg=_SRC["gb"] + 5 * D_MODEL)
_TILE_ORDER = (("q", 0), ("mg", 0), ("q", 1), ("mg", 1), ("kv", 0), ("mg", 2), ("xa", 0), ("mg", 3),
               ("xa", 1), ("mg", 4), ("xc", 0), ("cc", 0), ("ga", 0), ("bc", 0), ("gc", 0), ("mg", 5),
               ("xc", 1), ("cc", 1), ("ga", 1), ("bc", 1), ("gc", 1), ("gb", 0), ("gb", 1))
INPROJ_TILES = tuple((k, _SRC[k] + t * INPROJ_TN, t) for k, t in _TILE_ORDER)
assert len(INPROJ_TILES) * INPROJ_TN == N_IN

HALO = 16
VMEM_LIMIT = 56 * 1024 * 1024

F32 = jnp.float32
BF16 = jnp.bfloat16


def _sigmoid(x):
    return 1.0 / (1.0 + jnp.exp2(x * (-LOG2E)))


def _silu(x):
    return x * _sigmoid(x)


def _cparams(sem):
    return pltpu.CompilerParams(dimension_semantics=sem, vmem_limit_bytes=VMEM_LIMIT)


def _mod_kernel(c_ref, w_ref, b_ref, o_ref):
    acc = jnp.dot(c_ref[...], w_ref[0], preferred_element_type=F32, precision=lax.Precision.HIGHEST)
    o_ref[0] = acc + b_ref[0]


def _modulation(c, w_ada, b_ada):
    depth, d, n = w_ada.shape
    bsz = c.shape[0]
    rows = 8
    c8 = jnp.zeros((rows, d), F32).at[:bsz].set(c)
    tn = 1024
    out = pl.pallas_call(
        _mod_kernel,
        grid=(depth, n // tn),
        in_specs=[
            pl.BlockSpec((rows, d), lambda l, j: (0, 0)),
            pl.BlockSpec((1, d, tn), lambda l, j: (l, 0, j)),
            pl.BlockSpec((1, 1, tn), lambda l, j: (l, 0, j)),
        ],
        out_specs=pl.BlockSpec((1, rows, tn), lambda l, j: (l, 0, j)),
        out_shape=jax.ShapeDtypeStruct((depth, rows, n), F32),
        compiler_params=_cparams(("parallel", "parallel")),
        name="adaln_mod",
    )(c8, w_ada, b_ada.reshape(depth, 1, n))
    return out[:, :bsz].reshape(depth, bsz, 3, d)


def _rope_tables(s):
    half = HEAD_DIM // 2
    inv = ROPE_THETA ** (-np.arange(0, half, 2, dtype=np.float64) / half)
    rows = s // GRID_W
    ang_r = np.arange(rows, dtype=np.float64)[:, None] * inv
    ang_c = np.arange(GRID_W, dtype=np.float64)[:, None] * inv
    zr = np.zeros((rows, half))
    zc = np.zeros((GRID_W, half))
    row_cos = np.concatenate([np.cos(ang_r), np.cos(ang_r), zr], -1)
    row_sin = np.concatenate([-np.sin(ang_r), np.sin(ang_r), zr], -1)
    col_cos = np.concatenate([zc, np.cos(ang_c), np.cos(ang_c)], -1)
    col_sin = np.concatenate([zc, -np.sin(ang_c), np.sin(ang_c)], -1)
    bcast = lambda t: jnp.broadcast_to(jnp.asarray(t, F32)[:, None, :], (rows, 8, HEAD_DIM))
    return bcast(row_cos), bcast(row_sin), jnp.asarray(col_cos, F32), jnp.asarray(col_sin, F32)


def _norm_rope(xh, g, cos_t, sin_t, lane_lo):
    ms = jnp.mean(xh * xh, axis=-1, keepdims=True)
    y = xh * lax.rsqrt(ms + EPS) * g
    partner = jnp.where(lane_lo, pltpu.roll(y, HEAD_DIM - 32, axis=1), pltpu.roll(y, 32, axis=1))
    return y * cos_t + partner * sin_t


def _shift_rows(tile, halo, shift):
    t = tile.shape[0]
    row8 = lax.broadcasted_iota(jnp.int32, (8, tile.shape[1]), 0)
    if shift > 0:
        rolled = pltpu.roll(tile, shift, axis=0)
        edge = pltpu.roll(halo, shift, axis=0)[0:8]
        fixed = jnp.where(row8 < shift, edge, rolled[0:8])
        return jnp.concatenate([fixed, rolled[8:]], axis=0)
    k = -shift
    rolled = pltpu.roll(tile, t - k, axis=0)
    edge = pltpu.roll(halo, HALO - k, axis=0)[HALO - 8:HALO]
    fixed = jnp.where(row8 >= 8 - k, edge, rolled[t - 8:t])
    return jnp.concatenate([rolled[:t - 8], fixed], axis=0)


def _inproj_kernel(x_ref, xp_ref, xn_ref, mod_ref, g_ref, w_ref, rcos_ref, rsin_ref, ccos_ref, csin_ref, qg_ref,
                   kg_ref, caw_ref, cab_ref, ccw_ref, o_ref, qt_ref, k_ref, vt_ref, *, nt):
    i = pl.program_id(1)
    tm = x_ref.shape[1]
    xv = jnp.concatenate([xp_ref[0], x_ref[0], xn_ref[0]], axis=0)
    ms = jnp.mean(xv * xv, axis=-1, keepdims=True)
    y = xv * lax.rsqrt(ms + EPS) * g_ref[...]
    shift = mod_ref[0, 0:1, :]
    scale = mod_ref[0, 1:2, :]
    h_ext = (y * (1.0 + scale) + shift).astype(BF16)
    h = h_ext[HALO:HALO + tm]

    def with_halos(acc_ext):
        prev = jnp.where(i == 0, 0.0, acc_ext[0:HALO])
        nxt = jnp.where(i == nt - 1, 0.0, acc_ext[HALO + tm:2 * HALO + tm])
        return acc_ext[HALO:HALO + tm], prev, nxt

    reps = GRID_W // 8
    cos_t = jnp.concatenate([jnp.tile(rcos_ref[r], (reps, 1)) + ccos_ref[...] for r in range(tm // GRID_W)], 0)
    sin_t = jnp.concatenate([jnp.tile(rsin_ref[r], (reps, 1)) + csin_ref[...] for r in range(tm // GRID_W)], 0)
    lane = lax.broadcasted_iota(jnp.int32, cos_t.shape, 1)
    lane_lo = (lane % 64) < 32
    q_scale = (HEAD_DIM ** -0.5) * LOG2E

    def project(j):
        lhs = h_ext if INPROJ_TILES[j][0] in ("xa", "xc", "cc") else h
        return jnp.dot(lhs, w_ref[:, j * INPROJ_TN:(j + 1) * INPROJ_TN], preferred_element_type=F32)

    held = {}
    acc_next = project(0)
    for j, (kind, _, half) in enumerate(INPROJ_TILES):
        lanes = slice(half * INPROJ_TN, (half + 1) * INPROJ_TN)
        acc = acc_next
        if j + 1 < len(INPROJ_TILES):
            acc_next = project(j + 1)

        def store(col, val):
            o_ref[0, :, col * D_MODEL + half * INPROJ_TN:col * D_MODEL + (half + 1) * INPROJ_TN] = val.astype(BF16)

        if kind == "q":
            for hh in range(INPROJ_TN // HEAD_DIM):
                qh = _norm_rope(acc[:, hh * HEAD_DIM:(hh + 1) * HEAD_DIM], qg_ref[...], cos_t, sin_t, lane_lo)
                qt_ref[0, half * (INPROJ_TN // HEAD_DIM) + hh] = (qh * q_scale).T.astype(BF16)
        elif kind == "kv":
            for hh in range(N_KV):
                kh = acc[:, hh * HEAD_DIM:(hh + 1) * HEAD_DIM]
                k_ref[0, hh, 0] = _norm_rope(kh, kg_ref[...], cos_t, sin_t, lane_lo).astype(BF16)
                vt_ref[0, hh, 0] = acc[:, KV_W + hh * HEAD_DIM:KV_W + (hh + 1) * HEAD_DIM].T.astype(BF16)
        elif kind == "xa":
            tile, prev, nxt = with_halos(acc)
            cw = caw_ref[:, lanes]
            store(COL_UA, cw[0:1] * _shift_rows(tile, prev, 2) + cw[1:2] * _shift_rows(tile, prev, 1)
                  + cw[2:3] * tile + cw[3:4] * _shift_rows(tile, nxt, -1) + cab_ref[:, lanes])
        elif kind == "xc":
            held["xc"] = acc
        elif kind == "cc":
            tile, prev, nxt = with_halos(acc * held.pop("xc"))
            cw = ccw_ref[:, lanes]
            held["conv"] = cw[0:1] * _shift_rows(tile, prev, 1) + cw[1:2] * tile + cw[2:3] * _shift_rows(tile, nxt, -1)
        elif kind == "bc":
            held["conv"] = acc * held["conv"]
        elif kind == "gc":
            store(COL_YC, held.pop("conv") * _silu(acc))
        elif kind == "ga":
            store(COL_GA, _silu(acc))
        elif kind == "gb":
            store(COL_GB, _silu(acc))
        else:
            o_ref[0, :, COL_MG * D_MODEL + half * INPROJ_TN:COL_MG * D_MODEL + (half + 1) * INPROJ_TN] = (
                _sigmoid(acc).astype(BF16))


def _in_projection(x, mod, norm_g, w_in_p, tables, q_g, k_g, conv_a_w, conv_a_b, conv_c_w, tk):
    bsz, s, d = x.shape
    tm = min(256, s)
    assert INPROJ_TN == 2 * KV_W and D_MODEL == 2 * INPROJ_TN and tm % GRID_W == 0 and tk % tm == 0
    nt = s // tm
    nkb = s // tk
    sub = tk // tm
    hbk = tm // HALO
    rcos, rsin, ccos, csin = tables
    row_spec = pl.BlockSpec((tm // GRID_W, 8, HEAD_DIM), lambda b, i: (i, 0, 0))
    col_spec = pl.BlockSpec((GRID_W, HEAD_DIM), lambda b, i: (0, 0))
    gain_spec = pl.BlockSpec((1, HEAD_DIM), lambda b, i: (0, 0))
    const2 = lambda b, i: (0, 0)
    return pl.pallas_call(
        functools.partial(_inproj_kernel, nt=nt),
        grid=(bsz, nt),
        in_specs=[
            pl.BlockSpec((1, tm, d), lambda b, i: (b, i, 0)),
            pl.BlockSpec((1, HALO, d), lambda b, i: (b, jnp.maximum(i * hbk - 1, 0), 0)),
            pl.BlockSpec((1, HALO, d), lambda b, i: (b, jnp.minimum((i + 1) * hbk, s // HALO - 1), 0)),
            pl.BlockSpec((1, 3, d), lambda b, i: (b, 0, 0)),
            pl.BlockSpec((1, d), const2),
            pl.BlockSpec((d, N_IN), const2, pipeline_mode=pl.Buffered(1)),
            row_spec, row_spec, col_spec, col_spec, gain_spec, gain_spec,
            pl.BlockSpec(conv_a_w.shape, const2),
            pl.BlockSpec((1, d), const2),
            pl.BlockSpec(conv_c_w.shape, const2),
        ],
        out_specs=[
            pl.BlockSpec((1, tm, N_STORED), lambda b, i: (b, i, 0)),
            pl.BlockSpec((1, N_HEADS, HEAD_DIM, tm), lambda b, i: (b, 0, 0, i)),
            pl.BlockSpec((1, N_KV, 1, tm, HEAD_DIM), lambda b, i: (b, 0, i // sub, i % sub, 0)),
            pl.BlockSpec((1, N_KV, 1, HEAD_DIM, tm), lambda b, i: (b, 0, i // sub, 0, i % sub)),
        ],
        out_shape=[
            jax.ShapeDtypeStruct((bsz, s, N_STORED), BF16),
            jax.ShapeDtypeStruct((bsz, N_HEADS, HEAD_DIM, s), BF16),
            jax.ShapeDtypeStruct((bsz, N_KV, nkb, tk, HEAD_DIM), BF16),
            jax.ShapeDtypeStruct((bsz, N_KV, nkb, HEAD_DIM, tk), BF16),
        ],
        compiler_params=_cparams(("parallel", "parallel")),
        name="norm_inproj",
    )(x, x, x, mod, norm_g.reshape(1, d), w_in_p, rcos, rsin, ccos, csin, q_g.reshape(1, HEAD_DIM),
      k_g.reshape(1, HEAD_DIM), conv_a_w, conv_a_b.reshape(1, d), conv_c_w)


SCORE_BOUND_NOSHIFT = 40.0


def _score_bound(q_g, k_g):
    q_scale = (HEAD_DIM ** -0.5) * LOG2E
    return HEAD_DIM * q_scale * jnp.max(jnp.abs(q_g)) * jnp.max(jnp.abs(k_g))


KV_BLOCKS_PER_TRIP = 16


def _largest_divisor(n, cap):
    return max(d for d in range(1, cap + 1) if n % d == 0)


def _attn_kernel(flag_ref, qt_ref, k_ref, vt_ref, gb_ref, o_ref, m_ref, l_ref, acc_ref, s_ref, *, nkb):
    tq = acc_ref.shape[-1]
    l_ref[...] = jnp.zeros(l_ref.shape, F32)
    acc_ref[...] = jnp.zeros(acc_ref.shape, F32)
    bounded = flag_ref[0] != 0

    @pl.when(bounded)
    def _():
        def scores(j, h):
            return jnp.dot(k_ref[0, 0, j], qt_ref[0, h], preferred_element_type=F32)

        s_ref[...] = scores(0, 0)

        group = _largest_divisor(nkb, KV_BLOCKS_PER_TRIP)

        def kv_group(g, carry):
            s_cur = s_ref[...]
            for u in range(group):
                j = g * group + u
                vblk = vt_ref[0, 0, j]
                for h in range(GROUP):
                    s_next = scores(j, h + 1) if h + 1 < GROUP else scores(jnp.minimum(j + 1, nkb - 1), 0)
                    p = jnp.exp2(s_cur)
                    l_ref[h] += jnp.sum(p.reshape(-1, 8, tq), axis=0)
                    acc_ref[h] += jnp.dot(vblk, p.astype(BF16), preferred_element_type=F32)
                    s_cur = s_next
            s_ref[...] = s_cur
            return carry

        lax.fori_loop(0, nkb // group, kv_group, 0)

    @pl.when(jnp.logical_not(bounded))
    def _():
        m_ref[...] = jnp.full(m_ref.shape, -jnp.inf, F32)

        def kv_step(j, carry):
            kblk = k_ref[0, 0, j]
            vblk = vt_ref[0, 0, j]
            for h in range(GROUP):
                s = jnp.dot(kblk, qt_ref[0, h], preferred_element_type=F32)
                m_old = m_ref[h]
                m_new = jnp.maximum(m_old, jnp.max(s, axis=0, keepdims=True))
                alpha = jnp.exp2(m_old - m_new)
                p = jnp.exp2(s - m_new)
                l_ref[h] = alpha * l_ref[h] + jnp.sum(p.reshape(-1, 8, tq), axis=0)
                acc_ref[h] = alpha * acc_ref[h] + jnp.dot(vblk, p.astype(BF16), preferred_element_type=F32)
                m_ref[h] = m_new
            return carry

        lax.fori_loop(0, nkb, kv_step, 0)

    for h in range(GROUP):
        l = jnp.sum(l_ref[h], axis=0, keepdims=True)
        o = acc_ref[h] * (1.0 / l)
        gate = gb_ref[0, :, h * HEAD_DIM:(h + 1) * HEAD_DIM].astype(F32)
        o_ref[0, :, h * HEAD_DIM:(h + 1) * HEAD_DIM] = (o.T * gate).astype(BF16)


def _attention(qt, kk, vt, proj, bounded_flag, tq):
    bsz, _, _, s = qt.shape
    nkb, tk = kk.shape[2], kk.shape[3]
    gw = GROUP * HEAD_DIM
    grid_spec = pltpu.PrefetchScalarGridSpec(
        num_scalar_prefetch=1,
        grid=(bsz, N_KV, s // tq),
        in_specs=[
            pl.BlockSpec((1, GROUP, HEAD_DIM, tq), lambda b, g, i, f: (b, g, 0, i)),
            pl.BlockSpec((1, 1, nkb, tk, HEAD_DIM), lambda b, g, i, f: (b, g, 0, 0, 0)),
            pl.BlockSpec((1, 1, nkb, HEAD_DIM, tk), lambda b, g, i, f: (b, g, 0, 0, 0)),
            pl.BlockSpec((1, tq, gw), lambda b, g, i, f: (b, i, COL_GB * D_MODEL // gw + g)),
        ],
        out_specs=pl.BlockSpec((1, tq, gw), lambda b, g, i, f: (b, i, g)),
        scratch_shapes=[
            pltpu.VMEM((GROUP, 1, tq), F32),
            pltpu.VMEM((GROUP, 8, tq), F32),
            pltpu.VMEM((GROUP, HEAD_DIM, tq), F32),
            pltpu.VMEM((tk, tq), F32),
        ],
    )
    return pl.pallas_call(
        functools.partial(_attn_kernel, nkb=nkb),
        grid_spec=grid_spec,
        out_shape=jax.ShapeDtypeStruct((bsz, s, D_MODEL), BF16),
        compiler_params=_cparams(("parallel", "parallel", "arbitrary")),
        name="gqa_attention",
    )(bounded_flag, qt, kk, vt, proj)


def _lru_gates(x_ref, wr_ref, wi_ref, d, ua_ref, rpre_ref, ipre_ref):
    ub = x_ref[0]
    ua_ref[d] = ub.astype(F32)
    for n in range(RNN_BLOCKS):
        cols = slice(n * RNN_BW, (n + 1) * RNN_BW)
        rpre_ref[d, :, cols] = jnp.dot(ub[:, cols], wr_ref[d, n], preferred_element_type=F32)
        ipre_ref[d, :, cols] = jnp.dot(ub[:, cols], wi_ref[d, n], preferred_element_type=F32)


def _lru_coefficients(ua, r_half, i_half, b_r_half, b_i_half, c_half):
    y = c_half + c_half * jnp.tanh(r_half + b_r_half)
    ua_half = 0.5 * ua
    gated = ua_half + ua_half * jnp.tanh(i_half + b_i_half)
    a = jnp.exp2(y * (-LOG2E))
    x = jnp.tanh(y) * (1.0 + a * a)
    root = jnp.where(x > 0.0, x * lax.rsqrt(x), x)
    return a, root * gated


LANES = 128
N_SLABS = D_MODEL // LANES


def _slab_pitch(t):
    return t + 8 if (t // 8) % 2 == 0 else t


def _lru_kernel(xf_ref, xb_ref, wr_ref, br_ref, wi_ref, bi_ref, lam_ref, hf_ref, hb_ref, cf_ref, cbk_ref,
                sf_ref, sb_ref, ua_ref, rpre_ref, ipre_ref, chan_ref, *coef_refs, nt):
    i = pl.program_id(1)
    tm = hf_ref.shape[1]
    pitch = _slab_pitch(tm)
    coef_sets = (coef_refs[:4], coef_refs[4:])

    @pl.when(i == 0)
    def _():
        cf_ref[...] = jnp.zeros(cf_ref.shape, F32)
        cbk_ref[...] = jnp.zeros(cbk_ref.shape, F32)
        for ref in coef_sets[1]:
            ref[...] = jnp.zeros(ref.shape, F32)

    _lru_gates(xf_ref, wr_ref, wi_ref, 0, ua_ref, rpre_ref, ipre_ref)
    _lru_gates(xb_ref, wr_ref, wi_ref, 1, ua_ref, rpre_ref, ipre_ref)
    nl = -lam_ref[...]
    c_half = (0.5 * LRU_C) * (jnp.maximum(nl, 0.0) + jnp.log1p(jnp.exp(-jnp.abs(nl))))
    for d in range(2):
        for n, row in enumerate((0.5 * br_ref[d:d + 1, :], 0.5 * bi_ref[d:d + 1, :], c_half[d:d + 1, :])):
            chan_ref[3 * d + n] = jnp.broadcast_to(row, chan_ref.shape[1:])

    def step(cur, prev):
        af_ref, uf_ref, ab_ref, ub_ref = prev

        def rows8(g, carry):
            hf, hb = carry
            r0 = pl.multiple_of(g * 8, 8)
            rows = pl.ds(r0, 8)
            for d in range(2):
                a, u = _lru_coefficients(ua_ref[d, rows, :], rpre_ref[d, rows, :], ipre_ref[d, rows, :],
                                         chan_ref[3 * d], chan_ref[3 * d + 1], chan_ref[3 * d + 2])
                for c in range(N_SLABS):
                    dst = pl.ds(c * pitch + r0, 8)
                    cur[2 * d][dst, :] = a[:, c * LANES:(c + 1) * LANES]
                    cur[2 * d + 1][dst, :] = u[:, c * LANES:(c + 1) * LANES]
            for j in range(8):
                k = r0 + j
                tf = pl.ds(k, N_SLABS, stride=pitch)
                tb = pl.ds(tm - 1 - k, N_SLABS, stride=pitch)
                hf = af_ref[tf, :] * hf + uf_ref[tf, :]
                sf_ref[tf, :] = hf
                hb = ab_ref[tb, :] * hb + ub_ref[tb, :]
                sb_ref[tb, :] = hb
            return hf, hb

        hf, hb = lax.fori_loop(0, tm // 8, rows8, (cf_ref[...], cbk_ref[...]), unroll=4)
        keep = i > 0
        cf_ref[...] = jnp.where(keep, hf, cf_ref[...])
        cbk_ref[...] = jnp.where(keep, hb, cbk_ref[...])
        for s_ref, out_ref in ((sf_ref, hf_ref), (sb_ref, hb_ref)):
            for c in range(N_SLABS):
                out_ref[0, :, c * LANES:(c + 1) * LANES] = s_ref[pl.ds(c * pitch, tm), :].astype(out_ref.dtype)

    @pl.when(i % 2 == 0)
    def _():
        step(coef_sets[0], coef_sets[1])

    @pl.when(i % 2 == 1)
    def _():
        step(coef_sets[1], coef_sets[0])


def _rg_lru(proj, w_rg, b_rg, w_ig, b_ig, lam):
    bsz, s, _ = proj.shape
    d = D_MODEL
    tm = min(256, s)
    nt = s // tm
    const2 = lambda b, i: (0, 0)
    const4 = lambda b, i: (0, 0, 0, 0)
    out_sds = jax.ShapeDtypeStruct((bsz, s, d), BF16)
    return pl.pallas_call(
        functools.partial(_lru_kernel, nt=nt),
        grid=(bsz, nt + 1),
        in_specs=[
            pl.BlockSpec((1, tm, d), lambda b, i: (b, jnp.minimum(i, nt - 1), COL_UA)),
            pl.BlockSpec((1, tm, d), lambda b, i: (b, nt - 1 - jnp.minimum(i, nt - 1), COL_UA)),
            pl.BlockSpec(w_rg.shape, const4),
            pl.BlockSpec(b_rg.shape, const2),
            pl.BlockSpec(w_ig.shape, const4),
            pl.BlockSpec(b_ig.shape, const2),
            pl.BlockSpec(lam.shape, const2),
        ],
        out_specs=[
            pl.BlockSpec((1, tm, d), lambda b, i: (b, jnp.maximum(i - 1, 0), 0)),
            pl.BlockSpec((1, tm, d), lambda b, i: (b, nt - 1 - jnp.maximum(i - 1, 0), 0)),
        ],
        out_shape=[out_sds, out_sds],
        scratch_shapes=[pltpu.VMEM((N_SLABS, LANES), F32)] * 2
        + [pltpu.VMEM((N_SLABS * _slab_pitch(tm), LANES), F32)] * 2
        + [pltpu.VMEM((2, tm, d), F32)] * 3
        + [pltpu.VMEM((6, 8, d), F32)]
        + [pltpu.VMEM((N_SLABS * _slab_pitch(tm), LANES), F32)] * 8,
        compiler_params=_cparams(("parallel", "arbitrary")),
        name="rg_lru",
    )(proj, proj, w_rg, b_rg, w_ig, b_ig, lam)


def _merge_kernel(hf_ref, hb_ref, ga_ref, yb_ref, yc_ref, mg0_ref, mg1_ref, mg2_ref, x_ref, mod_ref, wb_ref,
                  wo_ref, o_ref):
    merged = mg1_ref[0].astype(F32) * jnp.dot(yb_ref[0], wb_ref[1], preferred_element_type=F32)
    merged += mg2_ref[0].astype(F32) * jnp.dot(yc_ref[0], wb_ref[2], preferred_element_type=F32)
    ya = (hf_ref[0].astype(F32) + hb_ref[0].astype(F32)) * ga_ref[0].astype(F32)
    merged += mg0_ref[0].astype(F32) * jnp.dot(ya.astype(BF16), wb_ref[0], preferred_element_type=F32)
    out = jnp.dot(merged.astype(BF16), wo_ref[...], preferred_element_type=F32)
    o_ref[0] = x_ref[0] + mod_ref[0, 2:3, :] * out


def _merge(hf, hb, yb, proj, x, mod, w_branch, w_out):
    bsz, s, d = x.shape
    tm = min(512, s)

    def col(c):
        return pl.BlockSpec((1, tm, d), lambda b, i: (b, i, c))

    return pl.pallas_call(
        _merge_kernel,
        grid=(bsz, s // tm),
        in_specs=[
            col(0), col(0), col(COL_GA), col(0), col(COL_YC), col(COL_MG), col(COL_MG + 1), col(COL_MG + 2),
            col(0),
            pl.BlockSpec((1, 3, d), lambda b, i: (b, 0, 0)),
            pl.BlockSpec(w_branch.shape, lambda b, i: (0, 0, 0)),
            pl.BlockSpec(w_out.shape, lambda b, i: (0, 0)),
        ],
        out_specs=pl.BlockSpec((1, tm, d), lambda b, i: (b, i, 0)),
        out_shape=jax.ShapeDtypeStruct((bsz, s, d), F32),
        compiler_params=_cparams(("parallel", "parallel")),
        name="merge_outproj",
    )(hf, hb, proj, yb, proj, proj, proj, proj, x, mod, w_branch, w_out)


def _permute_w_in(w):
    return jnp.concatenate([w[:, off:off + INPROJ_TN] for _, off, _ in INPROJ_TILES], axis=1)


def _attn_tiles(s):
    tq = min(512, s)
    tk = min(512, s)
    return tq, tk


def kernel(x, c, w_ada, b_ada, norm_g, w_in, conv_a_w, conv_a_b, w_rg, b_rg, w_ig, b_ig, lru_lam, q_norm_g,
           k_norm_g, conv_c_w, w_branch, w_out):
    bsz, s, d = x.shape
    depth = w_ada.shape[0]
    tq, tk = _attn_tiles(s)
    tables = _rope_tables(s)
    mods = _modulation(c, w_ada, b_ada)
    for l in range(depth):
        mod = mods[l]
        proj, qt, kk, vt = _in_projection(x, mod, norm_g[l], _permute_w_in(w_in[l]).astype(BF16), tables,
                                          q_norm_g[l], k_norm_g[l], conv_a_w[l], conv_a_b[l], conv_c_w[l], tk)
        hf, hb = _rg_lru(proj, (0.5 * w_rg[l]).astype(BF16), b_rg[l], (0.5 * w_ig[l]).astype(BF16), b_ig[l],
                         lru_lam[l])
        bounded = (_score_bound(q_norm_g[l], k_norm_g[l]) <= SCORE_BOUND_NOSHIFT).astype(jnp.int32).reshape(1)
        yb = _attention(qt, kk, vt, proj, bounded, tq)
        x = _merge(hf, hb, yb, proj, x, mod, w_branch[l].astype(BF16), w_out[l].astype(BF16))
    return x
```

```python
import functools
import math

import numpy as np
import jax
import jax.numpy as jnp
from jax import lax
from jax.experimental import pallas as pl
from jax.experimental.pallas import tpu as pltpu

D_MODEL = 1024
EPS = 1e-6
GRID_W = 64
N_HEADS = 8
N_KV = 2
GROUP = N_HEADS // N_KV
HEAD_DIM = 128
RNN_BLOCKS = 4
RNN_BW = D_MODEL // RNN_BLOCKS
LRU_C = 8.0
ROPE_THETA = 10000.0
N_BRANCH = 3
LOG2E = 1.4426950408889634

KV_W = N_KV * HEAD_DIM
QKV_W = D_MODEL + 2 * KV_W
N_IN = 11 * D_MODEL + 2 * KV_W
INPROJ_TN = 512

COL_UA, COL_GA, COL_GB, COL_YC, COL_MG = 0, 1, 2, 3, 4
N_STORED = (COL_MG + N_BRANCH) * D_MODEL

_SRC = dict(xa=0, ga=D_MODEL, q=2 * D_MODEL, kv=3 * D_MODEL, gb=3 * D_MODEL + 2 * KV_W)
_SRC.update(xc=_SRC["gb"] + D_MODEL, bc=_SRC["gb"] + 2 * D_MODEL, cc=_SRC["gb"] + 3 * D_MODEL,
            gc=_SRC["gb"] + 4 * D_MODEL, mg=_SRC["gb"] + 5 * D_MODEL)
_TILE_ORDER = (("q", 0), ("mg", 0), ("q", 1), ("mg", 1), ("kv", 0), ("mg", 2), ("xa", 0), ("mg", 3),
               ("xa", 1), ("mg", 4), ("xc", 0), ("cc", 0), ("ga", 0), ("bc", 0), ("gc", 0), ("mg", 5),
               ("xc", 1), ("cc", 1), ("ga", 1), ("bc", 1), ("gc", 1), ("gb", 0), ("gb", 1))
INPROJ_TILES = tuple((k, _SRC[k] + t * INPROJ_TN, t) for k, t in _TILE_ORDER)
assert len(INPROJ_TILES) * INPROJ_TN == N_IN

HALO = 16
VMEM_LIMIT = 56 * 1024 * 1024

F32 = jnp.float32
BF16 = jnp.bfloat16


def _sigmoid(x):
    return 1.0 / (1.0 + jnp.exp2(x * (-LOG2E)))


def _silu(x):
    return x * _sigmoid(x)


def _cparams(sem):
    return pltpu.CompilerParams(dimension_semantics=sem, vmem_limit_bytes=VMEM_LIMIT)


def _mod_kernel(c_ref, w_ref, b_ref, o_ref):
    acc = jnp.dot(c_ref[...], w_ref[0], preferred_element_type=F32, precision=lax.Precision.HIGHEST)
    o_ref[0] = acc + b_ref[0]


def _modulation(c, w_ada, b_ada):
    depth, d, n = w_ada.shape
    bsz = c.shape[0]
    rows = 8
    c8 = jnp.zeros((rows, d), F32).at[:bsz].set(c)
    tn = 1024
    out = pl.pallas_call(
        _mod_kernel,
        grid=(depth, n // tn),
        in_specs=[
            pl.BlockSpec((rows, d), lambda l, j: (0, 0)),
            pl.BlockSpec((1, d, tn), lambda l, j: (l, 0, j)),
            pl.BlockSpec((1, 1, tn), lambda l, j: (l, 0, j)),
        ],
        out_specs=pl.BlockSpec((1, rows, tn), lambda l, j: (l, 0, j)),
        out_shape=jax.ShapeDtypeStruct((depth, rows, n), F32),
        compiler_params=_cparams(("parallel", "parallel")),
        name="adaln_mod",
    )(c8, w_ada, b_ada.reshape(depth, 1, n))
    return out[:, :bsz].reshape(depth, bsz, 3, d)


def _rope_tables(s):
    half = HEAD_DIM // 2
    inv = ROPE_THETA ** (-np.arange(0, half, 2, dtype=np.float64) / half)
    rows = s // GRID_W
    ang_r = np.arange(rows, dtype=np.float64)[:, None] * inv
    ang_c = np.arange(GRID_W, dtype=np.float64)[:, None] * inv
    zr = np.zeros((rows, half))
    zc = np.zeros((GRID_W, half))
    row_cos = np.concatenate([np.cos(ang_r), np.cos(ang_r), zr], -1)
    row_sin = np.concatenate([-np.sin(ang_r), np.sin(ang_r), zr], -1)
    col_cos = np.concatenate([zc, np.cos(ang_c), np.cos(ang_c)], -1)
    col_sin = np.concatenate([zc, -np.sin(ang_c), np.sin(ang_c)], -1)
    bcast = lambda t: jnp.broadcast_to(jnp.asarray(t, F32)[:, None, :], (rows, 8, HEAD_DIM))
    return bcast(row_cos), bcast(row_sin), jnp.asarray(col_cos, F32), jnp.asarray(col_sin, F32)


def _norm_rope(xh, g, cos_t, sin_t, lane_lo):
    ms = jnp.mean(xh * xh, axis=-1, keepdims=True)
    y = xh * lax.rsqrt(ms + EPS) * g
    partner = jnp.where(lane_lo, pltpu.roll(y, HEAD_DIM - 32, axis=1), pltpu.roll(y, 32, axis=1))
    return y * cos_t + partner * sin_t


def _shift_rows(tile, halo, shift):
    t = tile.shape[0]
    row8 = lax.broadcasted_iota(jnp.int32, (8, tile.shape[1]), 0)
    if shift > 0:
        rolled = pltpu.roll(tile, shift, axis=0)
        edge = pltpu.roll(halo, shift, axis=0)[0:8]
        fixed = jnp.where(row8 < shift, edge, rolled[0:8])
        return jnp.concatenate([fixed, rolled[8:]], axis=0)
    k = -shift
    rolled = pltpu.roll(tile, t - k, axis=0)
    edge = pltpu.roll(halo, HALO - k, axis=0)[HALO - 8:HALO]
    fixed = jnp.where(row8 >= 8 - k, edge, rolled[t - 8:t])
    return jnp.concatenate([rolled[:t - 8], fixed], axis=0)


def _inproj_kernel(x_ref, xp_ref, xn_ref, mod_ref, g_ref, w_ref, rcos_ref, rsin_ref, ccos_ref, csin_ref, qg_ref,
                   kg_ref, caw_ref, cab_ref, ccw_ref, o_ref, qt_ref, k_ref, vt_ref, *, nt):
    i = pl.program_id(1)
    tm = x_ref.shape[1]
    xv = jnp.concatenate([xp_ref[0], x_ref[0], xn_ref[0]], axis=0)
    ms = jnp.mean(xv * xv, axis=-1, keepdims=True)
    y = xv * lax.rsqrt(ms + EPS) * g_ref[...]
    shift = mod_ref[0, 0:1, :]
    scale = mod_ref[0, 1:2, :]
    h_ext = (y * (1.0 + scale) + shift).astype(BF16)
    h = h_ext[HALO:HALO + tm]

    def with_halos(acc_ext):
        prev = jnp.where(i == 0, 0.0, acc_ext[0:HALO])
        nxt = jnp.where(i == nt - 1, 0.0, acc_ext[HALO + tm:2 * HALO + tm])
        return acc_ext[HALO:HALO + tm], prev, nxt

    reps = GRID_W // 8
    cos_t = jnp.concatenate([jnp.tile(rcos_ref[r], (reps, 1)) + ccos_ref[...] for r in range(tm // GRID_W)], 0)
    sin_t = jnp.concatenate([jnp.tile(rsin_ref[r], (reps, 1)) + csin_ref[...] for r in range(tm // GRID_W)], 0)
    lane = lax.broadcasted_iota(jnp.int32, cos_t.shape, 1)
    lane_lo = (lane % 64) < 32
    q_scale = (HEAD_DIM ** -0.5) * LOG2E

    def project(j):
        lhs = h_ext if INPROJ_TILES[j][0] in ("xa", "xc", "cc") else h
        src = INPROJ_TILES[j][1]
        return jnp.dot(lhs, w_ref[:, src:src + INPROJ_TN], preferred_element_type=F32)

    held = {}
    acc_next = project(0)
    for j, (kind, _, half) in enumerate(INPROJ_TILES):
        lanes = slice(half * INPROJ_TN, (half + 1) * INPROJ_TN)
        acc = acc_next
        if j + 1 < len(INPROJ_TILES):
            acc_next = project(j + 1)

        def store(col, val):
            o_ref[0, :, col * D_MODEL + half * INPROJ_TN:col * D_MODEL + (half + 1) * INPROJ_TN] = val.astype(BF16)

        if kind == "q":
            for hh in range(INPROJ_TN // HEAD_DIM):
                qh = _norm_rope(acc[:, hh * HEAD_DIM:(hh + 1) * HEAD_DIM], qg_ref[...], cos_t, sin_t, lane_lo)
                qt_ref[0, half * (INPROJ_TN // HEAD_DIM) + hh] = (qh * q_scale).T.astype(BF16)
        elif kind == "kv":
            for hh in range(N_KV):
                kh = acc[:, hh * HEAD_DIM:(hh + 1) * HEAD_DIM]
                k_ref[0, hh, 0] = _norm_rope(kh, kg_ref[...], cos_t, sin_t, lane_lo).astype(BF16)
                vt_ref[0, hh, 0] = acc[:, KV_W + hh * HEAD_DIM:KV_W + (hh + 1) * HEAD_DIM].T.astype(BF16)
        elif kind == "xa":
            tile, prev, nxt = with_halos(acc)
            cw = caw_ref[:, lanes]
            store(COL_UA, cw[0:1] * _shift_rows(tile, prev, 2) + cw[1:2] * _shift_rows(tile, prev, 1)
                  + cw[2:3] * tile + cw[3:4] * _shift_rows(tile, nxt, -1) + cab_ref[:, lanes])
        elif kind == "xc":
            held["xc"] = acc
        elif kind == "cc":
            tile, prev, nxt = with_halos(acc * held.pop("xc"))
            cw = ccw_ref[:, lanes]
            held["conv"] = cw[0:1] * _shift_rows(tile, prev, 1) + cw[1:2] * tile + cw[2:3] * _shift_rows(tile, nxt, -1)
        elif kind == "bc":
            held["conv"] = acc * held["conv"]
        elif kind == "gc":
            store(COL_YC, held.pop("conv") * _silu(acc))
        elif kind == "ga":
            store(COL_GA, _silu(acc))
        elif kind == "gb":
            store(COL_GB, _silu(acc))
        else:
            o_ref[0, :, COL_MG * D_MODEL + half * INPROJ_TN:COL_MG * D_MODEL + (half + 1) * INPROJ_TN] = (
                _sigmoid(acc).astype(BF16))


def _layer_spec(arr, l):
    nd = arr.ndim - 1
    assert nd >= 2
    return pl.BlockSpec((None,) + arr.shape[1:], lambda *_: (l,) + (0,) * nd)


def _mod_spec(mods, l):
    return pl.BlockSpec((None, 1) + mods.shape[2:], lambda b, i: (l, b, 0, 0))


def _in_projection(x, mods, l, norm_g, w_in, tables, q_g, k_g, conv_a_w, conv_a_b, conv_c_w, tk):
    bsz, s, d = x.shape
    tm = min(256, s)
    assert INPROJ_TN == 2 * KV_W and D_MODEL == 2 * INPROJ_TN and tm % GRID_W == 0 and tk % tm == 0
    nt = s // tm
    nkb = s // tk
    sub = tk // tm
    hbk = tm // HALO
    rcos, rsin, ccos, csin = tables
    row_spec = pl.BlockSpec((tm // GRID_W, 8, HEAD_DIM), lambda b, i: (i, 0, 0))
    col_spec = pl.BlockSpec((GRID_W, HEAD_DIM), lambda b, i: (0, 0))
    return pl.pallas_call(
        functools.partial(_inproj_kernel, nt=nt),
        grid=(bsz, nt),
        in_specs=[
            pl.BlockSpec((1, tm, d), lambda b, i: (b, i, 0)),
            pl.BlockSpec((1, HALO, d), lambda b, i: (b, jnp.maximum(i * hbk - 1, 0), 0)),
            pl.BlockSpec((1, HALO, d), lambda b, i: (b, jnp.minimum((i + 1) * hbk, s // HALO - 1), 0)),
            _mod_spec(mods, l),
            _layer_spec(norm_g, l),
            pl.BlockSpec((None, d, N_IN), lambda b, i: (l, 0, 0), pipeline_mode=pl.Buffered(1)),
            row_spec, row_spec, col_spec, col_spec, _layer_spec(q_g, l), _layer_spec(k_g, l),
            _layer_spec(conv_a_w, l), _layer_spec(conv_a_b, l), _layer_spec(conv_c_w, l),
        ],
        out_specs=[
            pl.BlockSpec((1, tm, N_STORED), lambda b, i: (b, i, 0)),
            pl.BlockSpec((1, N_HEADS, HEAD_DIM, tm), lambda b, i: (b, 0, 0, i)),
            pl.BlockSpec((1, N_KV, 1, tm, HEAD_DIM), lambda b, i: (b, 0, i // sub, i % sub, 0)),
            pl.BlockSpec((1, N_KV, 1, HEAD_DIM, tm), lambda b, i: (b, 0, i // sub, 0, i % sub)),
        ],
        out_shape=[
            jax.ShapeDtypeStruct((bsz, s, N_STORED), BF16),
            jax.ShapeDtypeStruct((bsz, N_HEADS, HEAD_DIM, s), BF16),
            jax.ShapeDtypeStruct((bsz, N_KV, nkb, tk, HEAD_DIM), BF16),
            jax.ShapeDtypeStruct((bsz, N_KV, nkb, HEAD_DIM, tk), BF16),
        ],
        compiler_params=_cparams(("parallel", "parallel")),
        name="norm_inproj",
    )(x, x, x, mods, norm_g, w_in, rcos, rsin, ccos, csin, q_g, k_g, conv_a_w, conv_a_b, conv_c_w)


SCORE_BOUND_NOSHIFT = 40.0


def _score_bound(q_g, k_g):
    q_scale = (HEAD_DIM ** -0.5) * LOG2E
    return HEAD_DIM * q_scale * jnp.max(jnp.abs(q_g), axis=-1) * jnp.max(jnp.abs(k_g), axis=-1)


KV_BLOCKS_PER_TRIP = 16


def _largest_divisor(n, cap):
    return max(d for d in range(1, cap + 1) if n % d == 0)


def _attn_kernel(flag_ref, qt_ref, k_ref, vt_ref, gb_ref, o_ref, m_ref, l_ref, acc_ref, s_ref, *, nkb, layer):
    tq = acc_ref.shape[-1]
    l_ref[...] = jnp.zeros(l_ref.shape, F32)
    acc_ref[...] = jnp.zeros(acc_ref.shape, F32)
    bounded = flag_ref[layer] != 0

    @pl.when(bounded)
    def _():
        def scores(j, h):
            return jnp.dot(k_ref[0, 0, j], qt_ref[0, h], preferred_element_type=F32)

        s_ref[...] = scores(0, 0)

        group = _largest_divisor(nkb, KV_BLOCKS_PER_TRIP)

        def kv_group(g, carry):
            s_cur = s_ref[...]
            for u in range(group):
                j = g * group + u
                vblk = vt_ref[0, 0, j]
                for h in range(GROUP):
                    s_next = scores(j, h + 1) if h + 1 < GROUP else scores(jnp.minimum(j + 1, nkb - 1), 0)
                    p = jnp.exp2(s_cur)
                    l_ref[h] += jnp.sum(p.reshape(-1, 8, tq), axis=0)
                    acc_ref[h] += jnp.dot(vblk, p.astype(BF16), preferred_element_type=F32)
                    s_cur = s_next
            s_ref[...] = s_cur
            return carry

        lax.fori_loop(0, nkb // group, kv_group, 0)

    @pl.when(jnp.logical_not(bounded))
    def _():
        m_ref[...] = jnp.full(m_ref.shape, -jnp.inf, F32)

        def kv_step(j, carry):
            kblk = k_ref[0, 0, j]
            vblk = vt_ref[0, 0, j]
            for h in range(GROUP):
                s = jnp.dot(kblk, qt_ref[0, h], preferred_element_type=F32)
                m_old = m_ref[h]
                m_new = jnp.maximum(m_old, jnp.max(s, axis=0, keepdims=True))
                alpha = jnp.exp2(m_old - m_new)
                p = jnp.exp2(s - m_new)
                l_ref[h] = alpha * l_ref[h] + jnp.sum(p.reshape(-1, 8, tq), axis=0)
                acc_ref[h] = alpha * acc_ref[h] + jnp.dot(vblk, p.astype(BF16), preferred_element_type=F32)
                m_ref[h] = m_new
            return carry

        lax.fori_loop(0, nkb, kv_step, 0)

    for h in range(GROUP):
        l = jnp.sum(l_ref[h], axis=0, keepdims=True)
        o = acc_ref[h] * (1.0 / l)
        gate = gb_ref[0, :, h * HEAD_DIM:(h + 1) * HEAD_DIM].astype(F32)
        o_ref[0, :, h * HEAD_DIM:(h + 1) * HEAD_DIM] = (o.T * gate).astype(BF16)


def _attention(qt, kk, vt, proj, bounded_flags, layer, tq):
    bsz, _, _, s = qt.shape
    nkb, tk = kk.shape[2], kk.shape[3]
    gw = GROUP * HEAD_DIM
    grid_spec = pltpu.PrefetchScalarGridSpec(
        num_scalar_prefetch=1,
        grid=(bsz, N_KV, s // tq),
        in_specs=[
            pl.BlockSpec((1, GROUP, HEAD_DIM, tq), lambda b, g, i, f: (b, g, 0, i)),
            pl.BlockSpec((1, 1, nkb, tk, HEAD_DIM), lambda b, g, i, f: (b, g, 0, 0, 0)),
            pl.BlockSpec((1, 1, nkb, HEAD_DIM, tk), lambda b, g, i, f: (b, g, 0, 0, 0)),
            pl.BlockSpec((1, tq, gw), lambda b, g, i, f: (b, i, COL_GB * D_MODEL // gw + g)),
        ],
        out_specs=pl.BlockSpec((1, tq, gw), lambda b, g, i, f: (b, i, g)),
        scratch_shapes=[
            pltpu.VMEM((GROUP, 1, tq), F32),
            pltpu.VMEM((GROUP, 8, tq), F32),
            pltpu.VMEM((GROUP, HEAD_DIM, tq), F32),
            pltpu.VMEM((tk, tq), F32),
        ],
    )
    return pl.pallas_call(
        functools.partial(_attn_kernel, nkb=nkb, layer=layer),
        grid_spec=grid_spec,
        out_shape=jax.ShapeDtypeStruct((bsz, s, D_MODEL), BF16),
        compiler_params=_cparams(("parallel", "parallel", "arbitrary")),
        name="gqa_attention",
    )(bounded_flags, qt, kk, vt, proj)


def _lru_gates(x_ref, wr_ref, wi_ref, d, ua_ref, rpre_ref, ipre_ref):
    ub = x_ref[0]
    ua_ref[d] = ub.astype(F32)
    for n in range(RNN_BLOCKS):
        cols = slice(n * RNN_BW, (n + 1) * RNN_BW)
        rpre_ref[d, :, cols] = jnp.dot(ub[:, cols], wr_ref[d, n], preferred_element_type=F32)
        ipre_ref[d, :, cols] = jnp.dot(ub[:, cols], wi_ref[d, n], preferred_element_type=F32)


def _lru_coefficients(ua, r_half, i_half, b_r_half, b_i_half, c_half):
    y = c_half + c_half * jnp.tanh(r_half + b_r_half)
    ua_half = 0.5 * ua
    gated = ua_half + ua_half * jnp.tanh(i_half + b_i_half)
    a = jnp.exp2(y * (-LOG2E))
    x = jnp.tanh(y) * (1.0 + a * a)
    root = jnp.where(x > 0.0, x * lax.rsqrt(x), x)
    return a, root * gated


LANES = 128
N_SLABS = D_MODEL // LANES


def _slab_pitch(t):
    return t + 8 if (t // 8) % 2 == 0 else t


def _lru_kernel(xf_ref, xb_ref, wr_ref, br_ref, wi_ref, bi_ref, lam_ref, hf_ref, hb_ref, cf_ref, cbk_ref,
                sf_ref, sb_ref, ua_ref, rpre_ref, ipre_ref, chan_ref, *coef_refs, nt):
    i = pl.program_id(1)
    tm = hf_ref.shape[1]
    pitch = _slab_pitch(tm)
    coef_sets = (coef_refs[:4], coef_refs[4:])

    @pl.when(i == 0)
    def _():
        cf_ref[...] = jnp.zeros(cf_ref.shape, F32)
        cbk_ref[...] = jnp.zeros(cbk_ref.shape, F32)
        for ref in coef_sets[1]:
            ref[...] = jnp.zeros(ref.shape, F32)

    _lru_gates(xf_ref, wr_ref, wi_ref, 0, ua_ref, rpre_ref, ipre_ref)
    _lru_gates(xb_ref, wr_ref, wi_ref, 1, ua_ref, rpre_ref, ipre_ref)
    nl = -lam_ref[...]
    c_half = (0.5 * LRU_C) * (jnp.maximum(nl, 0.0) + jnp.log1p(jnp.exp(-jnp.abs(nl))))
    for d in range(2):
        for n, row in enumerate((0.5 * br_ref[d:d + 1, :], 0.5 * bi_ref[d:d + 1, :], c_half[d:d + 1, :])):
            chan_ref[3 * d + n] = jnp.broadcast_to(row, chan_ref.shape[1:])

    def step(cur, prev):
        af_ref, uf_ref, ab_ref, ub_ref = prev

        def rows8(g, carry):
            hf, hb = carry
            r0 = pl.multiple_of(g * 8, 8)
            rows = pl.ds(r0, 8)
            for d in range(2):
                a, u = _lru_coefficients(ua_ref[d, rows, :], rpre_ref[d, rows, :], ipre_ref[d, rows, :],
                                         chan_ref[3 * d], chan_ref[3 * d + 1], chan_ref[3 * d + 2])
                for c in range(N_SLABS):
                    dst = pl.ds(c * pitch + r0, 8)
                    cur[2 * d][dst, :] = a[:, c * LANES:(c + 1) * LANES]
                    cur[2 * d + 1][dst, :] = u[:, c * LANES:(c + 1) * LANES]
            for j in range(8):
                k = r0 + j
                tf = pl.ds(k, N_SLABS, stride=pitch)
                tb = pl.ds(tm - 1 - k, N_SLABS, stride=pitch)
                hf = af_ref[tf, :] * hf + uf_ref[tf, :]
                sf_ref[tf, :] = hf
                hb = ab_ref[tb, :] * hb + ub_ref[tb, :]
                sb_ref[tb, :] = hb
            return hf, hb

        hf, hb = lax.fori_loop(0, tm // 8, rows8, (cf_ref[...], cbk_ref[...]), unroll=4)
        keep = i > 0
        cf_ref[...] = jnp.where(keep, hf, cf_ref[...])
        cbk_ref[...] = jnp.where(keep, hb, cbk_ref[...])
        for s_ref, out_ref in ((sf_ref, hf_ref), (sb_ref, hb_ref)):
            for c in range(N_SLABS):
                out_ref[0, :, c * LANES:(c + 1) * LANES] = s_ref[pl.ds(c * pitch, tm), :].astype(out_ref.dtype)

    @pl.when(i % 2 == 0)
    def _():
        step(coef_sets[0], coef_sets[1])

    @pl.when(i % 2 == 1)
    def _():
        step(coef_sets[1], coef_sets[0])


def _rg_lru(proj, l, w_rg, b_rg, w_ig, b_ig, lam):
    bsz, s, _ = proj.shape
    d = D_MODEL
    tm = min(256, s)
    nt = s // tm
    out_sds = jax.ShapeDtypeStruct((bsz, s, d), BF16)
    return pl.pallas_call(
        functools.partial(_lru_kernel, nt=nt),
        grid=(bsz, nt + 1),
        in_specs=[
            pl.BlockSpec((1, tm, d), lambda b, i: (b, jnp.minimum(i, nt - 1), COL_UA)),
            pl.BlockSpec((1, tm, d), lambda b, i: (b, nt - 1 - jnp.minimum(i, nt - 1), COL_UA)),
            _layer_spec(w_rg, l), _layer_spec(b_rg, l), _layer_spec(w_ig, l), _layer_spec(b_ig, l),
            _layer_spec(lam, l),
        ],
        out_specs=[
            pl.BlockSpec((1, tm, d), lambda b, i: (b, jnp.maximum(i - 1, 0), 0)),
            pl.BlockSpec((1, tm, d), lambda b, i: (b, nt - 1 - jnp.maximum(i - 1, 0), 0)),
        ],
        out_shape=[out_sds, out_sds],
        scratch_shapes=[pltpu.VMEM((N_SLABS, LANES), F32)] * 2
        + [pltpu.VMEM((N_SLABS * _slab_pitch(tm), LANES), F32)] * 2
        + [pltpu.VMEM((2, tm, d), F32)] * 3
        + [pltpu.VMEM((6, 8, d), F32)]
        + [pltpu.VMEM((N_SLABS * _slab_pitch(tm), LANES), F32)] * 8,
        compiler_params=_cparams(("parallel", "arbitrary")),
        name="rg_lru",
    )(proj, proj, w_rg, b_rg, w_ig, b_ig, lam)


def _merge_kernel(hf_ref, hb_ref, ga_ref, yb_ref, yc_ref, mg0_ref, mg1_ref, mg2_ref, x_ref, mod_ref, wb_ref,
                  wo_ref, o_ref):
    merged = mg1_ref[0].astype(F32) * jnp.dot(yb_ref[0], wb_ref[1], preferred_element_type=F32)
    merged += mg2_ref[0].astype(F32) * jnp.dot(yc_ref[0], wb_ref[2], preferred_element_type=F32)
    ya = (hf_ref[0].astype(F32) + hb_ref[0].astype(F32)) * ga_ref[0].astype(F32)
    merged += mg0_ref[0].astype(F32) * jnp.dot(ya.astype(BF16), wb_ref[0], preferred_element_type=F32)
    out = jnp.dot(merged.astype(BF16), wo_ref[...], preferred_element_type=F32)
    o_ref[0] = x_ref[0] + mod_ref[0, 2:3, :] * out


def _merge(hf, hb, yb, proj, x, mods, l, w_branch, w_out):
    bsz, s, d = x.shape
    tm = min(512, s)

    def col(c):
        return pl.BlockSpec((1, tm, d), lambda b, i: (b, i, c))

    return pl.pallas_call(
        _merge_kernel,
        grid=(bsz, s // tm),
        in_specs=[
            col(0), col(0), col(COL_GA), col(0), col(COL_YC), col(COL_MG), col(COL_MG + 1), col(COL_MG + 2),
            col(0),
            _mod_spec(mods, l), _layer_spec(w_branch, l), _layer_spec(w_out, l),
        ],
        out_specs=pl.BlockSpec((1, tm, d), lambda b, i: (b, i, 0)),
        out_shape=jax.ShapeDtypeStruct((bsz, s, d), F32),
        compiler_params=_cparams(("parallel", "parallel")),
        name="merge_outproj",
    )(hf, hb, proj, yb, proj, proj, proj, proj, x, mods, w_branch, w_out)


def _attn_tiles(s):
    tq = min(512, s)
    tk = min(512, s)
    return tq, tk


def kernel(x, c, w_ada, b_ada, norm_g, w_in, conv_a_w, conv_a_b, w_rg, b_rg, w_ig, b_ig, lru_lam, q_norm_g,
           k_norm_g, conv_c_w, w_branch, w_out):
    bsz, s, d = x.shape
    depth = w_ada.shape[0]
    tq, tk = _attn_tiles(s)
    tables = _rope_tables(s)
    mods = _modulation(c, w_ada, b_ada)
    w_in_bf = w_in.astype(BF16)
    w_rg_half = (0.5 * w_rg).astype(BF16)
    w_ig_half = (0.5 * w_ig).astype(BF16)
    w_branch_bf = w_branch.astype(BF16)
    w_out_bf = w_out.astype(BF16)
    bounded = (_score_bound(q_norm_g, k_norm_g) <= SCORE_BOUND_NOSHIFT).astype(jnp.int32)
    rows = lambda p: p.reshape(depth, 1, p.shape[-1])
    norm_g, conv_a_b, q_norm_g, k_norm_g = rows(norm_g), rows(conv_a_b), rows(q_norm_g), rows(k_norm_g)
    for l in range(depth):
        proj, qt, kk, vt = _in_projection(x, mods, l, norm_g, w_in_bf, tables, q_norm_g, k_norm_g, conv_a_w,
                                          conv_a_b, conv_c_w, tk)
        hf, hb = _rg_lru(proj, l, w_rg_half, b_rg, w_ig_half, b_ig, lru_lam)
        yb = _attention(qt, kk, vt, proj, bounded, l, tq)
        x = _merge(hf, hb, yb, proj, x, mods, l, w_branch_bf, w_out_bf)
    return x
```

```python
import functools

import numpy as np
import jax
import jax.numpy as jnp
from jax import lax
from jax.experimental import pallas as pl
from jax.experimental.pallas import tpu as pltpu

D_MODEL = 1024
EPS = 1e-6
GRID_W = 64
N_HEADS = 8
N_KV = 2
GROUP = N_HEADS // N_KV
HEAD_DIM = 128
RNN_BLOCKS = 4
RNN_BW = D_MODEL // RNN_BLOCKS
LRU_C = 8.0
ROPE_THETA = 10000.0
N_BRANCH = 3
LOG2E = 1.4426950408889634

KV_W = N_KV * HEAD_DIM
QKV_W = D_MODEL + 2 * KV_W
N_IN = 11 * D_MODEL + 2 * KV_W
INPROJ_TN = 512

COL_UA, COL_GA, COL_GB, COL_YC, COL_MG = 0, 1, 2, 3, 4
N_STORED = (COL_MG + N_BRANCH) * D_MODEL

_SRC = dict(xa=0, ga=D_MODEL, q=2 * D_MODEL, kv=3 * D_MODEL, gb=3 * D_MODEL + 2 * KV_W)
_SRC.update(xc=_SRC["gb"] + D_MODEL, bc=_SRC["gb"] + 2 * D_MODEL, cc=_SRC["gb"] + 3 * D_MODEL,
            gc=_SRC["gb"] + 4 * D_MODEL, mg=_SRC["gb"] + 5 * D_MODEL)
_TILE_ORDER = (("q", 0), ("mg", 0), ("q", 1), ("mg", 1), ("kv", 0), ("mg", 2), ("xa", 0), ("mg", 3),
               ("xa", 1), ("mg", 4), ("xc", 0), ("cc", 0), ("ga", 0), ("bc", 0), ("gc", 0), ("mg", 5),
               ("xc", 1), ("cc", 1), ("ga", 1), ("bc", 1), ("gc", 1), ("gb", 0), ("gb", 1))
INPROJ_TILES = tuple((k, _SRC[k] + t * INPROJ_TN, t) for k, t in _TILE_ORDER)
assert len(INPROJ_TILES) * INPROJ_TN == N_IN

SUBLANES = 8
LANES = 128
ROPE_SPAN = HEAD_DIM // 2
ROPE_PAIR = ROPE_SPAN // 2
HALO = 16
VMEM_LIMIT = 56 * 1024 * 1024

F32 = jnp.float32
BF16 = jnp.bfloat16


def _sigmoid(x):
    return 1.0 / (1.0 + jnp.exp2(x * (-LOG2E)))


def _silu(x):
    return x * _sigmoid(x)


def _cparams(sem):
    return pltpu.CompilerParams(dimension_semantics=sem, vmem_limit_bytes=VMEM_LIMIT)


def _mod_kernel(c_ref, w_ref, b_ref, o_ref):
    acc = jnp.dot(c_ref[...], w_ref[0], preferred_element_type=F32, precision=lax.Precision.HIGHEST)
    o_ref[0] = acc + b_ref[0]


def _modulation(c, w_ada, b_ada):
    depth, d, n = w_ada.shape
    bsz = c.shape[0]
    rows = SUBLANES
    assert bsz <= rows
    c8 = jnp.zeros((rows, d), F32).at[:bsz].set(c)
    tn = 1024
    out = pl.pallas_call(
        _mod_kernel,
        grid=(depth, n // tn),
        in_specs=[
            pl.BlockSpec((rows, d), lambda l, j: (0, 0)),
            pl.BlockSpec((1, d, tn), lambda l, j: (l, 0, j)),
            pl.BlockSpec((1, 1, tn), lambda l, j: (l, 0, j)),
        ],
        out_specs=pl.BlockSpec((1, rows, tn), lambda l, j: (l, 0, j)),
        out_shape=jax.ShapeDtypeStruct((depth, rows, n), F32),
        compiler_params=_cparams(("parallel", "parallel")),
        name="adaln_mod",
    )(c8, w_ada, b_ada.reshape(depth, 1, n))
    return out[:, :bsz].reshape(depth, bsz, 3, d)


def _rope_tables(s):
    half = HEAD_DIM // 2
    inv = ROPE_THETA ** (-np.arange(0, half, 2, dtype=np.float64) / half)
    rows = s // GRID_W
    ang_r = np.arange(rows, dtype=np.float64)[:, None] * inv
    ang_c = np.arange(GRID_W, dtype=np.float64)[:, None] * inv
    zr = np.zeros((rows, half))
    zc = np.zeros((GRID_W, half))
    row_cos = np.concatenate([np.cos(ang_r), np.cos(ang_r), zr], -1)
    row_sin = np.concatenate([-np.sin(ang_r), np.sin(ang_r), zr], -1)
    col_cos = np.concatenate([zc, np.cos(ang_c), np.cos(ang_c)], -1)
    col_sin = np.concatenate([zc, -np.sin(ang_c), np.sin(ang_c)], -1)
    bcast = lambda t: jnp.broadcast_to(jnp.asarray(t, F32)[:, None, :], (rows, SUBLANES, HEAD_DIM))
    return bcast(row_cos), bcast(row_sin), jnp.asarray(col_cos, F32), jnp.asarray(col_sin, F32)


def _norm_rope(xh, g, cos_t, sin_t, lane_lo):
    ms = jnp.mean(xh * xh, axis=-1, keepdims=True)
    y = xh * lax.rsqrt(ms + EPS) * g
    partner = jnp.where(lane_lo, pltpu.roll(y, HEAD_DIM - ROPE_PAIR, axis=1), pltpu.roll(y, ROPE_PAIR, axis=1))
    return y * cos_t + partner * sin_t


def _shift_rows(tile, halo, shift):
    t = tile.shape[0]
    g = SUBLANES
    row = lax.broadcasted_iota(jnp.int32, (g, tile.shape[1]), 0)
    if shift > 0:
        rolled = pltpu.roll(tile, shift, axis=0)
        edge = pltpu.roll(halo, shift, axis=0)[0:g]
        fixed = jnp.where(row < shift, edge, rolled[0:g])
        return jnp.concatenate([fixed, rolled[g:]], axis=0)
    k = -shift
    rolled = pltpu.roll(tile, t - k, axis=0)
    edge = pltpu.roll(halo, HALO - k, axis=0)[HALO - g:HALO]
    fixed = jnp.where(row >= g - k, edge, rolled[t - g:t])
    return jnp.concatenate([rolled[:t - g], fixed], axis=0)


def _inproj_kernel(x_ref, xp_ref, xn_ref, mod_ref, g_ref, w_ref, rcos_ref, rsin_ref, ccos_ref, csin_ref, qg_ref,
                   kg_ref, caw_ref, cab_ref, ccw_ref, o_ref, qt_ref, k_ref, vt_ref, *, nt):
    i = pl.program_id(1)
    tm = x_ref.shape[1]
    xv = jnp.concatenate([xp_ref[0], x_ref[0], xn_ref[0]], axis=0)
    ms = jnp.mean(xv * xv, axis=-1, keepdims=True)
    y = xv * lax.rsqrt(ms + EPS) * g_ref[...]
    shift = mod_ref[0, 0:1, :]
    scale = mod_ref[0, 1:2, :]
    h_ext = (y * (1.0 + scale) + shift).astype(BF16)
    h = h_ext[HALO:HALO + tm]

    def with_halos(acc_ext):
        prev = jnp.where(i == 0, 0.0, acc_ext[0:HALO])
        nxt = jnp.where(i == nt - 1, 0.0, acc_ext[HALO + tm:2 * HALO + tm])
        return acc_ext[HALO:HALO + tm], prev, nxt

    reps = GRID_W // SUBLANES
    cos_t = jnp.concatenate([jnp.tile(rcos_ref[r], (reps, 1)) + ccos_ref[...] for r in range(tm // GRID_W)], 0)
    sin_t = jnp.concatenate([jnp.tile(rsin_ref[r], (reps, 1)) + csin_ref[...] for r in range(tm // GRID_W)], 0)
    lane = lax.broadcasted_iota(jnp.int32, cos_t.shape, 1)
    lane_lo = (lane % ROPE_SPAN) < ROPE_PAIR
    q_scale = (HEAD_DIM ** -0.5) * LOG2E

    def project(j):
        lhs = h_ext if INPROJ_TILES[j][0] in ("xa", "xc", "cc") else h
        src = INPROJ_TILES[j][1]
        return jnp.dot(lhs, w_ref[:, src:src + INPROJ_TN], preferred_element_type=F32)

    held = {}
    acc_next = project(0)
    for j, (kind, _, half) in enumerate(INPROJ_TILES):
        lanes = slice(half * INPROJ_TN, (half + 1) * INPROJ_TN)
        acc = acc_next
        if j + 1 < len(INPROJ_TILES):
            acc_next = project(j + 1)

        def store(col, val):
            o_ref[0, :, col * D_MODEL + half * INPROJ_TN:col * D_MODEL + (half + 1) * INPROJ_TN] = val.astype(BF16)

        if kind == "q":
            for hh in range(INPROJ_TN // HEAD_DIM):
                qh = _norm_rope(acc[:, hh * HEAD_DIM:(hh + 1) * HEAD_DIM], qg_ref[...], cos_t, sin_t, lane_lo)
                qt_ref[0, half * (INPROJ_TN // HEAD_DIM) + hh] = (qh * q_scale).T.astype(BF16)
        elif kind == "kv":
            for hh in range(N_KV):
                kh = acc[:, hh * HEAD_DIM:(hh + 1) * HEAD_DIM]
                k_ref[0, hh, 0] = _norm_rope(kh, kg_ref[...], cos_t, sin_t, lane_lo).astype(BF16)
                vt_ref[0, hh, 0] = acc[:, KV_W + hh * HEAD_DIM:KV_W + (hh + 1) * HEAD_DIM].T.astype(BF16)
        elif kind == "xa":
            tile, prev, nxt = with_halos(acc)
            cw = caw_ref[:, lanes]
            store(COL_UA, cw[0:1] * _shift_rows(tile, prev, 2) + cw[1:2] * _shift_rows(tile, prev, 1)
                  + cw[2:3] * tile + cw[3:4] * _shift_rows(tile, nxt, -1) + cab_ref[:, lanes])
        elif kind == "xc":
            held["xc"] = acc
        elif kind == "cc":
            tile, prev, nxt = with_halos(acc * held.pop("xc"))
            cw = ccw_ref[:, lanes]
            held["conv"] = cw[0:1] * _shift_rows(tile, prev, 1) + cw[1:2] * tile + cw[2:3] * _shift_rows(tile, nxt, -1)
        elif kind == "bc":
            held["conv"] = acc * held["conv"]
        elif kind == "gc":
            store(COL_YC, held.pop("conv") * _silu(acc))
        elif kind == "ga":
            store(COL_GA, _silu(acc))
        elif kind == "gb":
            store(COL_GB, _silu(acc))
        else:
            o_ref[0, :, COL_MG * D_MODEL + half * INPROJ_TN:COL_MG * D_MODEL + (half + 1) * INPROJ_TN] = (
                _sigmoid(acc).astype(BF16))


def _layer_spec(arr, l):
    nd = arr.ndim - 1
    assert nd >= 2
    return pl.BlockSpec((None,) + arr.shape[1:], lambda *_: (l,) + (0,) * nd)


def _mod_spec(mods, l):
    return pl.BlockSpec((None, 1) + mods.shape[2:], lambda b, i: (l, b, 0, 0))


def _in_projection(x, mods, l, norm_g, w_in, tables, q_g, k_g, conv_a_w, conv_a_b, conv_c_w, tk):
    bsz, s, d = x.shape
    tm = min(256, s)
    assert INPROJ_TN == 2 * KV_W and D_MODEL == 2 * INPROJ_TN and tm % GRID_W == 0 and tk % tm == 0
    nt = s // tm
    nkb = s // tk
    sub = tk // tm
    hbk = tm // HALO
    rcos, rsin, ccos, csin = tables
    row_spec = pl.BlockSpec((tm // GRID_W, 8, HEAD_DIM), lambda b, i: (i, 0, 0))
    col_spec = pl.BlockSpec((GRID_W, HEAD_DIM), lambda b, i: (0, 0))
    return pl.pallas_call(
        functools.partial(_inproj_kernel, nt=nt),
        grid=(bsz, nt),
        in_specs=[
            pl.BlockSpec((1, tm, d), lambda b, i: (b, i, 0)),
            pl.BlockSpec((1, HALO, d), lambda b, i: (b, jnp.maximum(i * hbk - 1, 0), 0)),
            pl.BlockSpec((1, HALO, d), lambda b, i: (b, jnp.minimum((i + 1) * hbk, s // HALO - 1), 0)),
            _mod_spec(mods, l),
            _layer_spec(norm_g, l),
            pl.BlockSpec((None, d, N_IN), lambda b, i: (l, 0, 0), pipeline_mode=pl.Buffered(1)),
            row_spec, row_spec, col_spec, col_spec, _layer_spec(q_g, l), _layer_spec(k_g, l),
            _layer_spec(conv_a_w, l), _layer_spec(conv_a_b, l), _layer_spec(conv_c_w, l),
        ],
        out_specs=[
            pl.BlockSpec((1, tm, N_STORED), lambda b, i: (b, i, 0)),
            pl.BlockSpec((1, N_HEADS, HEAD_DIM, tm), lambda b, i: (b, 0, 0, i)),
            pl.BlockSpec((1, N_KV, 1, tm, HEAD_DIM), lambda b, i: (b, 0, i // sub, i % sub, 0)),
            pl.BlockSpec((1, N_KV, 1, HEAD_DIM, tm), lambda b, i: (b, 0, i // sub, 0, i % sub)),
        ],
        out_shape=[
            jax.ShapeDtypeStruct((bsz, s, N_STORED), BF16),
            jax.ShapeDtypeStruct((bsz, N_HEADS, HEAD_DIM, s), BF16),
            jax.ShapeDtypeStruct((bsz, N_KV, nkb, tk, HEAD_DIM), BF16),
            jax.ShapeDtypeStruct((bsz, N_KV, nkb, HEAD_DIM, tk), BF16),
        ],
        compiler_params=_cparams(("parallel", "parallel")),
        name="norm_inproj",
    )(x, x, x, mods, norm_g, w_in, rcos, rsin, ccos, csin, q_g, k_g, conv_a_w, conv_a_b, conv_c_w)


SCORE_BOUND_NOSHIFT = 40.0


def _score_bound(q_g, k_g):
    q_scale = (HEAD_DIM ** -0.5) * LOG2E
    return HEAD_DIM * q_scale * jnp.max(jnp.abs(q_g), axis=-1) * jnp.max(jnp.abs(k_g), axis=-1)


KV_BLOCKS_PER_TRIP = 16


def _largest_divisor(n, cap):
    return max(d for d in range(1, cap + 1) if n % d == 0)


def _attn_kernel(flag_ref, qt_ref, k_ref, vt_ref, gb_ref, o_ref, m_ref, l_ref, acc_ref, s_ref, *, nkb, layer):
    tq = acc_ref.shape[-1]
    l_ref[...] = jnp.zeros(l_ref.shape, F32)
    acc_ref[...] = jnp.zeros(acc_ref.shape, F32)
    bounded = flag_ref[layer] != 0

    @pl.when(bounded)
    def _():
        def scores(j, h):
            return jnp.dot(k_ref[0, 0, j], qt_ref[0, h], preferred_element_type=F32)

        s_ref[...] = scores(0, 0)

        group = _largest_divisor(nkb, KV_BLOCKS_PER_TRIP)

        def kv_group(g, carry):
            s_cur = s_ref[...]
            for u in range(group):
                j = g * group + u
                vblk = vt_ref[0, 0, j]
                for h in range(GROUP):
                    s_next = scores(j, h + 1) if h + 1 < GROUP else scores(jnp.minimum(j + 1, nkb - 1), 0)
                    p = jnp.exp2(s_cur)
                    l_ref[h] += jnp.sum(p.reshape(-1, SUBLANES, tq), axis=0)
                    acc_ref[h] += jnp.dot(vblk, p.astype(BF16), preferred_element_type=F32)
                    s_cur = s_next
            s_ref[...] = s_cur
            return carry

        lax.fori_loop(0, nkb // group, kv_group, 0)

    @pl.when(jnp.logical_not(bounded))
    def _():
        m_ref[...] = jnp.full(m_ref.shape, -jnp.inf, F32)

        def kv_step(j, carry):
            kblk = k_ref[0, 0, j]
            vblk = vt_ref[0, 0, j]
            for h in range(GROUP):
                s = jnp.dot(kblk, qt_ref[0, h], preferred_element_type=F32)
                m_old = m_ref[h]
                m_new = jnp.maximum(m_old, jnp.max(s, axis=0, keepdims=True))
                alpha = jnp.exp2(m_old - m_new)
                p = jnp.exp2(s - m_new)
                l_ref[h] = alpha * l_ref[h] + jnp.sum(p.reshape(-1, SUBLANES, tq), axis=0)
                acc_ref[h] = alpha * acc_ref[h] + jnp.dot(vblk, p.astype(BF16), preferred_element_type=F32)
                m_ref[h] = m_new
            return carry

        lax.fori_loop(0, nkb, kv_step, 0)

    for h in range(GROUP):
        l = jnp.sum(l_ref[h], axis=0, keepdims=True)
        o = acc_ref[h] * (1.0 / l)
        gate = gb_ref[0, :, h * HEAD_DIM:(h + 1) * HEAD_DIM].astype(F32)
        o_ref[0, :, h * HEAD_DIM:(h + 1) * HEAD_DIM] = (o.T * gate).astype(BF16)


def _attention(qt, kk, vt, proj, bounded_flags, layer, tq):
    bsz, _, _, s = qt.shape
    nkb, tk = kk.shape[2], kk.shape[3]
    gw = GROUP * HEAD_DIM
    grid_spec = pltpu.PrefetchScalarGridSpec(
        num_scalar_prefetch=1,
        grid=(bsz, N_KV, s // tq),
        in_specs=[
            pl.BlockSpec((1, GROUP, HEAD_DIM, tq), lambda b, g, i, f: (b, g, 0, i)),
            pl.BlockSpec((1, 1, nkb, tk, HEAD_DIM), lambda b, g, i, f: (b, g, 0, 0, 0)),
            pl.BlockSpec((1, 1, nkb, HEAD_DIM, tk), lambda b, g, i, f: (b, g, 0, 0, 0)),
            pl.BlockSpec((1, tq, gw), lambda b, g, i, f: (b, i, COL_GB * D_MODEL // gw + g)),
        ],
        out_specs=pl.BlockSpec((1, tq, gw), lambda b, g, i, f: (b, i, g)),
        scratch_shapes=[
            pltpu.VMEM((GROUP, 1, tq), F32),
            pltpu.VMEM((GROUP, SUBLANES, tq), F32),
            pltpu.VMEM((GROUP, HEAD_DIM, tq), F32),
            pltpu.VMEM((tk, tq), F32),
        ],
    )
    return pl.pallas_call(
        functools.partial(_attn_kernel, nkb=nkb, layer=layer),
        grid_spec=grid_spec,
        out_shape=jax.ShapeDtypeStruct((bsz, s, D_MODEL), BF16),
        compiler_params=_cparams(("parallel", "parallel", "arbitrary")),
        name="gqa_attention",
    )(bounded_flags, qt, kk, vt, proj)


def _lru_gates(x_ref, wr_ref, wi_ref, d, ua_ref, rpre_ref, ipre_ref):
    ub = x_ref[0]
    ua_ref[d] = ub.astype(F32)
    for n in range(RNN_BLOCKS):
        cols = slice(n * RNN_BW, (n + 1) * RNN_BW)
        rpre_ref[d, :, cols] = jnp.dot(ub[:, cols], wr_ref[d, n], preferred_element_type=F32)
        ipre_ref[d, :, cols] = jnp.dot(ub[:, cols], wi_ref[d, n], preferred_element_type=F32)


def _lru_coefficients(ua, r_half, i_half, b_r_half, b_i_half, c_half):
    y = c_half + c_half * jnp.tanh(r_half + b_r_half)
    ua_half = 0.5 * ua
    gated = ua_half + ua_half * jnp.tanh(i_half + b_i_half)
    a = jnp.exp2(y * (-LOG2E))
    x = jnp.tanh(y) * (1.0 + a * a)
    root = jnp.where(x > 0.0, x * lax.rsqrt(x), x)
    return a, root * gated


N_SLABS = D_MODEL // LANES
assert N_SLABS == SUBLANES


def _slab_pitch(t):
    return t + SUBLANES if (t // SUBLANES) % 2 == 0 else t


def _lru_kernel(xf_ref, xb_ref, wr_ref, br_ref, wi_ref, bi_ref, lam_ref, hf_ref, hb_ref, cf_ref, cbk_ref,
                sf_ref, sb_ref, ua_ref, rpre_ref, ipre_ref, chan_ref, *coef_refs, nt):
    i = pl.program_id(1)
    tm = hf_ref.shape[1]
    pitch = _slab_pitch(tm)
    coef_sets = (coef_refs[:4], coef_refs[4:])

    @pl.when(i == 0)
    def _():
        cf_ref[...] = jnp.zeros(cf_ref.shape, F32)
        cbk_ref[...] = jnp.zeros(cbk_ref.shape, F32)
        for ref in coef_sets[1]:
            ref[...] = jnp.zeros(ref.shape, F32)

    _lru_gates(xf_ref, wr_ref, wi_ref, 0, ua_ref, rpre_ref, ipre_ref)
    _lru_gates(xb_ref, wr_ref, wi_ref, 1, ua_ref, rpre_ref, ipre_ref)
    nl = -lam_ref[...]
    c_half = (0.5 * LRU_C) * (jnp.maximum(nl, 0.0) + jnp.log1p(jnp.exp(-jnp.abs(nl))))
    for d in range(2):
        for n, row in enumerate((0.5 * br_ref[d:d + 1, :], 0.5 * bi_ref[d:d + 1, :], c_half[d:d + 1, :])):
            chan_ref[3 * d + n] = jnp.broadcast_to(row, chan_ref.shape[1:])

    def step(cur, prev):
        af_ref, uf_ref, ab_ref, ub_ref = prev

        def rows8(g, carry):
            hf, hb = carry
            r0 = pl.multiple_of(g * SUBLANES, SUBLANES)
            rows = pl.ds(r0, SUBLANES)
            for d in range(2):
                a, u = _lru_coefficients(ua_ref[d, rows, :], rpre_ref[d, rows, :], ipre_ref[d, rows, :],
                                         chan_ref[3 * d], chan_ref[3 * d + 1], chan_ref[3 * d + 2])
                for c in range(N_SLABS):
                    dst = pl.ds(c * pitch + r0, SUBLANES)
                    cur[2 * d][dst, :] = a[:, c * LANES:(c + 1) * LANES]
                    cur[2 * d + 1][dst, :] = u[:, c * LANES:(c + 1) * LANES]
            for j in range(SUBLANES):
                k = r0 + j
                tf = pl.ds(k, N_SLABS, stride=pitch)
                tb = pl.ds(tm - 1 - k, N_SLABS, stride=pitch)
                hf = af_ref[tf, :] * hf + uf_ref[tf, :]
                sf_ref[tf, :] = hf
                hb = ab_ref[tb, :] * hb + ub_ref[tb, :]
                sb_ref[tb, :] = hb
            return hf, hb

        hf, hb = lax.fori_loop(0, tm // SUBLANES, rows8, (cf_ref[...], cbk_ref[...]), unroll=4)
        keep = i > 0
        cf_ref[...] = jnp.where(keep, hf, cf_ref[...])
        cbk_ref[...] = jnp.where(keep, hb, cbk_ref[...])
        for s_ref, out_ref in ((sf_ref, hf_ref), (sb_ref, hb_ref)):
            for c in range(N_SLABS):
                out_ref[0, :, c * LANES:(c + 1) * LANES] = s_ref[pl.ds(c * pitch, tm), :].astype(out_ref.dtype)

    @pl.when(i % 2 == 0)
    def _():
        step(coef_sets[0], coef_sets[1])

    @pl.when(i % 2 == 1)
    def _():
        step(coef_sets[1], coef_sets[0])


def _rg_lru(proj, l, w_rg, b_rg, w_ig, b_ig, lam):
    bsz, s, _ = proj.shape
    d = D_MODEL
    tm = min(256, s)
    nt = s // tm
    out_sds = jax.ShapeDtypeStruct((bsz, s, d), BF16)
    return pl.pallas_call(
        functools.partial(_lru_kernel, nt=nt),
        grid=(bsz, nt + 1),
        in_specs=[
            pl.BlockSpec((1, tm, d), lambda b, i: (b, jnp.minimum(i, nt - 1), COL_UA)),
            pl.BlockSpec((1, tm, d), lambda b, i: (b, nt - 1 - jnp.minimum(i, nt - 1), COL_UA)),
            _layer_spec(w_rg, l), _layer_spec(b_rg, l), _layer_spec(w_ig, l), _layer_spec(b_ig, l),
            _layer_spec(lam, l),
        ],
        out_specs=[
            pl.BlockSpec((1, tm, d), lambda b, i: (b, jnp.maximum(i - 1, 0), 0)),
            pl.BlockSpec((1, tm, d), lambda b, i: (b, nt - 1 - jnp.maximum(i - 1, 0), 0)),
        ],
        out_shape=[out_sds, out_sds],
        scratch_shapes=[pltpu.VMEM((N_SLABS, LANES), F32)] * 2
        + [pltpu.VMEM((N_SLABS * _slab_pitch(tm), LANES), F32)] * 2
        + [pltpu.VMEM((2, tm, d), F32)] * 3
        + [pltpu.VMEM((6, 8, d), F32)]
        + [pltpu.VMEM((N_SLABS * _slab_pitch(tm), LANES), F32)] * 8,
        compiler_params=_cparams(("parallel", "arbitrary")),
        name="rg_lru",
    )(proj, proj, w_rg, b_rg, w_ig, b_ig, lam)


def _merge_kernel(hf_ref, hb_ref, ga_ref, yb_ref, yc_ref, mg0_ref, mg1_ref, mg2_ref, x_ref, mod_ref, wb_ref,
                  wo_ref, o_ref):
    merged = mg1_ref[0].astype(F32) * jnp.dot(yb_ref[0], wb_ref[1], preferred_element_type=F32)
    merged += mg2_ref[0].astype(F32) * jnp.dot(yc_ref[0], wb_ref[2], preferred_element_type=F32)
    ya = (hf_ref[0].astype(F32) + hb_ref[0].astype(F32)) * ga_ref[0].astype(F32)
    merged += mg0_ref[0].astype(F32) * jnp.dot(ya.astype(BF16), wb_ref[0], preferred_element_type=F32)
    out = jnp.dot(merged.astype(BF16), wo_ref[...], preferred_element_type=F32)
    o_ref[0] = x_ref[0] + mod_ref[0, 2:3, :] * out


def _merge(hf, hb, yb, proj, x, mods, l, w_branch, w_out):
    bsz, s, d = x.shape
    tm = min(512, s)

    def col(c):
        return pl.BlockSpec((1, tm, d), lambda b, i: (b, i, c))

    return pl.pallas_call(
        _merge_kernel,
        grid=(bsz, s // tm),
        in_specs=[
            col(0), col(0), col(COL_GA), col(0), col(COL_YC), col(COL_MG), col(COL_MG + 1), col(COL_MG + 2),
            col(0),
            _mod_spec(mods, l), _layer_spec(w_branch, l), _layer_spec(w_out, l),
        ],
        out_specs=pl.BlockSpec((1, tm, d), lambda b, i: (b, i, 0)),
        out_shape=jax.ShapeDtypeStruct((bsz, s, d), F32),
        compiler_params=_cparams(("parallel", "parallel")),
        name="merge_outproj",
    )(hf, hb, proj, yb, proj, proj, proj, proj, x, mods, w_branch, w_out)


def _attn_tiles(s):
    tq = min(512, s)
    tk = min(512, s)
    return tq, tk


def kernel(x, c, w_ada, b_ada, norm_g, w_in, conv_a_w, conv_a_b, w_rg, b_rg, w_ig, b_ig, lru_lam, q_norm_g,
           k_norm_g, conv_c_w, w_branch, w_out):
    bsz, s, d = x.shape
    depth = w_ada.shape[0]
    tq, tk = _attn_tiles(s)
    tables = _rope_tables(s)
    mods = _modulation(c, w_ada, b_ada)
    w_in_bf = w_in.astype(BF16)
    w_rg_half = (0.5 * w_rg).astype(BF16)
    w_ig_half = (0.5 * w_ig).astype(BF16)
    w_branch_bf = w_branch.astype(BF16)
    w_out_bf = w_out.astype(BF16)
    bounded = (_score_bound(q_norm_g, k_norm_g) <= SCORE_BOUND_NOSHIFT).astype(jnp.int32)
    rows = lambda p: p.reshape(depth, 1, p.shape[-1])
    norm_g, conv_a_b, q_norm_g, k_norm_g = rows(norm_g), rows(conv_a_b), rows(q_norm_g), rows(k_norm_g)
    for l in range(depth):
        proj, qt, kk, vt = _in_projection(x, mods, l, norm_g, w_in_bf, tables, q_norm_g, k_norm_g, conv_a_w,
                                          conv_a_b, conv_c_w, tk)
        hf, hb = _rg_lru(proj, l, w_rg_half, b_rg, w_ig_half, b_ig, lru_lam)
        yb = _attention(qt, kk, vt, proj, bounded, l, tq)
        x = _merge(hf, hb, yb, proj, x, mods, l, w_branch_bf, w_out_bf)
    return x
```

```python
import functools

import numpy as np
import jax
import jax.numpy as jnp
from jax import lax
from jax.experimental import pallas as pl
from jax.experimental.pallas import tpu as pltpu

D_MODEL = 1024
EPS = 1e-6
GRID_W = 64
N_HEADS = 8
N_KV = 2
GROUP = N_HEADS // N_KV
HEAD_DIM = 128
RNN_BLOCKS = 4
RNN_BW = D_MODEL // RNN_BLOCKS
LRU_C = 8.0
ROPE_THETA = 10000.0
N_BRANCH = 3
LOG2E = 1.4426950408889634

KV_W = N_KV * HEAD_DIM
QKV_W = D_MODEL + 2 * KV_W
N_IN = 11 * D_MODEL + 2 * KV_W
INPROJ_TN = 512
INPROJ_ROWS = 256

COL_UA, COL_GA, COL_GB, COL_YC, COL_MG = 0, 1, 2, 3, 4
N_STORED = (COL_MG + N_BRANCH) * D_MODEL

_SRC = dict(xa=0, ga=D_MODEL, q=2 * D_MODEL, kv=3 * D_MODEL, gb=3 * D_MODEL + 2 * KV_W)
_SRC.update(xc=_SRC["gb"] + D_MODEL, bc=_SRC["gb"] + 2 * D_MODEL, cc=_SRC["gb"] + 3 * D_MODEL,
            gc=_SRC["gb"] + 4 * D_MODEL, mg=_SRC["gb"] + 5 * D_MODEL)
_TILE_ORDER = (("q", 0), ("mg", 0), ("q", 1), ("mg", 1), ("kv", 0), ("mg", 2), ("xa", 0), ("mg", 3),
               ("xa", 1), ("mg", 4), ("xc", 0), ("cc", 0), ("ga", 0), ("bc", 0), ("gc", 0), ("mg", 5),
               ("xc", 1), ("cc", 1), ("ga", 1), ("bc", 1), ("gc", 1), ("gb", 0), ("gb", 1))
INPROJ_TILES = tuple((k, _SRC[k] + t * INPROJ_TN, t) for k, t in _TILE_ORDER)
assert len(INPROJ_TILES) * INPROJ_TN == N_IN

SUBLANES = 8
LANES = 128
ROPE_SPAN = HEAD_DIM // 2
ROPE_PAIR = ROPE_SPAN // 2
HALO = 16
VMEM_LIMIT = 56 * 1024 * 1024

F32 = jnp.float32
BF16 = jnp.bfloat16


def _sigmoid(x):
    return 1.0 / (1.0 + jnp.exp2(x * (-LOG2E)))


def _silu(x):
    return x * _sigmoid(x)


def _cparams(sem):
    return pltpu.CompilerParams(dimension_semantics=sem, vmem_limit_bytes=VMEM_LIMIT)


def _mod_kernel(c_ref, w_ref, b_ref, o_ref):
    acc = jnp.dot(c_ref[...], w_ref[0], preferred_element_type=F32, precision=lax.Precision.HIGHEST)
    o_ref[0] = acc + b_ref[0]


def _modulation(c, w_ada, b_ada):
    depth, d, n = w_ada.shape
    bsz = c.shape[0]
    rows = SUBLANES
    assert bsz <= rows
    c8 = jnp.zeros((rows, d), F32).at[:bsz].set(c)
    tn = 1024
    out = pl.pallas_call(
        _mod_kernel,
        grid=(depth, n // tn),
        in_specs=[
            pl.BlockSpec((rows, d), lambda l, j: (0, 0)),
            pl.BlockSpec((1, d, tn), lambda l, j: (l, 0, j)),
            pl.BlockSpec((1, 1, tn), lambda l, j: (l, 0, j)),
        ],
        out_specs=pl.BlockSpec((1, rows, tn), lambda l, j: (l, 0, j)),
        out_shape=jax.ShapeDtypeStruct((depth, rows, n), F32),
        compiler_params=_cparams(("parallel", "parallel")),
        name="adaln_mod",
    )(c8, w_ada, b_ada.reshape(depth, 1, n))
    return out[:, :bsz].reshape(depth, bsz, 3, d)


def _rope_tables(s):
    half = HEAD_DIM // 2
    inv = ROPE_THETA ** (-np.arange(0, half, 2, dtype=np.float64) / half)
    rows = s // GRID_W
    ang_r = np.arange(rows, dtype=np.float64)[:, None] * inv
    ang_c = np.arange(GRID_W, dtype=np.float64)[:, None] * inv
    zr = np.zeros((rows, half))
    zc = np.zeros((GRID_W, half))
    row_cos = np.concatenate([np.cos(ang_r), np.cos(ang_r), zr], -1)
    row_sin = np.concatenate([-np.sin(ang_r), np.sin(ang_r), zr], -1)
    col_cos = np.concatenate([zc, np.cos(ang_c), np.cos(ang_c)], -1)
    col_sin = np.concatenate([zc, -np.sin(ang_c), np.sin(ang_c)], -1)
    bcast = lambda t: jnp.broadcast_to(jnp.asarray(t, F32)[:, None, :], (rows, SUBLANES, HEAD_DIM))
    return bcast(row_cos), bcast(row_sin), jnp.asarray(col_cos, F32), jnp.asarray(col_sin, F32)


def _norm_rope(xh, g, cos_t, sin_t, lane_lo):
    ms = jnp.mean(xh * xh, axis=-1, keepdims=True)
    y = xh * lax.rsqrt(ms + EPS) * g
    partner = jnp.where(lane_lo, pltpu.roll(y, HEAD_DIM - ROPE_PAIR, axis=1), pltpu.roll(y, ROPE_PAIR, axis=1))
    return y * cos_t + partner * sin_t


def _shift_rows(tile, halo, shift):
    t = tile.shape[0]
    g = SUBLANES
    row = lax.broadcasted_iota(jnp.int32, (g, tile.shape[1]), 0)
    if shift > 0:
        rolled = pltpu.roll(tile, shift, axis=0)
        edge = pltpu.roll(halo, shift, axis=0)[0:g]
        fixed = jnp.where(row < shift, edge, rolled[0:g])
        return jnp.concatenate([fixed, rolled[g:]], axis=0)
    k = -shift
    rolled = pltpu.roll(tile, t - k, axis=0)
    edge = pltpu.roll(halo, HALO - k, axis=0)[HALO - g:HALO]
    fixed = jnp.where(row >= g - k, edge, rolled[t - g:t])
    return jnp.concatenate([rolled[:t - g], fixed], axis=0)


def _inproj_kernel(x_ref, xp_ref, xn_ref, mod_ref, g_ref, w_ref, rcos_ref, rsin_ref, ccos_ref, csin_ref, qg_ref,
                   kg_ref, caw_ref, cab_ref, ccw_ref, o_ref, qt_ref, k_ref, vt_ref, *, nt):
    i = pl.program_id(1)
    tm = x_ref.shape[1]
    ts = min(INPROJ_ROWS, tm)
    xv = jnp.concatenate([xp_ref[0], x_ref[0], xn_ref[0]], axis=0)
    ms = jnp.mean(xv * xv, axis=-1, keepdims=True)
    y = xv * lax.rsqrt(ms + EPS) * g_ref[...]
    shift = mod_ref[0, 0:1, :]
    scale = mod_ref[0, 1:2, :]
    h_all = (y * (1.0 + scale) + shift).astype(BF16)

    reps = GRID_W // SUBLANES
    cos_all = jnp.concatenate([jnp.tile(rcos_ref[r], (reps, 1)) + ccos_ref[...] for r in range(tm // GRID_W)], 0)
    sin_all = jnp.concatenate([jnp.tile(rsin_ref[r], (reps, 1)) + csin_ref[...] for r in range(tm // GRID_W)], 0)
    lane = lax.broadcasted_iota(jnp.int32, (ts, HEAD_DIM), 1)
    lane_lo = (lane % ROPE_SPAN) < ROPE_PAIR
    q_scale = (HEAD_DIM ** -0.5) * LOG2E

    n_sub = tm // ts
    work = [(s, j) for s in range(n_sub) for j in range(len(INPROJ_TILES))]

    def project(s, j):
        ext = INPROJ_TILES[j][0] in ("xa", "xc", "cc")
        rows = slice(s * ts, s * ts + ts + 2 * HALO) if ext else slice(s * ts + HALO, s * ts + HALO + ts)
        src = INPROJ_TILES[j][1]
        return jnp.dot(h_all[rows], w_ref[:, src:src + INPROJ_TN], preferred_element_type=F32)

    held = {}
    acc_next = project(*work[0])
    for n, (s, j) in enumerate(work):
        kind, _, half = INPROJ_TILES[j]
        lanes = slice(half * INPROJ_TN, (half + 1) * INPROJ_TN)
        rows = slice(s * ts, (s + 1) * ts)
        cos_t, sin_t = cos_all[rows], sin_all[rows]
        acc = acc_next
        if n + 1 < len(work):
            acc_next = project(*work[n + 1])

        def with_halos(acc_ext):
            prev, nxt = acc_ext[0:HALO], acc_ext[HALO + ts:2 * HALO + ts]
            if s == 0:
                prev = jnp.where(i == 0, 0.0, prev)
            if s == n_sub - 1:
                nxt = jnp.where(i == nt - 1, 0.0, nxt)
            return acc_ext[HALO:HALO + ts], prev, nxt

        def store(col, val):
            o_ref[0, rows, col * D_MODEL + half * INPROJ_TN:col * D_MODEL + (half + 1) * INPROJ_TN] = val.astype(BF16)

        if kind == "q":
            for hh in range(INPROJ_TN // HEAD_DIM):
                qh = _norm_rope(acc[:, hh * HEAD_DIM:(hh + 1) * HEAD_DIM], qg_ref[...], cos_t, sin_t, lane_lo)
                qt_ref[0, half * (INPROJ_TN // HEAD_DIM) + hh, :, rows] = (qh * q_scale).T.astype(BF16)
        elif kind == "kv":
            for hh in range(N_KV):
                kh = acc[:, hh * HEAD_DIM:(hh + 1) * HEAD_DIM]
                k_ref[0, hh, 0, rows, :] = _norm_rope(kh, kg_ref[...], cos_t, sin_t, lane_lo).astype(BF16)
                vt_ref[0, hh, 0, :, rows] = acc[:, KV_W + hh * HEAD_DIM:KV_W + (hh + 1) * HEAD_DIM].T.astype(BF16)
        elif kind == "xa":
            tile, prev, nxt = with_halos(acc)
            cw = caw_ref[:, lanes]
            store(COL_UA, cw[0:1] * _shift_rows(tile, prev, 2) + cw[1:2] * _shift_rows(tile, prev, 1)
                  + cw[2:3] * tile + cw[3:4] * _shift_rows(tile, nxt, -1) + cab_ref[:, lanes])
        elif kind == "xc":
            held["xc"] = acc
        elif kind == "cc":
            tile, prev, nxt = with_halos(acc * held.pop("xc"))
            cw = ccw_ref[:, lanes]
            held["conv"] = cw[0:1] * _shift_rows(tile, prev, 1) + cw[1:2] * tile + cw[2:3] * _shift_rows(tile, nxt, -1)
        elif kind == "bc":
            held["conv"] = acc * held["conv"]
        elif kind == "gc":
            store(COL_YC, held.pop("conv") * _silu(acc))
        elif kind == "ga":
            store(COL_GA, _silu(acc))
        elif kind == "gb":
            store(COL_GB, _silu(acc))
        else:
            o_ref[0, rows, COL_MG * D_MODEL + half * INPROJ_TN:COL_MG * D_MODEL + (half + 1) * INPROJ_TN] = (
                _sigmoid(acc).astype(BF16))


def _layer_spec(arr, l):
    nd = arr.ndim - 1
    assert nd >= 2
    return pl.BlockSpec((None,) + arr.shape[1:], lambda *_: (l,) + (0,) * nd)


def _mod_spec(mods, l):
    return pl.BlockSpec((None, 1) + mods.shape[2:], lambda b, i: (l, b, 0, 0))


def _in_projection(x, mods, l, norm_g, w_in, tables, q_g, k_g, conv_a_w, conv_a_b, conv_c_w, tk):
    bsz, s, d = x.shape
    tm = min(2 * INPROJ_ROWS, s)
    assert INPROJ_TN == 2 * KV_W and D_MODEL == 2 * INPROJ_TN and tm % GRID_W == 0 and tk % tm == 0
    assert tm % min(INPROJ_ROWS, tm) == 0
    nt = s // tm
    nkb = s // tk
    sub = tk // tm
    hbk = tm // HALO
    rcos, rsin, ccos, csin = tables
    row_spec = pl.BlockSpec((tm // GRID_W, 8, HEAD_DIM), lambda b, i: (i, 0, 0))
    col_spec = pl.BlockSpec((GRID_W, HEAD_DIM), lambda b, i: (0, 0))
    return pl.pallas_call(
        functools.partial(_inproj_kernel, nt=nt),
        grid=(bsz, nt),
        in_specs=[
            pl.BlockSpec((1, tm, d), lambda b, i: (b, i, 0)),
            pl.BlockSpec((1, HALO, d), lambda b, i: (b, jnp.maximum(i * hbk - 1, 0), 0)),
            pl.BlockSpec((1, HALO, d), lambda b, i: (b, jnp.minimum((i + 1) * hbk, s // HALO - 1), 0)),
            _mod_spec(mods, l),
            _layer_spec(norm_g, l),
            pl.BlockSpec((None, d, N_IN), lambda b, i: (l, 0, 0), pipeline_mode=pl.Buffered(1)),
            row_spec, row_spec, col_spec, col_spec, _layer_spec(q_g, l), _layer_spec(k_g, l),
            _layer_spec(conv_a_w, l), _layer_spec(conv_a_b, l), _layer_spec(conv_c_w, l),
        ],
        out_specs=[
            pl.BlockSpec((1, tm, N_STORED), lambda b, i: (b, i, 0)),
            pl.BlockSpec((1, N_HEADS, HEAD_DIM, tm), lambda b, i: (b, 0, 0, i)),
            pl.BlockSpec((1, N_KV, 1, tm, HEAD_DIM), lambda b, i: (b, 0, i // sub, i % sub, 0)),
            pl.BlockSpec((1, N_KV, 1, HEAD_DIM, tm), lambda b, i: (b, 0, i // sub, 0, i % sub)),
        ],
        out_shape=[
            jax.ShapeDtypeStruct((bsz, s, N_STORED), BF16),
            jax.ShapeDtypeStruct((bsz, N_HEADS, HEAD_DIM, s), BF16),
            jax.ShapeDtypeStruct((bsz, N_KV, nkb, tk, HEAD_DIM), BF16),
            jax.ShapeDtypeStruct((bsz, N_KV, nkb, HEAD_DIM, tk), BF16),
        ],
        compiler_params=_cparams(("parallel", "parallel")),
        name="norm_inproj",
    )(x, x, x, mods, norm_g, w_in, rcos, rsin, ccos, csin, q_g, k_g, conv_a_w, conv_a_b, conv_c_w)


SCORE_BOUND_NOSHIFT = 40.0


def _score_bound(q_g, k_g):
    q_scale = (HEAD_DIM ** -0.5) * LOG2E
    return HEAD_DIM * q_scale * jnp.max(jnp.abs(q_g), axis=-1) * jnp.max(jnp.abs(k_g), axis=-1)


KV_BLOCKS_PER_TRIP = 16


def _largest_divisor(n, cap):
    return max(d for d in range(1, cap + 1) if n % d == 0)


def _attn_kernel(flag_ref, qt_ref, k_ref, vt_ref, gb_ref, o_ref, m_ref, l_ref, acc_ref, s_ref, *, nkb, layer):
    tq = acc_ref.shape[-1]
    l_ref[...] = jnp.zeros(l_ref.shape, F32)
    acc_ref[...] = jnp.zeros(acc_ref.shape, F32)
    bounded = flag_ref[layer] != 0

    @pl.when(bounded)
    def _():
        def scores(j, h):
            return jnp.dot(k_ref[0, 0, j], qt_ref[0, h], preferred_element_type=F32)

        s_ref[...] = scores(0, 0)

        group = _largest_divisor(nkb, KV_BLOCKS_PER_TRIP)

        def kv_group(g, carry):
            s_cur = s_ref[...]
            for u in range(group):
                j = g * group + u
                vblk = vt_ref[0, 0, j]
                for h in range(GROUP):
                    s_next = scores(j, h + 1) if h + 1 < GROUP else scores(jnp.minimum(j + 1, nkb - 1), 0)
                    p = jnp.exp2(s_cur)
                    l_ref[h] += jnp.sum(p.reshape(-1, SUBLANES, tq), axis=0)
                    acc_ref[h] += jnp.dot(vblk, p.astype(BF16), preferred_element_type=F32)
                    s_cur = s_next
            s_ref[...] = s_cur
            return carry

        lax.fori_loop(0, nkb // group, kv_group, 0)

    @pl.when(jnp.logical_not(bounded))
    def _():
        m_ref[...] = jnp.full(m_ref.shape, -jnp.inf, F32)

        def kv_step(j, carry):
            kblk = k_ref[0, 0, j]
            vblk = vt_ref[0, 0, j]
            for h in range(GROUP):
                s = jnp.dot(kblk, qt_ref[0, h], preferred_element_type=F32)
                m_old = m_ref[h]
                m_new = jnp.maximum(m_old, jnp.max(s, axis=0, keepdims=True))
                alpha = jnp.exp2(m_old - m_new)
                p = jnp.exp2(s - m_new)
                l_ref[h] = alpha * l_ref[h] + jnp.sum(p.reshape(-1, SUBLANES, tq), axis=0)
                acc_ref[h] = alpha * acc_ref[h] + jnp.dot(vblk, p.astype(BF16), preferred_element_type=F32)
                m_ref[h] = m_new
            return carry

        lax.fori_loop(0, nkb, kv_step, 0)

    for h in range(GROUP):
        l = jnp.sum(l_ref[h], axis=0, keepdims=True)
        o = acc_ref[h] * (1.0 / l)
        gate = gb_ref[0, :, h * HEAD_DIM:(h + 1) * HEAD_DIM].astype(F32)
        o_ref[0, :, h * HEAD_DIM:(h + 1) * HEAD_DIM] = (o.T * gate).astype(BF16)


def _attention(qt, kk, vt, proj, bounded_flags, layer, tq):
    bsz, _, _, s = qt.shape
    nkb, tk = kk.shape[2], kk.shape[3]
    gw = GROUP * HEAD_DIM
    grid_spec = pltpu.PrefetchScalarGridSpec(
        num_scalar_prefetch=1,
        grid=(bsz, N_KV, s // tq),
        in_specs=[
            pl.BlockSpec((1, GROUP, HEAD_DIM, tq), lambda b, g, i, f: (b, g, 0, i)),
            pl.BlockSpec((1, 1, nkb, tk, HEAD_DIM), lambda b, g, i, f: (b, g, 0, 0, 0)),
            pl.BlockSpec((1, 1, nkb, HEAD_DIM, tk), lambda b, g, i, f: (b, g, 0, 0, 0)),
            pl.BlockSpec((1, tq, gw), lambda b, g, i, f: (b, i, COL_GB * D_MODEL // gw + g)),
        ],
        out_specs=pl.BlockSpec((1, tq, gw), lambda b, g, i, f: (b, i, g)),
        scratch_shapes=[
            pltpu.VMEM((GROUP, 1, tq), F32),
            pltpu.VMEM((GROUP, SUBLANES, tq), F32),
            pltpu.VMEM((GROUP, HEAD_DIM, tq), F32),
            pltpu.VMEM((tk, tq), F32),
        ],
    )
    return pl.pallas_call(
        functools.partial(_attn_kernel, nkb=nkb, layer=layer),
        grid_spec=grid_spec,
        out_shape=jax.ShapeDtypeStruct((bsz, s, D_MODEL), BF16),
        compiler_params=_cparams(("parallel", "parallel", "arbitrary")),
        name="gqa_attention",
    )(bounded_flags, qt, kk, vt, proj)


def _lru_gates(x_ref, wr_ref, wi_ref, d, ua_ref, rpre_ref, ipre_ref):
    ub = x_ref[0]
    ua_ref[d] = ub.astype(F32)
    for n in range(RNN_BLOCKS):
        cols = slice(n * RNN_BW, (n + 1) * RNN_BW)
        rpre_ref[d, :, cols] = jnp.dot(ub[:, cols], wr_ref[d, n], preferred_element_type=F32)
        ipre_ref[d, :, cols] = jnp.dot(ub[:, cols], wi_ref[d, n], preferred_element_type=F32)


def _lru_coefficients(ua, r_half, i_half, b_r_half, b_i_half, c_half):
    y = c_half + c_half * jnp.tanh(r_half + b_r_half)
    ua_half = 0.5 * ua
    gated = ua_half + ua_half * jnp.tanh(i_half + b_i_half)
    a = jnp.exp2(y * (-LOG2E))
    x = jnp.tanh(y) * (1.0 + a * a)
    root = jnp.where(x > 0.0, x * lax.rsqrt(x), x)
    return a, root * gated


N_SLABS = D_MODEL // LANES
assert N_SLABS == SUBLANES


def _slab_pitch(t):
    return t + SUBLANES if (t // SUBLANES) % 2 == 0 else t


def _lru_kernel(xf_ref, xb_ref, wr_ref, br_ref, wi_ref, bi_ref, lam_ref, hf_ref, hb_ref, cf_ref, cbk_ref,
                sf_ref, sb_ref, ua_ref, rpre_ref, ipre_ref, chan_ref, *coef_refs, nt):
    i = pl.program_id(1)
    tm = hf_ref.shape[1]
    pitch = _slab_pitch(tm)
    coef_sets = (coef_refs[:4], coef_refs[4:])

    @pl.when(i == 0)
    def _():
        cf_ref[...] = jnp.zeros(cf_ref.shape, F32)
        cbk_ref[...] = jnp.zeros(cbk_ref.shape, F32)
        for ref in coef_sets[1]:
            ref[...] = jnp.zeros(ref.shape, F32)

    _lru_gates(xf_ref, wr_ref, wi_ref, 0, ua_ref, rpre_ref, ipre_ref)
    ua_ref[1] = xb_ref[0].astype(F32)
    nl = -lam_ref[...]
    c_half = (0.5 * LRU_C) * (jnp.maximum(nl, 0.0) + jnp.log1p(jnp.exp(-jnp.abs(nl))))
    for d in range(2):
        for n, row in enumerate((0.5 * br_ref[d:d + 1, :], 0.5 * bi_ref[d:d + 1, :], c_half[d:d + 1, :])):
            chan_ref[3 * d + n] = jnp.broadcast_to(row, chan_ref.shape[1:])

    def step(cur, prev):
        af_ref, uf_ref, ab_ref, ub_ref = prev

        hf, hb = cf_ref[...], cbk_ref[...]
        n_it = tm // SUBLANES
        chain_steps = tm // (2 * n_it)
        ub_bwd = xb_ref[0]
        bwd_dots = [(n, gate) for n in range(RNN_BLOCKS) for gate in range(2)]
        dot_every = n_it // len(bwd_dots)
        k = 0
        for it in range(2 * n_it):
            d, r0 = divmod(it, n_it)
            r0 *= SUBLANES
            rows = slice(r0, r0 + SUBLANES)
            a, u = _lru_coefficients(ua_ref[d, rows, :], rpre_ref[d, rows, :], ipre_ref[d, rows, :],
                                     chan_ref[3 * d], chan_ref[3 * d + 1], chan_ref[3 * d + 2])
            for c in range(N_SLABS):
                dst = pl.ds(c * pitch + r0, SUBLANES)
                cur[2 * d][dst, :] = a[:, c * LANES:(c + 1) * LANES]
                cur[2 * d + 1][dst, :] = u[:, c * LANES:(c + 1) * LANES]
            for _ in range(chain_steps):
                tf = pl.ds(k, N_SLABS, stride=pitch)
                tb = pl.ds(tm - 1 - k, N_SLABS, stride=pitch)
                hf = af_ref[tf, :] * hf + uf_ref[tf, :]
                sf_ref[tf, :] = hf
                hb = ab_ref[tb, :] * hb + ub_ref[tb, :]
                sb_ref[tb, :] = hb
                k += 1
            if d == 0 and it % dot_every == 0 and bwd_dots:
                n, gate = bwd_dots.pop(0)
                cols = slice(n * RNN_BW, (n + 1) * RNN_BW)
                w_ref, pre_ref = ((wr_ref, rpre_ref), (wi_ref, ipre_ref))[gate]
                pre_ref[1, :, cols] = jnp.dot(ub_bwd[:, cols], w_ref[1, n], preferred_element_type=F32)
        assert not bwd_dots and k == tm
        keep = i > 0
        cf_ref[...] = jnp.where(keep, hf, cf_ref[...])
        cbk_ref[...] = jnp.where(keep, hb, cbk_ref[...])
        for s_ref, out_ref in ((sf_ref, hf_ref), (sb_ref, hb_ref)):
            for c in range(N_SLABS):
                out_ref[0, :, c * LANES:(c + 1) * LANES] = s_ref[pl.ds(c * pitch, tm), :].astype(out_ref.dtype)

    @pl.when(i % 2 == 0)
    def _():
        step(coef_sets[0], coef_sets[1])

    @pl.when(i % 2 == 1)
    def _():
        step(coef_sets[1], coef_sets[0])


def _rg_lru(proj, l, w_rg, b_rg, w_ig, b_ig, lam):
    bsz, s, _ = proj.shape
    d = D_MODEL
    tm = min(256, s)
    nt = s // tm
    out_sds = jax.ShapeDtypeStruct((bsz, s, d), BF16)
    return pl.pallas_call(
        functools.partial(_lru_kernel, nt=nt),
        grid=(bsz, nt + 1),
        in_specs=[
            pl.BlockSpec((1, tm, d), lambda b, i: (b, jnp.minimum(i, nt - 1), COL_UA)),
            pl.BlockSpec((1, tm, d), lambda b, i: (b, nt - 1 - jnp.minimum(i, nt - 1), COL_UA)),
            _layer_spec(w_rg, l), _layer_spec(b_rg, l), _layer_spec(w_ig, l), _layer_spec(b_ig, l),
            _layer_spec(lam, l),
        ],
        out_specs=[
            pl.BlockSpec((1, tm, d), lambda b, i: (b, jnp.maximum(i - 1, 0), 0)),
            pl.BlockSpec((1, tm, d), lambda b, i: (b, nt - 1 - jnp.maximum(i - 1, 0), 0)),
        ],
        out_shape=[out_sds, out_sds],
        scratch_shapes=[pltpu.VMEM((N_SLABS, LANES), F32)] * 2
        + [pltpu.VMEM((N_SLABS * _slab_pitch(tm), LANES), F32)] * 2
        + [pltpu.VMEM((2, tm, d), F32)] * 3
        + [pltpu.VMEM((6, 8, d), F32)]
        + [pltpu.VMEM((N_SLABS * _slab_pitch(tm), LANES), F32)] * 8,
        compiler_params=_cparams(("parallel", "arbitrary")),
        name="rg_lru",
    )(proj, proj, w_rg, b_rg, w_ig, b_ig, lam)


def _merge_kernel(hf_ref, hb_ref, ga_ref, yb_ref, yc_ref, mg0_ref, mg1_ref, mg2_ref, x_ref, mod_ref, wb_ref,
                  wo_ref, o_ref):
    merged = mg1_ref[0].astype(F32) * jnp.dot(yb_ref[0], wb_ref[1], preferred_element_type=F32)
    merged += mg2_ref[0].astype(F32) * jnp.dot(yc_ref[0], wb_ref[2], preferred_element_type=F32)
    ya = (hf_ref[0].astype(F32) + hb_ref[0].astype(F32)) * ga_ref[0].astype(F32)
    merged += mg0_ref[0].astype(F32) * jnp.dot(ya.astype(BF16), wb_ref[0], preferred_element_type=F32)
    out = jnp.dot(merged.astype(BF16), wo_ref[...], preferred_element_type=F32)
    o_ref[0] = x_ref[0] + mod_ref[0, 2:3, :] * out


def _merge(hf, hb, yb, proj, x, mods, l, w_branch, w_out):
    bsz, s, d = x.shape
    tm = min(512, s)

    def col(c):
        return pl.BlockSpec((1, tm, d), lambda b, i: (b, i, c))

    return pl.pallas_call(
        _merge_kernel,
        grid=(bsz, s // tm),
        in_specs=[
            col(0), col(0), col(COL_GA), col(0), col(COL_YC), col(COL_MG), col(COL_MG + 1), col(COL_MG + 2),
            col(0),
            _mod_spec(mods, l), _layer_spec(w_branch, l), _layer_spec(w_out, l),
        ],
        out_specs=pl.BlockSpec((1, tm, d), lambda b, i: (b, i, 0)),
        out_shape=jax.ShapeDtypeStruct((bsz, s, d), F32),
        compiler_params=_cparams(("parallel", "parallel")),
        name="merge_outproj",
    )(hf, hb, proj, yb, proj, proj, proj, proj, x, mods, w_branch, w_out)


def _attn_tiles(s):
    tq = min(512, s)
    tk = min(512, s)
    return tq, tk


def kernel(x, c, w_ada, b_ada, norm_g, w_in, conv_a_w, conv_a_b, w_rg, b_rg, w_ig, b_ig, lru_lam, q_norm_g,
           k_norm_g, conv_c_w, w_branch, w_out):
    bsz, s, d = x.shape
    depth = w_ada.shape[0]
    tq, tk = _attn_tiles(s)
    tables = _rope_tables(s)
    mods = _modulation(c, w_ada, b_ada)
    w_in_bf = w_in.astype(BF16)
    w_rg_half = (0.5 * w_rg).astype(BF16)
    w_ig_half = (0.5 * w_ig).astype(BF16)
    w_branch_bf = w_branch.astype(BF16)
    w_out_bf = w_out.astype(BF16)
    bounded = (_score_bound(q_norm_g, k_norm_g) <= SCORE_BOUND_NOSHIFT).astype(jnp.int32)
    rows = lambda p: p.reshape(depth, 1, p.shape[-1])
    norm_g, conv_a_b, q_norm_g, k_norm_g = rows(norm_g), rows(conv_a_b), rows(q_norm_g), rows(k_norm_g)
    for l in range(depth):
        proj, qt, kk, vt = _in_projection(x, mods, l, norm_g, w_in_bf, tables, q_norm_g, k_norm_g, conv_a_w,
                                          conv_a_b, conv_c_w, tk)
        hf, hb = _rg_lru(proj, l, w_rg_half, b_rg, w_ig_half, b_ig, lru_lam)
        yb = _attention(qt, kk, vt, proj, bounded, l, tq)
        x = _merge(hf, hb, yb, proj, x, mods, l, w_branch_bf, w_out_bf)
    return x
```

```python
import functools

import numpy as np
import jax
import jax.numpy as jnp
from jax import lax
from jax.experimental import pallas as pl
from jax.experimental.pallas import tpu as pltpu

D_MODEL = 1024
EPS = 1e-6
GRID_W = 64
N_HEADS = 8
N_KV = 2
GROUP = N_HEADS // N_KV
HEAD_DIM = 128
RNN_BLOCKS = 4
RNN_BW = D_MODEL // RNN_BLOCKS
LRU_C = 8.0
ROPE_THETA = 10000.0
N_BRANCH = 3
LOG2E = 1.4426950408889634

KV_W = N_KV * HEAD_DIM
QKV_W = D_MODEL + 2 * KV_W
N_IN = 11 * D_MODEL + 2 * KV_W
INPROJ_TN = 512
INPROJ_ROWS = 256

COL_UA, COL_GA, COL_GB, COL_YC, COL_MG = 0, 1, 2, 3, 4
N_STORED = (COL_MG + N_BRANCH) * D_MODEL

_SRC = dict(xa=0, ga=D_MODEL, q=2 * D_MODEL, kv=3 * D_MODEL, gb=3 * D_MODEL + 2 * KV_W)
_SRC.update(xc=_SRC["gb"] + D_MODEL, bc=_SRC["gb"] + 2 * D_MODEL, cc=_SRC["gb"] + 3 * D_MODEL,
            gc=_SRC["gb"] + 4 * D_MODEL, mg=_SRC["gb"] + 5 * D_MODEL)
_TILE_ORDER = (("q", 0), ("mg", 0), ("q", 1), ("mg", 1), ("kv", 0), ("mg", 2), ("xa", 0), ("mg", 3),
               ("xa", 1), ("mg", 4), ("xc", 0), ("cc", 0), ("ga", 0), ("bc", 0), ("gc", 0), ("mg", 5),
               ("xc", 1), ("cc", 1), ("ga", 1), ("bc", 1), ("gc", 1), ("gb", 0), ("gb", 1))
INPROJ_TILES = tuple((k, _SRC[k] + t * INPROJ_TN, t) for k, t in _TILE_ORDER)
assert len(INPROJ_TILES) * INPROJ_TN == N_IN

SUBLANES = 8
LANES = 128
ROPE_SPAN = HEAD_DIM // 2
ROPE_PAIR = ROPE_SPAN // 2
HALO = 16
VMEM_LIMIT = 56 * 1024 * 1024

F32 = jnp.float32
BF16 = jnp.bfloat16


def _sigmoid(x):
    return 1.0 / (1.0 + jnp.exp2(x * (-LOG2E)))


def _silu(x):
    return x * _sigmoid(x)


def _cparams(sem):
    return pltpu.CompilerParams(dimension_semantics=sem, vmem_limit_bytes=VMEM_LIMIT)


def _mod_kernel(c_ref, w_ref, b_ref, o_ref):
    acc = jnp.dot(c_ref[...], w_ref[0], preferred_element_type=F32, precision=lax.Precision.HIGHEST)
    o_ref[0] = acc + b_ref[0]


def _modulation(c, w_ada, b_ada):
    depth, d, n = w_ada.shape
    bsz = c.shape[0]
    rows = SUBLANES
    assert bsz <= rows
    c8 = jnp.zeros((rows, d), F32).at[:bsz].set(c)
    tn = 1024
    out = pl.pallas_call(
        _mod_kernel,
        grid=(depth, n // tn),
        in_specs=[
            pl.BlockSpec((rows, d), lambda l, j: (0, 0)),
            pl.BlockSpec((1, d, tn), lambda l, j: (l, 0, j)),
            pl.BlockSpec((1, 1, tn), lambda l, j: (l, 0, j)),
        ],
        out_specs=pl.BlockSpec((1, rows, tn), lambda l, j: (l, 0, j)),
        out_shape=jax.ShapeDtypeStruct((depth, rows, n), F32),
        compiler_params=_cparams(("parallel", "parallel")),
        name="adaln_mod",
    )(c8, w_ada, b_ada.reshape(depth, 1, n))
    return out[:, :bsz].reshape(depth, bsz, 3, d)


def _rope_tables(s):
    half = HEAD_DIM // 2
    inv = ROPE_THETA ** (-np.arange(0, half, 2, dtype=np.float64) / half)
    rows = s // GRID_W
    ang_r = np.arange(rows, dtype=np.float64)[:, None] * inv
    ang_c = np.arange(GRID_W, dtype=np.float64)[:, None] * inv
    zr = np.zeros((rows, half))
    zc = np.zeros((GRID_W, half))
    row_cos = np.concatenate([np.cos(ang_r), np.cos(ang_r), zr], -1)
    row_sin = np.concatenate([-np.sin(ang_r), np.sin(ang_r), zr], -1)
    col_cos = np.concatenate([zc, np.cos(ang_c), np.cos(ang_c)], -1)
    col_sin = np.concatenate([zc, -np.sin(ang_c), np.sin(ang_c)], -1)
    bcast = lambda t: jnp.broadcast_to(jnp.asarray(t, F32)[:, None, :], (rows, SUBLANES, HEAD_DIM))
    return bcast(row_cos), bcast(row_sin), jnp.asarray(col_cos, F32), jnp.asarray(col_sin, F32)


def _norm_rope(xh, g, cos_t, sin_t, lane_lo):
    ms = jnp.mean(xh * xh, axis=-1, keepdims=True)
    y = xh * lax.rsqrt(ms + EPS) * g
    partner = jnp.where(lane_lo, pltpu.roll(y, HEAD_DIM - ROPE_PAIR, axis=1), pltpu.roll(y, ROPE_PAIR, axis=1))
    return y * cos_t + partner * sin_t


def _shift_rows(tile, halo, shift):
    t = tile.shape[0]
    g = SUBLANES
    row = lax.broadcasted_iota(jnp.int32, (g, tile.shape[1]), 0)
    if shift > 0:
        rolled = pltpu.roll(tile, shift, axis=0)
        edge = pltpu.roll(halo, shift, axis=0)[0:g]
        fixed = jnp.where(row < shift, edge, rolled[0:g])
        return jnp.concatenate([fixed, rolled[g:]], axis=0)
    k = -shift
    rolled = pltpu.roll(tile, t - k, axis=0)
    edge = pltpu.roll(halo, HALO - k, axis=0)[HALO - g:HALO]
    fixed = jnp.where(row >= g - k, edge, rolled[t - g:t])
    return jnp.concatenate([rolled[:t - g], fixed], axis=0)


def _inproj_kernel(x_ref, xp_ref, xn_ref, mod_ref, g_ref, w_ref, rcos_ref, rsin_ref, ccos_ref, csin_ref, qg_ref,
                   kg_ref, caw_ref, cab_ref, ccw_ref, o_ref, qt_ref, k_ref, vt_ref, *, nt):
    i = pl.program_id(1)
    tm = x_ref.shape[1]
    ts = min(INPROJ_ROWS, tm)
    xv = jnp.concatenate([xp_ref[0], x_ref[0], xn_ref[0]], axis=0)
    ms = jnp.mean(xv * xv, axis=-1, keepdims=True)
    y = xv * lax.rsqrt(ms + EPS) * g_ref[...]
    shift = mod_ref[0, 0:1, :]
    scale = mod_ref[0, 1:2, :]
    h_all = (y * (1.0 + scale) + shift).astype(BF16)

    reps = GRID_W // SUBLANES
    cos_all = jnp.concatenate([jnp.tile(rcos_ref[r], (reps, 1)) + ccos_ref[...] for r in range(tm // GRID_W)], 0)
    sin_all = jnp.concatenate([jnp.tile(rsin_ref[r], (reps, 1)) + csin_ref[...] for r in range(tm // GRID_W)], 0)
    lane = lax.broadcasted_iota(jnp.int32, (ts, HEAD_DIM), 1)
    lane_lo = (lane % ROPE_SPAN) < ROPE_PAIR
    q_scale = (HEAD_DIM ** -0.5) * LOG2E

    n_sub = tm // ts
    work = [(s, j) for s in range(n_sub) for j in range(len(INPROJ_TILES))]

    def project(s, j):
        ext = INPROJ_TILES[j][0] in ("xa", "xc", "cc")
        rows = slice(s * ts, s * ts + ts + 2 * HALO) if ext else slice(s * ts + HALO, s * ts + HALO + ts)
        src = INPROJ_TILES[j][1]
        return jnp.dot(h_all[rows], w_ref[:, src:src + INPROJ_TN], preferred_element_type=F32)

    held = {}
    acc_next = project(*work[0])
    for n, (s, j) in enumerate(work):
        kind, _, half = INPROJ_TILES[j]
        lanes = slice(half * INPROJ_TN, (half + 1) * INPROJ_TN)
        rows = slice(s * ts, (s + 1) * ts)
        cos_t, sin_t = cos_all[rows], sin_all[rows]
        acc = acc_next
        if n + 1 < len(work):
            acc_next = project(*work[n + 1])

        def with_halos(acc_ext):
            prev, nxt = acc_ext[0:HALO], acc_ext[HALO + ts:2 * HALO + ts]
            if s == 0:
                prev = jnp.where(i == 0, 0.0, prev)
            if s == n_sub - 1:
                nxt = jnp.where(i == nt - 1, 0.0, nxt)
            return acc_ext[HALO:HALO + ts], prev, nxt

        def store(col, val):
            o_ref[0, rows, col * D_MODEL + half * INPROJ_TN:col * D_MODEL + (half + 1) * INPROJ_TN] = val.astype(BF16)

        if kind == "q":
            for hh in range(INPROJ_TN // HEAD_DIM):
                qh = _norm_rope(acc[:, hh * HEAD_DIM:(hh + 1) * HEAD_DIM], qg_ref[...], cos_t, sin_t, lane_lo)
                qt_ref[0, half * (INPROJ_TN // HEAD_DIM) + hh, :, rows] = (qh * q_scale).T.astype(BF16)
        elif kind == "kv":
            for hh in range(N_KV):
                kh = acc[:, hh * HEAD_DIM:(hh + 1) * HEAD_DIM]
                k_ref[0, hh, 0, rows, :] = _norm_rope(kh, kg_ref[...], cos_t, sin_t, lane_lo).astype(BF16)
                vt_ref[0, hh, 0, :, rows] = acc[:, KV_W + hh * HEAD_DIM:KV_W + (hh + 1) * HEAD_DIM].T.astype(BF16)
        elif kind == "xa":
            tile, prev, nxt = with_halos(acc)
            cw = caw_ref[:, lanes]
            store(COL_UA, cw[0:1] * _shift_rows(tile, prev, 2) + cw[1:2] * _shift_rows(tile, prev, 1)
                  + cw[2:3] * tile + cw[3:4] * _shift_rows(tile, nxt, -1) + cab_ref[:, lanes])
        elif kind == "xc":
            held["xc"] = acc
        elif kind == "cc":
            tile, prev, nxt = with_halos(acc * held.pop("xc"))
            cw = ccw_ref[:, lanes]
            held["conv"] = cw[0:1] * _shift_rows(tile, prev, 1) + cw[1:2] * tile + cw[2:3] * _shift_rows(tile, nxt, -1)
        elif kind == "bc":
            held["conv"] = acc * held["conv"]
        elif kind == "gc":
            store(COL_YC, held.pop("conv") * _silu(acc))
        elif kind == "ga":
            store(COL_GA, _silu(acc))
        elif kind == "gb":
            store(COL_GB, _silu(acc))
        else:
            o_ref[0, rows, COL_MG * D_MODEL + half * INPROJ_TN:COL_MG * D_MODEL + (half + 1) * INPROJ_TN] = (
                _sigmoid(acc).astype(BF16))


def _layer_spec(arr, l):
    nd = arr.ndim - 1
    assert nd >= 2
    return pl.BlockSpec((None,) + arr.shape[1:], lambda *_: (l,) + (0,) * nd)


def _mod_spec(mods, l):
    return pl.BlockSpec((None, 1) + mods.shape[2:], lambda b, i: (l, b, 0, 0))


def _in_projection(x, mods, l, norm_g, w_in, tables, q_g, k_g, conv_a_w, conv_a_b, conv_c_w, tk):
    bsz, s, d = x.shape
    tm = min(2 * INPROJ_ROWS, s)
    assert INPROJ_TN == 2 * KV_W and D_MODEL == 2 * INPROJ_TN and tm % GRID_W == 0 and tk % tm == 0
    assert tm % min(INPROJ_ROWS, tm) == 0
    nt = s // tm
    nkb = s // tk
    sub = tk // tm
    hbk = tm // HALO
    rcos, rsin, ccos, csin = tables
    row_spec = pl.BlockSpec((tm // GRID_W, 8, HEAD_DIM), lambda b, i: (i, 0, 0))
    col_spec = pl.BlockSpec((GRID_W, HEAD_DIM), lambda b, i: (0, 0))
    return pl.pallas_call(
        functools.partial(_inproj_kernel, nt=nt),
        grid=(bsz, nt),
        in_specs=[
            pl.BlockSpec((1, tm, d), lambda b, i: (b, i, 0)),
            pl.BlockSpec((1, HALO, d), lambda b, i: (b, jnp.maximum(i * hbk - 1, 0), 0)),
            pl.BlockSpec((1, HALO, d), lambda b, i: (b, jnp.minimum((i + 1) * hbk, s // HALO - 1), 0)),
            _mod_spec(mods, l),
            _layer_spec(norm_g, l),
            pl.BlockSpec((None, d, N_IN), lambda b, i: (l, 0, 0), pipeline_mode=pl.Buffered(1)),
            row_spec, row_spec, col_spec, col_spec, _layer_spec(q_g, l), _layer_spec(k_g, l),
            _layer_spec(conv_a_w, l), _layer_spec(conv_a_b, l), _layer_spec(conv_c_w, l),
        ],
        out_specs=[
            pl.BlockSpec((1, tm, N_STORED), lambda b, i: (b, i, 0)),
            pl.BlockSpec((1, N_HEADS, HEAD_DIM, tm), lambda b, i: (b, 0, 0, i)),
            pl.BlockSpec((1, N_KV, 1, tm, HEAD_DIM), lambda b, i: (b, 0, i // sub, i % sub, 0)),
            pl.BlockSpec((1, N_KV, 1, HEAD_DIM, tm), lambda b, i: (b, 0, i // sub, 0, i % sub)),
        ],
        out_shape=[
            jax.ShapeDtypeStruct((bsz, s, N_STORED), BF16),
            jax.ShapeDtypeStruct((bsz, N_HEADS, HEAD_DIM, s), BF16),
            jax.ShapeDtypeStruct((bsz, N_KV, nkb, tk, HEAD_DIM), BF16),
            jax.ShapeDtypeStruct((bsz, N_KV, nkb, HEAD_DIM, tk), BF16),
        ],
        compiler_params=_cparams(("parallel", "parallel")),
        name="norm_inproj",
    )(x, x, x, mods, norm_g, w_in, rcos, rsin, ccos, csin, q_g, k_g, conv_a_w, conv_a_b, conv_c_w)


SCORE_BOUND_NOSHIFT = 40.0


def _score_bound(q_g, k_g):
    q_scale = (HEAD_DIM ** -0.5) * LOG2E
    return HEAD_DIM * q_scale * jnp.max(jnp.abs(q_g), axis=-1) * jnp.max(jnp.abs(k_g), axis=-1)


KV_BLOCKS_PER_TRIP = 16


def _largest_divisor(n, cap):
    return max(d for d in range(1, cap + 1) if n % d == 0)


def _attn_kernel(flag_ref, qt_ref, k_ref, vt_ref, gb_ref, o_ref, m_ref, l_ref, acc_ref, s_ref, *, nkb, layer):
    tq = acc_ref.shape[-1]
    l_ref[...] = jnp.zeros(l_ref.shape, F32)
    acc_ref[...] = jnp.zeros(acc_ref.shape, F32)
    bounded = flag_ref[layer] != 0

    @pl.when(bounded)
    def _():
        def scores(j, h):
            return jnp.dot(k_ref[0, 0, j], qt_ref[0, h], preferred_element_type=F32)

        s_ref[...] = scores(0, 0)

        group = _largest_divisor(nkb, KV_BLOCKS_PER_TRIP)

        def kv_group(g, carry):
            s_cur = s_ref[...]
            for u in range(group):
                j = g * group + u
                vblk = vt_ref[0, 0, j]
                for h in range(GROUP):
                    s_next = scores(j, h + 1) if h + 1 < GROUP else scores(jnp.minimum(j + 1, nkb - 1), 0)
                    p = jnp.exp2(s_cur)
                    l_ref[h] += jnp.sum(p.reshape(-1, SUBLANES, tq), axis=0)
                    acc_ref[h] += jnp.dot(vblk, p.astype(BF16), preferred_element_type=F32)
                    s_cur = s_next
            s_ref[...] = s_cur
            return carry

        lax.fori_loop(0, nkb // group, kv_group, 0)

    @pl.when(jnp.logical_not(bounded))
    def _():
        m_ref[...] = jnp.full(m_ref.shape, -jnp.inf, F32)

        def kv_step(j, carry):
            kblk = k_ref[0, 0, j]
            vblk = vt_ref[0, 0, j]
            for h in range(GROUP):
                s = jnp.dot(kblk, qt_ref[0, h], preferred_element_type=F32)
                m_old = m_ref[h]
                m_new = jnp.maximum(m_old, jnp.max(s, axis=0, keepdims=True))
                alpha = jnp.exp2(m_old - m_new)
                p = jnp.exp2(s - m_new)
                l_ref[h] = alpha * l_ref[h] + jnp.sum(p.reshape(-1, SUBLANES, tq), axis=0)
                acc_ref[h] = alpha * acc_ref[h] + jnp.dot(vblk, p.astype(BF16), preferred_element_type=F32)
                m_ref[h] = m_new
            return carry

        lax.fori_loop(0, nkb, kv_step, 0)

    for h in range(GROUP):
        l = jnp.sum(l_ref[h], axis=0, keepdims=True)
        o = acc_ref[h] * (1.0 / l)
        gate = gb_ref[0, :, h * HEAD_DIM:(h + 1) * HEAD_DIM].astype(F32)
        o_ref[0, :, h * HEAD_DIM:(h + 1) * HEAD_DIM] = (o.T * gate).astype(BF16)


def _attention(qt, kk, vt, proj, bounded_flags, layer, tq):
    bsz, _, _, s = qt.shape
    nkb, tk = kk.shape[2], kk.shape[3]
    gw = GROUP * HEAD_DIM
    grid_spec = pltpu.PrefetchScalarGridSpec(
        num_scalar_prefetch=1,
        grid=(bsz, N_KV, s // tq),
        in_specs=[
            pl.BlockSpec((1, GROUP, HEAD_DIM, tq), lambda b, g, i, f: (b, g, 0, i)),
            pl.BlockSpec((1, 1, nkb, tk, HEAD_DIM), lambda b, g, i, f: (b, g, 0, 0, 0)),
            pl.BlockSpec((1, 1, nkb, HEAD_DIM, tk), lambda b, g, i, f: (b, g, 0, 0, 0)),
            pl.BlockSpec((1, tq, gw), lambda b, g, i, f: (b, i, COL_GB * D_MODEL // gw + g)),
        ],
        out_specs=pl.BlockSpec((1, tq, gw), lambda b, g, i, f: (b, i, g)),
        scratch_shapes=[
            pltpu.VMEM((GROUP, 1, tq), F32),
            pltpu.VMEM((GROUP, SUBLANES, tq), F32),
            pltpu.VMEM((GROUP, HEAD_DIM, tq), F32),
            pltpu.VMEM((tk, tq), F32),
        ],
    )
    return pl.pallas_call(
        functools.partial(_attn_kernel, nkb=nkb, layer=layer),
        grid_spec=grid_spec,
        out_shape=jax.ShapeDtypeStruct((bsz, s, D_MODEL), BF16),
        compiler_params=_cparams(("parallel", "parallel", "arbitrary")),
        name="gqa_attention",
    )(bounded_flags, qt, kk, vt, proj)


LRU_BWD = 2


def _gate_dot(ub, w_ref, d, n, pre_ref, slot):
    cols = slice(n * RNN_BW, (n + 1) * RNN_BW)
    pre_ref[slot, :, cols] = jnp.dot(ub[:, cols], w_ref[d, n], preferred_element_type=F32)


def _lru_coefficients(ua, r_half, i_half, b_r_half, b_i_half, c_half):
    y = c_half + c_half * jnp.tanh(r_half + b_r_half)
    ua_half = 0.5 * ua
    gated = ua_half + ua_half * jnp.tanh(i_half + b_i_half)
    a = jnp.exp2(y * (-LOG2E))
    x = jnp.tanh(y) * (1.0 + a * a)
    root = jnp.where(x > 0.0, x * lax.rsqrt(x), x)
    return a, root * gated


N_SLABS = D_MODEL // LANES
assert N_SLABS == SUBLANES


def _slab_pitch(t):
    return t + SUBLANES if (t // SUBLANES) % 2 == 0 else t


def _lru_kernel(xf_ref, xn_ref, xb_ref, wr_ref, br_ref, wi_ref, bi_ref, lam_ref, hf_ref, hb_ref, cf_ref, cbk_ref,
                sf_ref, sb_ref, ua_ref, rpre_ref, ipre_ref, chan_ref, *coef_refs, nt):
    i = pl.program_id(1)
    tm = hf_ref.shape[1]
    pitch = _slab_pitch(tm)
    coef_sets = (coef_refs[:4], coef_refs[4:])
    gates = ((wr_ref, rpre_ref), (wi_ref, ipre_ref))
    gate_dots = [(n, g) for n in range(RNN_BLOCKS) for g in range(len(gates))]

    @pl.when(i == 0)
    def _():
        cf_ref[...] = jnp.zeros(cf_ref.shape, F32)
        cbk_ref[...] = jnp.zeros(cbk_ref.shape, F32)
        for ref in coef_sets[1]:
            ref[...] = jnp.zeros(ref.shape, F32)
        ub = xf_ref[0]
        ua_ref[0] = ub.astype(F32)
        for n, g in gate_dots:
            _gate_dot(ub, gates[g][0], 0, n, gates[g][1], 0)

    ua_ref[LRU_BWD] = xb_ref[0].astype(F32)
    nl = -lam_ref[...]
    c_half = (0.5 * LRU_C) * (jnp.maximum(nl, 0.0) + jnp.log1p(jnp.exp(-jnp.abs(nl))))
    for d in range(2):
        for n, row in enumerate((0.5 * br_ref[d:d + 1, :], 0.5 * bi_ref[d:d + 1, :], c_half[d:d + 1, :])):
            chan_ref[3 * d + n] = jnp.broadcast_to(row, chan_ref.shape[1:])

    def step(cur, prev, parity):
        af_ref, uf_ref, ab_ref, ub_ref = prev

        hf, hb = cf_ref[...], cbk_ref[...]
        n_it = tm // SUBLANES
        chain_steps = tm // (2 * n_it)
        ub_bwd = xb_ref[0]
        ub_next = xn_ref[0]
        dot_every = n_it // len(gate_dots)
        slot_of = (parity, LRU_BWD)
        k = 0
        for it in range(2 * n_it):
            d, r0 = divmod(it, n_it)
            r0 *= SUBLANES
            rows = slice(r0, r0 + SUBLANES)
            src = slot_of[d]
            a, u = _lru_coefficients(ua_ref[src, rows, :], rpre_ref[src, rows, :], ipre_ref[src, rows, :],
                                     chan_ref[3 * d], chan_ref[3 * d + 1], chan_ref[3 * d + 2])
            for c in range(N_SLABS):
                dst = pl.ds(c * pitch + r0, SUBLANES)
                cur[2 * d][dst, :] = a[:, c * LANES:(c + 1) * LANES]
                cur[2 * d + 1][dst, :] = u[:, c * LANES:(c + 1) * LANES]
            for _ in range(chain_steps):
                tf = pl.ds(k, N_SLABS, stride=pitch)
                tb = pl.ds(tm - 1 - k, N_SLABS, stride=pitch)
                hf = af_ref[tf, :] * hf + uf_ref[tf, :]
                sf_ref[tf, :] = hf
                hb = ab_ref[tb, :] * hb + ub_ref[tb, :]
                sb_ref[tb, :] = hb
                k += 1
            if it % dot_every == 0:
                n, g = gate_dots[(it % n_it) // dot_every]
                if d == 0:
                    _gate_dot(ub_bwd, gates[g][0], 1, n, gates[g][1], LRU_BWD)
                else:
                    _gate_dot(ub_next, gates[g][0], 0, n, gates[g][1], 1 - parity)
            if it == n_it:
                ua_ref[1 - parity] = ub_next.astype(F32)
        assert k == tm
        keep = i > 0
        cf_ref[...] = jnp.where(keep, hf, cf_ref[...])
        cbk_ref[...] = jnp.where(keep, hb, cbk_ref[...])
        for s_ref, out_ref in ((sf_ref, hf_ref), (sb_ref, hb_ref)):
            for c in range(N_SLABS):
                out_ref[0, :, c * LANES:(c + 1) * LANES] = s_ref[pl.ds(c * pitch, tm), :].astype(out_ref.dtype)

    @pl.when(i % 2 == 0)
    def _():
        step(coef_sets[0], coef_sets[1], 0)

    @pl.when(i % 2 == 1)
    def _():
        step(coef_sets[1], coef_sets[0], 1)


def _rg_lru(proj, l, w_rg, b_rg, w_ig, b_ig, lam):
    bsz, s, _ = proj.shape
    d = D_MODEL
    tm = min(256, s)
    nt = s // tm
    out_sds = jax.ShapeDtypeStruct((bsz, s, d), BF16)
    return pl.pallas_call(
        functools.partial(_lru_kernel, nt=nt),
        grid=(bsz, nt + 1),
        in_specs=[
            pl.BlockSpec((1, tm, d), lambda b, i: (b, jnp.minimum(i, nt - 1), COL_UA)),
            pl.BlockSpec((1, tm, d), lambda b, i: (b, jnp.minimum(i + 1, nt - 1), COL_UA)),
            pl.BlockSpec((1, tm, d), lambda b, i: (b, nt - 1 - jnp.minimum(i, nt - 1), COL_UA)),
            _layer_spec(w_rg, l), _layer_spec(b_rg, l), _layer_spec(w_ig, l), _layer_spec(b_ig, l),
            _layer_spec(lam, l),
        ],
        out_specs=[
            pl.BlockSpec((1, tm, d), lambda b, i: (b, jnp.maximum(i - 1, 0), 0)),
            pl.BlockSpec((1, tm, d), lambda b, i: (b, nt - 1 - jnp.maximum(i - 1, 0), 0)),
        ],
        out_shape=[out_sds, out_sds],
        scratch_shapes=[pltpu.VMEM((N_SLABS, LANES), F32)] * 2
        + [pltpu.VMEM((N_SLABS * _slab_pitch(tm), LANES), F32)] * 2
        + [pltpu.VMEM((LRU_BWD + 1, tm, d), F32)] * 3
        + [pltpu.VMEM((6, 8, d), F32)]
        + [pltpu.VMEM((N_SLABS * _slab_pitch(tm), LANES), F32)] * 8,
        compiler_params=_cparams(("parallel", "arbitrary")),
        name="rg_lru",
    )(proj, proj, proj, w_rg, b_rg, w_ig, b_ig, lam)


def _merge_kernel(hf_ref, hb_ref, ga_ref, yb_ref, yc_ref, mg0_ref, mg1_ref, mg2_ref, x_ref, mod_ref, wb_ref,
                  wo_ref, o_ref):
    merged = mg1_ref[0].astype(F32) * jnp.dot(yb_ref[0], wb_ref[1], preferred_element_type=F32)
    merged += mg2_ref[0].astype(F32) * jnp.dot(yc_ref[0], wb_ref[2], preferred_element_type=F32)
    ya = (hf_ref[0].astype(F32) + hb_ref[0].astype(F32)) * ga_ref[0].astype(F32)
    merged += mg0_ref[0].astype(F32) * jnp.dot(ya.astype(BF16), wb_ref[0], preferred_element_type=F32)
    out = jnp.dot(merged.astype(BF16), wo_ref[...], preferred_element_type=F32)
    o_ref[0] = x_ref[0] + mod_ref[0, 2:3, :] * out


def _merge(hf, hb, yb, proj, x, mods, l, w_branch, w_out):
    bsz, s, d = x.shape
    tm = min(512, s)

    def col(c):
        return pl.BlockSpec((1, tm, d), lambda b, i: (b, i, c))

    return pl.pallas_call(
        _merge_kernel,
        grid=(bsz, s // tm),
        in_specs=[
            col(0), col(0), col(COL_GA), col(0), col(COL_YC), col(COL_MG), col(COL_MG + 1), col(COL_MG + 2),
            col(0),
            _mod_spec(mods, l), _layer_spec(w_branch, l), _layer_spec(w_out, l),
        ],
        out_specs=pl.BlockSpec((1, tm, d), lambda b, i: (b, i, 0)),
        out_shape=jax.ShapeDtypeStruct((bsz, s, d), F32),
        compiler_params=_cparams(("parallel", "parallel")),
        name="merge_outproj",
    )(hf, hb, proj, yb, proj, proj, proj, proj, x, mods, w_branch, w_out)


def _attn_tiles(s):
    tq = min(512, s)
    tk = min(512, s)
    return tq, tk


def kernel(x, c, w_ada, b_ada, norm_g, w_in, conv_a_w, conv_a_b, w_rg, b_rg, w_ig, b_ig, lru_lam, q_norm_g,
           k_norm_g, conv_c_w, w_branch, w_out):
    bsz, s, d = x.shape
    depth = w_ada.shape[0]
    tq, tk = _attn_tiles(s)
    tables = _rope_tables(s)
    mods = _modulation(c, w_ada, b_ada)
    w_in_bf = w_in.astype(BF16)
    w_rg_half = (0.5 * w_rg).astype(BF16)
    w_ig_half = (0.5 * w_ig).astype(BF16)
    w_branch_bf = w_branch.astype(BF16)
    w_out_bf = w_out.astype(BF16)
    bounded = (_score_bound(q_norm_g, k_norm_g) <= SCORE_BOUND_NOSHIFT).astype(jnp.int32)
    rows = lambda p: p.reshape(depth, 1, p.shape[-1])
    norm_g, conv_a_b, q_norm_g, k_norm_g = rows(norm_g), rows(conv_a_b), rows(q_norm_g), rows(k_norm_g)
    for l in range(depth):
        proj, qt, kk, vt = _in_projection(x, mods, l, norm_g, w_in_bf, tables, q_norm_g, k_norm_g, conv_a_w,
                                          conv_a_b, conv_c_w, tk)
        hf, hb = _rg_lru(proj, l, w_rg_half, b_rg, w_ig_half, b_ig, lru_lam)
        yb = _attention(qt, kk, vt, proj, bounded, l, tq)
        x = _merge(hf, hb, yb, proj, x, mods, l, w_branch_bf, w_out_bf)
    return x
```

```python
import functools

import numpy as np
import jax
import jax.numpy as jnp
from jax import lax
from jax.experimental import pallas as pl
from jax.experimental.pallas import tpu as pltpu

D_MODEL = 1024
EPS = 1e-6
GRID_W = 64
N_HEADS = 8
N_KV = 2
GROUP = N_HEADS // N_KV
HEAD_DIM = 128
RNN_BLOCKS = 4
RNN_BW = D_MODEL // RNN_BLOCKS
LRU_C = 8.0
ROPE_THETA = 10000.0
N_BRANCH = 3
LOG2E = 1.4426950408889634

KV_W = N_KV * HEAD_DIM
QKV_W = D_MODEL + 2 * KV_W
N_IN = 11 * D_MODEL + 2 * KV_W
INPROJ_TN = 512
INPROJ_ROWS = 256

COL_UA, COL_GA, COL_GB, COL_YC, COL_MG = 0, 1, 2, 3, 4
N_STORED = (COL_MG + N_BRANCH) * D_MODEL

_SRC = dict(xa=0, ga=D_MODEL, q=2 * D_MODEL, kv=3 * D_MODEL, gb=3 * D_MODEL + 2 * KV_W)
_SRC.update(xc=_SRC["gb"] + D_MODEL, bc=_SRC["gb"] + 2 * D_MODEL, cc=_SRC["gb"] + 3 * D_MODEL,
            gc=_SRC["gb"] + 4 * D_MODEL, mg=_SRC["gb"] + 5 * D_MODEL)
_TILE_ORDER = (("q", 0), ("mg", 0), ("q", 1), ("mg", 1), ("kv", 0), ("mg", 2), ("xa", 0), ("mg", 3),
               ("xa", 1), ("mg", 4), ("xc", 0), ("cc", 0), ("ga", 0), ("bc", 0), ("gc", 0), ("mg", 5),
               ("xc", 1), ("cc", 1), ("ga", 1), ("bc", 1), ("gc", 1), ("gb", 0), ("gb", 1))
INPROJ_TILES = tuple((k, _SRC[k] + t * INPROJ_TN, t) for k, t in _TILE_ORDER)
assert len(INPROJ_TILES) * INPROJ_TN == N_IN

SUBLANES = 8
LANES = 128
ROPE_SPAN = HEAD_DIM // 2
ROPE_PAIR = ROPE_SPAN // 2
HALO = 16
VMEM_LIMIT = 56 * 1024 * 1024

F32 = jnp.float32
BF16 = jnp.bfloat16


def _sigmoid(x):
    return 1.0 / (1.0 + jnp.exp2(x * (-LOG2E)))


def _silu(x):
    return x * _sigmoid(x)


def _cparams(sem):
    return pltpu.CompilerParams(dimension_semantics=sem, vmem_limit_bytes=VMEM_LIMIT)


def _mod_kernel(c_ref, w_ref, b_ref, o_ref):
    acc = jnp.dot(c_ref[...], w_ref[0], preferred_element_type=F32, precision=lax.Precision.HIGHEST)
    o_ref[0] = acc + b_ref[0]


def _modulation(c, w_ada, b_ada):
    depth, d, n = w_ada.shape
    bsz = c.shape[0]
    rows = SUBLANES
    assert bsz <= rows
    c8 = jnp.zeros((rows, d), F32).at[:bsz].set(c)
    tn = 1024
    out = pl.pallas_call(
        _mod_kernel,
        grid=(depth, n // tn),
        in_specs=[
            pl.BlockSpec((rows, d), lambda l, j: (0, 0)),
            pl.BlockSpec((1, d, tn), lambda l, j: (l, 0, j)),
            pl.BlockSpec((1, 1, tn), lambda l, j: (l, 0, j)),
        ],
        out_specs=pl.BlockSpec((1, rows, tn), lambda l, j: (l, 0, j)),
        out_shape=jax.ShapeDtypeStruct((depth, rows, n), F32),
        compiler_params=_cparams(("parallel", "parallel")),
        name="adaln_mod",
    )(c8, w_ada, b_ada.reshape(depth, 1, n))
    return out[:, :bsz].reshape(depth, bsz, 3, d)


def _rope_tables(s):
    half = HEAD_DIM // 2
    inv = ROPE_THETA ** (-np.arange(0, half, 2, dtype=np.float64) / half)
    rows = s // GRID_W
    ang_r = np.arange(rows, dtype=np.float64)[:, None] * inv
    ang_c = np.arange(GRID_W, dtype=np.float64)[:, None] * inv
    zr = np.zeros((rows, half))
    zc = np.zeros((GRID_W, half))
    row_cos = np.concatenate([np.cos(ang_r), np.cos(ang_r), zr], -1)
    row_sin = np.concatenate([-np.sin(ang_r), np.sin(ang_r), zr], -1)
    col_cos = np.concatenate([zc, np.cos(ang_c), np.cos(ang_c)], -1)
    col_sin = np.concatenate([zc, -np.sin(ang_c), np.sin(ang_c)], -1)
    bcast = lambda t: jnp.broadcast_to(jnp.asarray(t, F32)[:, None, :], (rows, SUBLANES, HEAD_DIM))
    return bcast(row_cos), bcast(row_sin), jnp.asarray(col_cos, F32), jnp.asarray(col_sin, F32)


def _norm_rope(xh, g, cos_t, sin_t, lane_lo):
    ms = jnp.mean(xh * xh, axis=-1, keepdims=True)
    y = xh * lax.rsqrt(ms + EPS) * g
    partner = jnp.where(lane_lo, pltpu.roll(y, HEAD_DIM - ROPE_PAIR, axis=1), pltpu.roll(y, ROPE_PAIR, axis=1))
    return y * cos_t + partner * sin_t


def _shift_rows(tile, halo, shift):
    t = tile.shape[0]
    g = SUBLANES
    row = lax.broadcasted_iota(jnp.int32, (g, tile.shape[1]), 0)
    if shift > 0:
        rolled = pltpu.roll(tile, shift, axis=0)
        edge = pltpu.roll(halo, shift, axis=0)[0:g]
        fixed = jnp.where(row < shift, edge, rolled[0:g])
        return jnp.concatenate([fixed, rolled[g:]], axis=0)
    k = -shift
    rolled = pltpu.roll(tile, t - k, axis=0)
    edge = pltpu.roll(halo, HALO - k, axis=0)[HALO - g:HALO]
    fixed = jnp.where(row >= g - k, edge, rolled[t - g:t])
    return jnp.concatenate([rolled[:t - g], fixed], axis=0)


def _inproj_kernel(x_ref, xp_ref, xn_ref, mod_ref, g_ref, w_ref, rcos_ref, rsin_ref, ccos_ref, csin_ref, qg_ref,
                   kg_ref, caw_ref, cab_ref, ccw_ref, o_ref, qt_ref, k_ref, vt_ref, *, nt):
    i = pl.program_id(1)
    tm = x_ref.shape[1]
    ts = min(INPROJ_ROWS, tm)
    xv = jnp.concatenate([xp_ref[0], x_ref[0], xn_ref[0]], axis=0)
    ms = jnp.mean(xv * xv, axis=-1, keepdims=True)
    y = xv * lax.rsqrt(ms + EPS) * g_ref[...]
    shift = mod_ref[0, 0:1, :]
    scale = mod_ref[0, 1:2, :]
    h_all = (y * (1.0 + scale) + shift).astype(BF16)

    reps = GRID_W // SUBLANES
    cos_all = jnp.concatenate([jnp.tile(rcos_ref[r], (reps, 1)) + ccos_ref[...] for r in range(tm // GRID_W)], 0)
    sin_all = jnp.concatenate([jnp.tile(rsin_ref[r], (reps, 1)) + csin_ref[...] for r in range(tm // GRID_W)], 0)
    lane = lax.broadcasted_iota(jnp.int32, (ts, HEAD_DIM), 1)
    lane_lo = (lane % ROPE_SPAN) < ROPE_PAIR
    q_scale = (HEAD_DIM ** -0.5) * LOG2E

    n_sub = tm // ts
    work = [(s, j) for s in range(n_sub) for j in range(len(INPROJ_TILES))]

    def project(s, j):
        ext = INPROJ_TILES[j][0] in ("xa", "xc", "cc")
        rows = slice(s * ts, s * ts + ts + 2 * HALO) if ext else slice(s * ts + HALO, s * ts + HALO + ts)
        src = INPROJ_TILES[j][1]
        return jnp.dot(h_all[rows], w_ref[:, src:src + INPROJ_TN], preferred_element_type=F32)

    held = {}
    acc_next = project(*work[0])
    for n, (s, j) in enumerate(work):
        kind, _, half = INPROJ_TILES[j]
        lanes = slice(half * INPROJ_TN, (half + 1) * INPROJ_TN)
        rows = slice(s * ts, (s + 1) * ts)
        cos_t, sin_t = cos_all[rows], sin_all[rows]
        acc = acc_next
        if n + 1 < len(work):
            acc_next = project(*work[n + 1])

        def with_halos(acc_ext):
            prev, nxt = acc_ext[0:HALO], acc_ext[HALO + ts:2 * HALO + ts]
            if s == 0:
                prev = jnp.where(i == 0, 0.0, prev)
            if s == n_sub - 1:
                nxt = jnp.where(i == nt - 1, 0.0, nxt)
            return acc_ext[HALO:HALO + ts], prev, nxt

        def store(col, val):
            o_ref[0, rows, col * D_MODEL + half * INPROJ_TN:col * D_MODEL + (half + 1) * INPROJ_TN] = val.astype(BF16)

        if kind == "q":
            for hh in range(INPROJ_TN // HEAD_DIM):
                qh = _norm_rope(acc[:, hh * HEAD_DIM:(hh + 1) * HEAD_DIM], qg_ref[...], cos_t, sin_t, lane_lo)
                qt_ref[0, half * (INPROJ_TN // HEAD_DIM) + hh, :, rows] = (qh * q_scale).T.astype(BF16)
        elif kind == "kv":
            for hh in range(N_KV):
                kh = acc[:, hh * HEAD_DIM:(hh + 1) * HEAD_DIM]
                k_ref[0, hh, 0, rows, :] = _norm_rope(kh, kg_ref[...], cos_t, sin_t, lane_lo).astype(BF16)
                vt_ref[0, hh, 0, :, rows] = acc[:, KV_W + hh * HEAD_DIM:KV_W + (hh + 1) * HEAD_DIM].T.astype(BF16)
        elif kind == "xa":
            tile, prev, nxt = with_halos(acc)
            cw = caw_ref[:, lanes]
            store(COL_UA, cw[0:1] * _shift_rows(tile, prev, 2) + cw[1:2] * _shift_rows(tile, prev, 1)
                  + cw[2:3] * tile + cw[3:4] * _shift_rows(tile, nxt, -1) + cab_ref[:, lanes])
        elif kind == "xc":
            held["xc"] = acc
        elif kind == "cc":
            tile, prev, nxt = with_halos(acc * held.pop("xc"))
            cw = ccw_ref[:, lanes]
            held["conv"] = cw[0:1] * _shift_rows(tile, prev, 1) + cw[1:2] * tile + cw[2:3] * _shift_rows(tile, nxt, -1)
        elif kind == "bc":
            held["conv"] = acc * held["conv"]
        elif kind == "gc":
            store(COL_YC, held.pop("conv") * _silu(acc))
        elif kind == "ga":
            store(COL_GA, _silu(acc))
        elif kind == "gb":
            store(COL_GB, _silu(acc))
        else:
            o_ref[0, rows, COL_MG * D_MODEL + half * INPROJ_TN:COL_MG * D_MODEL + (half + 1) * INPROJ_TN] = (
                _sigmoid(acc).astype(BF16))


def _layer_spec(arr, l):
    nd = arr.ndim - 1
    assert nd >= 2
    return pl.BlockSpec((None,) + arr.shape[1:], lambda *_: (l,) + (0,) * nd)


def _mod_spec(mods, l):
    return pl.BlockSpec((None, 1) + mods.shape[2:], lambda b, i: (l, b, 0, 0))


def _in_projection(x, mods, l, norm_g, w_in, tables, q_g, k_g, conv_a_w, conv_a_b, conv_c_w, tk):
    bsz, s, d = x.shape
    tm = min(2 * INPROJ_ROWS, s)
    assert INPROJ_TN == 2 * KV_W and D_MODEL == 2 * INPROJ_TN and tm % GRID_W == 0 and tk % tm == 0
    assert tm % min(INPROJ_ROWS, tm) == 0
    nt = s // tm
    nkb = s // tk
    sub = tk // tm
    hbk = tm // HALO
    rcos, rsin, ccos, csin = tables
    row_spec = pl.BlockSpec((tm // GRID_W, 8, HEAD_DIM), lambda b, i: (i, 0, 0))
    col_spec = pl.BlockSpec((GRID_W, HEAD_DIM), lambda b, i: (0, 0))
    return pl.pallas_call(
        functools.partial(_inproj_kernel, nt=nt),
        grid=(bsz, nt),
        in_specs=[
            pl.BlockSpec((1, tm, d), lambda b, i: (b, i, 0)),
            pl.BlockSpec((1, HALO, d), lambda b, i: (b, jnp.maximum(i * hbk - 1, 0), 0)),
            pl.BlockSpec((1, HALO, d), lambda b, i: (b, jnp.minimum((i + 1) * hbk, s // HALO - 1), 0)),
            _mod_spec(mods, l),
            _layer_spec(norm_g, l),
            pl.BlockSpec((None, d, N_IN), lambda b, i: (l, 0, 0), pipeline_mode=pl.Buffered(1)),
            row_spec, row_spec, col_spec, col_spec, _layer_spec(q_g, l), _layer_spec(k_g, l),
            _layer_spec(conv_a_w, l), _layer_spec(conv_a_b, l), _layer_spec(conv_c_w, l),
        ],
        out_specs=[
            pl.BlockSpec((1, tm, N_STORED), lambda b, i: (b, i, 0)),
            pl.BlockSpec((1, N_HEADS, HEAD_DIM, tm), lambda b, i: (b, 0, 0, i)),
            pl.BlockSpec((1, N_KV, 1, tm, HEAD_DIM), lambda b, i: (b, 0, i // sub, i % sub, 0)),
            pl.BlockSpec((1, N_KV, 1, HEAD_DIM, tm), lambda b, i: (b, 0, i // sub, 0, i % sub)),
        ],
        out_shape=[
            jax.ShapeDtypeStruct((bsz, s, N_STORED), BF16),
            jax.ShapeDtypeStruct((bsz, N_HEADS, HEAD_DIM, s), BF16),
            jax.ShapeDtypeStruct((bsz, N_KV, nkb, tk, HEAD_DIM), BF16),
            jax.ShapeDtypeStruct((bsz, N_KV, nkb, HEAD_DIM, tk), BF16),
        ],
        compiler_params=_cparams(("parallel", "parallel")),
        name="norm_inproj",
    )(x, x, x, mods, norm_g, w_in, rcos, rsin, ccos, csin, q_g, k_g, conv_a_w, conv_a_b, conv_c_w)


SCORE_BOUND_NOSHIFT = 40.0


def _score_bound(q_g, k_g):
    q_scale = (HEAD_DIM ** -0.5) * LOG2E
    return HEAD_DIM * q_scale * jnp.max(jnp.abs(q_g), axis=-1) * jnp.max(jnp.abs(k_g), axis=-1)


KV_BLOCKS_PER_TRIP = 16


def _largest_divisor(n, cap):
    return max(d for d in range(1, cap + 1) if n % d == 0)


def _attn_kernel(flag_ref, qt_ref, qtn_ref, k_ref, vt_ref, gb_ref, o_ref, m_ref, l_ref, acc_ref, s_ref, q0_ref,
                 *, nkb, layer):
    tq = acc_ref.shape[-1]
    l_ref[...] = jnp.zeros(l_ref.shape, F32)
    acc_ref[...] = jnp.zeros(acc_ref.shape, F32)
    bounded = flag_ref[layer] != 0

    @pl.when(bounded)
    def _():
        def scores(j, h):
            return jnp.dot(k_ref[0, 0, j], qt_ref[0, h], preferred_element_type=F32)

        @pl.when(pl.program_id(2) == 0)
        def _():
            s_ref[...] = scores(0, 0)

        q0_ref[0] = qt_ref[0, 0]
        q0_ref[1] = qtn_ref[0, 0]

        def first_head_scores(j_next):
            wrap = j_next >= nkb
            return jnp.dot(k_ref[0, 0, jnp.where(wrap, 0, j_next)], q0_ref[wrap.astype(jnp.int32)],
                           preferred_element_type=F32)

        group = _largest_divisor(nkb, KV_BLOCKS_PER_TRIP)

        def kv_group(g, carry):
            s_cur = s_ref[...]
            for u in range(group):
                j = g * group + u
                vblk = vt_ref[0, 0, j]
                for h in range(GROUP):
                    s_next = scores(j, h + 1) if h + 1 < GROUP else first_head_scores(j + 1)
                    p = jnp.exp2(s_cur)
                    l_ref[h] += jnp.sum(p.reshape(-1, SUBLANES, tq), axis=0)
                    acc_ref[h] += jnp.dot(vblk, p.astype(BF16), preferred_element_type=F32)
                    s_cur = s_next
            s_ref[...] = s_cur
            return carry

        lax.fori_loop(0, nkb // group, kv_group, 0)

    @pl.when(jnp.logical_not(bounded))
    def _():
        m_ref[...] = jnp.full(m_ref.shape, -jnp.inf, F32)

        def kv_step(j, carry):
            kblk = k_ref[0, 0, j]
            vblk = vt_ref[0, 0, j]
            for h in range(GROUP):
                s = jnp.dot(kblk, qt_ref[0, h], preferred_element_type=F32)
                m_old = m_ref[h]
                m_new = jnp.maximum(m_old, jnp.max(s, axis=0, keepdims=True))
                alpha = jnp.exp2(m_old - m_new)
                p = jnp.exp2(s - m_new)
                l_ref[h] = alpha * l_ref[h] + jnp.sum(p.reshape(-1, SUBLANES, tq), axis=0)
                acc_ref[h] = alpha * acc_ref[h] + jnp.dot(vblk, p.astype(BF16), preferred_element_type=F32)
                m_ref[h] = m_new
            return carry

        lax.fori_loop(0, nkb, kv_step, 0)

    for h in range(GROUP):
        l = jnp.sum(l_ref[h], axis=0, keepdims=True)
        o = acc_ref[h] * (1.0 / l)
        gate = gb_ref[0, :, h * HEAD_DIM:(h + 1) * HEAD_DIM].astype(F32)
        o_ref[0, :, h * HEAD_DIM:(h + 1) * HEAD_DIM] = (o.T * gate).astype(BF16)


def _attention(qt, kk, vt, proj, bounded_flags, layer, tq):
    bsz, _, _, s = qt.shape
    nkb, tk = kk.shape[2], kk.shape[3]
    gw = GROUP * HEAD_DIM
    grid_spec = pltpu.PrefetchScalarGridSpec(
        num_scalar_prefetch=1,
        grid=(bsz, N_KV, s // tq),
        in_specs=[
            pl.BlockSpec((1, GROUP, HEAD_DIM, tq), lambda b, g, i, f: (b, g, 0, i)),
            pl.BlockSpec((1, 1, HEAD_DIM, tq), lambda b, g, i, f: (b, g * GROUP, 0, jnp.minimum(i + 1, s // tq - 1))),
            pl.BlockSpec((1, 1, nkb, tk, HEAD_DIM), lambda b, g, i, f: (b, g, 0, 0, 0)),
            pl.BlockSpec((1, 1, nkb, HEAD_DIM, tk), lambda b, g, i, f: (b, g, 0, 0, 0)),
            pl.BlockSpec((1, tq, gw), lambda b, g, i, f: (b, i, COL_GB * D_MODEL // gw + g)),
        ],
        out_specs=pl.BlockSpec((1, tq, gw), lambda b, g, i, f: (b, i, g)),
        scratch_shapes=[
            pltpu.VMEM((GROUP, 1, tq), F32),
            pltpu.VMEM((GROUP, SUBLANES, tq), F32),
            pltpu.VMEM((GROUP, HEAD_DIM, tq), F32),
            pltpu.VMEM((tk, tq), F32),
            pltpu.VMEM((2, HEAD_DIM, tq), BF16),
        ],
    )
    return pl.pallas_call(
        functools.partial(_attn_kernel, nkb=nkb, layer=layer),
        grid_spec=grid_spec,
        out_shape=jax.ShapeDtypeStruct((bsz, s, D_MODEL), BF16),
        compiler_params=_cparams(("parallel", "parallel", "arbitrary")),
        name="gqa_attention",
    )(bounded_flags, qt, qt, kk, vt, proj)


LRU_BWD = 2


def _gate_dot(ub, w_ref, d, n, pre_ref, slot):
    cols = slice(n * RNN_BW, (n + 1) * RNN_BW)
    pre_ref[slot, :, cols] = jnp.dot(ub[:, cols], w_ref[d, n], preferred_element_type=F32)


def _lru_coefficients(ua, r_half, i_half, b_r_half, b_i_half, c_half):
    y = c_half + c_half * jnp.tanh(r_half + b_r_half)
    ua_half = 0.5 * ua
    gated = ua_half + ua_half * jnp.tanh(i_half + b_i_half)
    a = jnp.exp2(y * (-LOG2E))
    x = jnp.tanh(y) * (1.0 + a * a)
    root = jnp.where(x > 0.0, x * lax.rsqrt(x), x)
    return a, root * gated


N_SLABS = D_MODEL // LANES
assert N_SLABS == SUBLANES


def _slab_pitch(t):
    return t + SUBLANES if (t // SUBLANES) % 2 == 0 else t


def _lru_kernel(xf_ref, xn_ref, xb_ref, wr_ref, br_ref, wi_ref, bi_ref, lam_ref, hf_ref, hb_ref, cf_ref, cbk_ref,
                sf_ref, sb_ref, ua_ref, rpre_ref, ipre_ref, chan_ref, *coef_refs, nt):
    i = pl.program_id(1)
    tm = hf_ref.shape[1]
    pitch = _slab_pitch(tm)
    coef_sets = (coef_refs[:4], coef_refs[4:])
    gates = ((wr_ref, rpre_ref), (wi_ref, ipre_ref))
    gate_dots = [(n, g) for n in range(RNN_BLOCKS) for g in range(len(gates))]

    @pl.when(i == 0)
    def _():
        cf_ref[...] = jnp.zeros(cf_ref.shape, F32)
        cbk_ref[...] = jnp.zeros(cbk_ref.shape, F32)
        for ref in coef_sets[1]:
            ref[...] = jnp.zeros(ref.shape, F32)
        ub = xf_ref[0]
        ua_ref[0] = ub.astype(F32)
        for n, g in gate_dots:
            _gate_dot(ub, gates[g][0], 0, n, gates[g][1], 0)

    ua_ref[LRU_BWD] = xb_ref[0].astype(F32)
    nl = -lam_ref[...]
    c_half = (0.5 * LRU_C) * (jnp.maximum(nl, 0.0) + jnp.log1p(jnp.exp(-jnp.abs(nl))))
    for d in range(2):
        for n, row in enumerate((0.5 * br_ref[d:d + 1, :], 0.5 * bi_ref[d:d + 1, :], c_half[d:d + 1, :])):
            chan_ref[3 * d + n] = jnp.broadcast_to(row, chan_ref.shape[1:])

    def step(cur, prev, parity):
        af_ref, uf_ref, ab_ref, ub_ref = prev

        hf, hb = cf_ref[...], cbk_ref[...]
        n_it = tm // SUBLANES
        chain_steps = tm // (2 * n_it)
        ub_bwd = xb_ref[0]
        ub_next = xn_ref[0]
        dot_every = n_it // len(gate_dots)
        slot_of = (parity, LRU_BWD)
        k = 0
        for it in range(2 * n_it):
            d, r0 = divmod(it, n_it)
            r0 *= SUBLANES
            rows = slice(r0, r0 + SUBLANES)
            src = slot_of[d]
            a, u = _lru_coefficients(ua_ref[src, rows, :], rpre_ref[src, rows, :], ipre_ref[src, rows, :],
                                     chan_ref[3 * d], chan_ref[3 * d + 1], chan_ref[3 * d + 2])
            for c in range(N_SLABS):
                dst = pl.ds(c * pitch + r0, SUBLANES)
                cur[2 * d][dst, :] = a[:, c * LANES:(c + 1) * LANES]
                cur[2 * d + 1][dst, :] = u[:, c * LANES:(c + 1) * LANES]
            for _ in range(chain_steps):
                tf = pl.ds(k, N_SLABS, stride=pitch)
                tb = pl.ds(tm - 1 - k, N_SLABS, stride=pitch)
                hf = af_ref[tf, :] * hf + uf_ref[tf, :]
                sf_ref[tf, :] = hf
                hb = ab_ref[tb, :] * hb + ub_ref[tb, :]
                sb_ref[tb, :] = hb
                k += 1
            if it % dot_every == 0:
                n, g = gate_dots[(it % n_it) // dot_every]
                if d == 0:
                    _gate_dot(ub_bwd, gates[g][0], 1, n, gates[g][1], LRU_BWD)
                else:
                    _gate_dot(ub_next, gates[g][0], 0, n, gates[g][1], 1 - parity)
            if it == n_it:
                ua_ref[1 - parity] = ub_next.astype(F32)
        assert k == tm
        keep = i > 0
        cf_ref[...] = jnp.where(keep, hf, cf_ref[...])
        cbk_ref[...] = jnp.where(keep, hb, cbk_ref[...])
        for s_ref, out_ref in ((sf_ref, hf_ref), (sb_ref, hb_ref)):
            for c in range(N_SLABS):
                out_ref[0, :, c * LANES:(c + 1) * LANES] = s_ref[pl.ds(c * pitch, tm), :].astype(out_ref.dtype)

    @pl.when(i % 2 == 0)
    def _():
        step(coef_sets[0], coef_sets[1], 0)

    @pl.when(i % 2 == 1)
    def _():
        step(coef_sets[1], coef_sets[0], 1)


def _rg_lru(proj, l, w_rg, b_rg, w_ig, b_ig, lam):
    bsz, s, _ = proj.shape
    d = D_MODEL
    tm = min(256, s)
    nt = s // tm
    out_sds = jax.ShapeDtypeStruct((bsz, s, d), BF16)
    return pl.pallas_call(
        functools.partial(_lru_kernel, nt=nt),
        grid=(bsz, nt + 1),
        in_specs=[
            pl.BlockSpec((1, tm, d), lambda b, i: (b, jnp.minimum(i, nt - 1), COL_UA)),
            pl.BlockSpec((1, tm, d), lambda b, i: (b, jnp.minimum(i + 1, nt - 1), COL_UA)),
            pl.BlockSpec((1, tm, d), lambda b, i: (b, nt - 1 - jnp.minimum(i, nt - 1), COL_UA)),
            _layer_spec(w_rg, l), _layer_spec(b_rg, l), _layer_spec(w_ig, l), _layer_spec(b_ig, l),
            _layer_spec(lam, l),
        ],
        out_specs=[
            pl.BlockSpec((1, tm, d), lambda b, i: (b, jnp.maximum(i - 1, 0), 0)),
            pl.BlockSpec((1, tm, d), lambda b, i: (b, nt - 1 - jnp.maximum(i - 1, 0), 0)),
        ],
        out_shape=[out_sds, out_sds],
        scratch_shapes=[pltpu.VMEM((N_SLABS, LANES), F32)] * 2
        + [pltpu.VMEM((N_SLABS * _slab_pitch(tm), LANES), F32)] * 2
        + [pltpu.VMEM((LRU_BWD + 1, tm, d), F32)] * 3
        + [pltpu.VMEM((6, 8, d), F32)]
        + [pltpu.VMEM((N_SLABS * _slab_pitch(tm), LANES), F32)] * 8,
        compiler_params=_cparams(("parallel", "arbitrary")),
        name="rg_lru",
    )(proj, proj, proj, w_rg, b_rg, w_ig, b_ig, lam)


def _merge_kernel(hf_ref, hb_ref, ga_ref, yb_ref, yc_ref, mg0_ref, mg1_ref, mg2_ref, x_ref, mod_ref, wb_ref,
                  wo_ref, o_ref):
    merged = mg1_ref[0].astype(F32) * jnp.dot(yb_ref[0], wb_ref[1], preferred_element_type=F32)
    merged += mg2_ref[0].astype(F32) * jnp.dot(yc_ref[0], wb_ref[2], preferred_element_type=F32)
    ya = (hf_ref[0].astype(F32) + hb_ref[0].astype(F32)) * ga_ref[0].astype(F32)
    merged += mg0_ref[0].astype(F32) * jnp.dot(ya.astype(BF16), wb_ref[0], preferred_element_type=F32)
    out = jnp.dot(merged.astype(BF16), wo_ref[...], preferred_element_type=F32)
    o_ref[0] = x_ref[0] + mod_ref[0, 2:3, :] * out


def _merge(hf, hb, yb, proj, x, mods, l, w_branch, w_out):
    bsz, s, d = x.shape
    tm = min(512, s)

    def col(c):
        return pl.BlockSpec((1, tm, d), lambda b, i: (b, i, c))

    return pl.pallas_call(
        _merge_kernel,
        grid=(bsz, s // tm),
        in_specs=[
            col(0), col(0), col(COL_GA), col(0), col(COL_YC), col(COL_MG), col(COL_MG + 1), col(COL_MG + 2),
            col(0),
            _mod_spec(mods, l), _layer_spec(w_branch, l), _layer_spec(w_out, l),
        ],
        out_specs=pl.BlockSpec((1, tm, d), lambda b, i: (b, i, 0)),
        out_shape=jax.ShapeDtypeStruct((bsz, s, d), F32),
        compiler_params=_cparams(("parallel", "parallel")),
        name="merge_outproj",
    )(hf, hb, proj, yb, proj, proj, proj, proj, x, mods, w_branch, w_out)


def _attn_tiles(s):
    tq = min(512, s)
    tk = min(512, s)
    return tq, tk


def kernel(x, c, w_ada, b_ada, norm_g, w_in, conv_a_w, conv_a_b, w_rg, b_rg, w_ig, b_ig, lru_lam, q_norm_g,
           k_norm_g, conv_c_w, w_branch, w_out):
    bsz, s, d = x.shape
    depth = w_ada.shape[0]
    tq, tk = _attn_tiles(s)
    tables = _rope_tables(s)
    mods = _modulation(c, w_ada, b_ada)
    w_in_bf = w_in.astype(BF16)
    w_rg_half = (0.5 * w_rg).astype(BF16)
    w_ig_half = (0.5 * w_ig).astype(BF16)
    w_branch_bf = w_branch.astype(BF16)
    w_out_bf = w_out.astype(BF16)
    bounded = (_score_bound(q_norm_g, k_norm_g) <= SCORE_BOUND_NOSHIFT).astype(jnp.int32)
    rows = lambda p: p.reshape(depth, 1, p.shape[-1])
    norm_g, conv_a_b, q_norm_g, k_norm_g = rows(norm_g), rows(conv_a_b), rows(q_norm_g), rows(k_norm_g)
    for l in range(depth):
        proj, qt, kk, vt = _in_projection(x, mods, l, norm_g, w_in_bf, tables, q_norm_g, k_norm_g, conv_a_w,
                                          conv_a_b, conv_c_w, tk)
        hf, hb = _rg_lru(proj, l, w_rg_half, b_rg, w_ig_half, b_ig, lru_lam)
        yb = _attention(qt, kk, vt, proj, bounded, l, tq)
        x = _merge(hf, hb, yb, proj, x, mods, l, w_branch_bf, w_out_bf)
    return x
```

```python
import functools

import numpy as np
import jax
import jax.numpy as jnp
from jax import lax
from jax.experimental import pallas as pl
from jax.experimental.pallas import tpu as pltpu

D_MODEL = 1024
EPS = 1e-6
GRID_W = 64
N_HEADS = 8
N_KV = 2
GROUP = N_HEADS // N_KV
HEAD_DIM = 128
RNN_BLOCKS = 4
RNN_BW = D_MODEL // RNN_BLOCKS
LRU_C = 8.0
ROPE_THETA = 10000.0
N_BRANCH = 3
LOG2E = 1.4426950408889634

KV_W = N_KV * HEAD_DIM
QKV_W = D_MODEL + 2 * KV_W
N_IN = 11 * D_MODEL + 2 * KV_W
INPROJ_TN = 512
INPROJ_ROWS = 256

COL_UA, COL_GA, COL_GB, COL_YC, COL_MG = 0, 1, 2, 3, 4
N_STORED = (COL_MG + N_BRANCH) * D_MODEL

_SRC = dict(xa=0, ga=D_MODEL, q=2 * D_MODEL, kv=3 * D_MODEL, gb=3 * D_MODEL + 2 * KV_W)
_SRC.update(xc=_SRC["gb"] + D_MODEL, bc=_SRC["gb"] + 2 * D_MODEL, cc=_SRC["gb"] + 3 * D_MODEL,
            gc=_SRC["gb"] + 4 * D_MODEL, mg=_SRC["gb"] + 5 * D_MODEL)
_TILE_ORDER = (("q", 0), ("mg", 0), ("q", 1), ("mg", 1), ("kv", 0), ("mg", 2), ("xa", 0), ("mg", 3),
               ("xa", 1), ("mg", 4), ("xc", 0), ("cc", 0), ("ga", 0), ("bc", 0), ("gc", 0), ("mg", 5),
               ("xc", 1), ("cc", 1), ("ga", 1), ("bc", 1), ("gc", 1), ("gb", 0), ("gb", 1))
INPROJ_TILES = tuple((k, _SRC[k] + t * INPROJ_TN, t) for k, t in _TILE_ORDER)
assert len(INPROJ_TILES) * INPROJ_TN == N_IN

SUBLANES = 8
LANES = 128
ROPE_SPAN = HEAD_DIM // 2
ROPE_PAIR = ROPE_SPAN // 2
HALO = 16
VMEM_LIMIT = 56 * 1024 * 1024

F32 = jnp.float32
BF16 = jnp.bfloat16


def _sigmoid(x):
    return 1.0 / (1.0 + jnp.exp2(x * (-LOG2E)))


def _silu(x):
    return x * _sigmoid(x)


def _cparams(sem):
    return pltpu.CompilerParams(dimension_semantics=sem, vmem_limit_bytes=VMEM_LIMIT)


def _mod_kernel(c_ref, w_ref, b_ref, o_ref):
    acc = jnp.dot(c_ref[...], w_ref[0], preferred_element_type=F32, precision=lax.Precision.HIGHEST)
    o_ref[0] = acc + b_ref[0]


def _modulation(c, w_ada, b_ada):
    depth, d, n = w_ada.shape
    bsz = c.shape[0]
    rows = SUBLANES
    assert bsz <= rows
    c8 = jnp.zeros((rows, d), F32).at[:bsz].set(c)
    tn = 1024
    out = pl.pallas_call(
        _mod_kernel,
        grid=(depth, n // tn),
        in_specs=[
            pl.BlockSpec((rows, d), lambda l, j: (0, 0)),
            pl.BlockSpec((1, d, tn), lambda l, j: (l, 0, j)),
            pl.BlockSpec((1, 1, tn), lambda l, j: (l, 0, j)),
        ],
        out_specs=pl.BlockSpec((1, rows, tn), lambda l, j: (l, 0, j)),
        out_shape=jax.ShapeDtypeStruct((depth, rows, n), F32),
        compiler_params=_cparams(("parallel", "parallel")),
        name="adaln_mod",
    )(c8, w_ada, b_ada.reshape(depth, 1, n))
    return out[:, :bsz].reshape(depth, bsz, 3, d)


def _rope_tables(s):
    half = HEAD_DIM // 2
    inv = ROPE_THETA ** (-np.arange(0, half, 2, dtype=np.float64) / half)
    rows = s // GRID_W
    ang_r = np.arange(rows, dtype=np.float64)[:, None] * inv
    ang_c = np.arange(GRID_W, dtype=np.float64)[:, None] * inv
    zr = np.zeros((rows, half))
    zc = np.zeros((GRID_W, half))
    row_cos = np.concatenate([np.cos(ang_r), np.cos(ang_r), zr], -1)
    row_sin = np.concatenate([-np.sin(ang_r), np.sin(ang_r), zr], -1)
    col_cos = np.concatenate([zc, np.cos(ang_c), np.cos(ang_c)], -1)
    col_sin = np.concatenate([zc, -np.sin(ang_c), np.sin(ang_c)], -1)
    bcast = lambda t: jnp.broadcast_to(jnp.asarray(t, F32)[:, None, :], (rows, SUBLANES, HEAD_DIM))
    return bcast(row_cos), bcast(row_sin), jnp.asarray(col_cos, F32), jnp.asarray(col_sin, F32)


def _norm_rope(xh, g, cos_t, sin_t, lane_lo):
    ms = jnp.mean(xh * xh, axis=-1, keepdims=True)
    y = xh * lax.rsqrt(ms + EPS) * g
    partner = jnp.where(lane_lo, pltpu.roll(y, HEAD_DIM - ROPE_PAIR, axis=1), pltpu.roll(y, ROPE_PAIR, axis=1))
    return y * cos_t + partner * sin_t


def _shift_rows(tile, halo, shift):
    t = tile.shape[0]
    g = SUBLANES
    row = lax.broadcasted_iota(jnp.int32, (g, tile.shape[1]), 0)
    if shift > 0:
        rolled = pltpu.roll(tile, shift, axis=0)
        edge = pltpu.roll(halo, shift, axis=0)[0:g]
        fixed = jnp.where(row < shift, edge, rolled[0:g])
        return jnp.concatenate([fixed, rolled[g:]], axis=0)
    k = -shift
    rolled = pltpu.roll(tile, t - k, axis=0)
    edge = pltpu.roll(halo, HALO - k, axis=0)[HALO - g:HALO]
    fixed = jnp.where(row >= g - k, edge, rolled[t - g:t])
    return jnp.concatenate([rolled[:t - g], fixed], axis=0)


def _inproj_kernel(x_ref, xp_ref, xn_ref, mod_ref, g_ref, w_ref, rcos_ref, rsin_ref, ccos_ref, csin_ref, qg_ref,
                   kg_ref, caw_ref, cab_ref, ccw_ref, o_ref, qt_ref, k_ref, vt_ref, *, nt):
    i = pl.program_id(1)
    tm = x_ref.shape[1]
    ts = min(INPROJ_ROWS, tm)
    xv = jnp.concatenate([xp_ref[0], x_ref[0], xn_ref[0]], axis=0)
    ms = jnp.mean(xv * xv, axis=-1, keepdims=True)
    y = xv * lax.rsqrt(ms + EPS) * g_ref[...]
    shift = mod_ref[0, 0:1, :]
    scale = mod_ref[0, 1:2, :]
    h_all = (y * (1.0 + scale) + shift).astype(BF16)

    reps = GRID_W // SUBLANES
    cos_all = jnp.concatenate([jnp.tile(rcos_ref[r], (reps, 1)) + ccos_ref[...] for r in range(tm // GRID_W)], 0)
    sin_all = jnp.concatenate([jnp.tile(rsin_ref[r], (reps, 1)) + csin_ref[...] for r in range(tm // GRID_W)], 0)
    lane = lax.broadcasted_iota(jnp.int32, (ts, HEAD_DIM), 1)
    lane_lo = (lane % ROPE_SPAN) < ROPE_PAIR
    q_scale = (HEAD_DIM ** -0.5) * LOG2E

    n_sub = tm // ts
    work = [(s, j) for s in range(n_sub) for j in range(len(INPROJ_TILES))]

    def project(s, j):
        ext = INPROJ_TILES[j][0] in ("xa", "xc", "cc")
        rows = slice(s * ts, s * ts + ts + 2 * HALO) if ext else slice(s * ts + HALO, s * ts + HALO + ts)
        src = INPROJ_TILES[j][1]
        return jnp.dot(h_all[rows], w_ref[:, src:src + INPROJ_TN], preferred_element_type=F32)

    held = {}
    acc_next = project(*work[0])
    for n, (s, j) in enumerate(work):
        kind, _, half = INPROJ_TILES[j]
        lanes = slice(half * INPROJ_TN, (half + 1) * INPROJ_TN)
        rows = slice(s * ts, (s + 1) * ts)
        cos_t, sin_t = cos_all[rows], sin_all[rows]
        acc = acc_next
        if n + 1 < len(work):
            acc_next = project(*work[n + 1])

        def with_halos(acc_ext):
            prev, nxt = acc_ext[0:HALO], acc_ext[HALO + ts:2 * HALO + ts]
            if s == 0:
                prev = jnp.where(i == 0, 0.0, prev)
            if s == n_sub - 1:
                nxt = jnp.where(i == nt - 1, 0.0, nxt)
            return acc_ext[HALO:HALO + ts], prev, nxt

        def store(col, val):
            o_ref[0, rows, col * D_MODEL + half * INPROJ_TN:col * D_MODEL + (half + 1) * INPROJ_TN] = val.astype(BF16)

        if kind == "q":
            for hh in range(INPROJ_TN // HEAD_DIM):
                qh = _norm_rope(acc[:, hh * HEAD_DIM:(hh + 1) * HEAD_DIM], qg_ref[...], cos_t, sin_t, lane_lo)
                qt_ref[0, half * (INPROJ_TN // HEAD_DIM) + hh, :, rows] = (qh * q_scale).T.astype(BF16)
        elif kind == "kv":
            for hh in range(N_KV):
                kh = acc[:, hh * HEAD_DIM:(hh + 1) * HEAD_DIM]
                k_ref[0, hh, 0, rows, :] = _norm_rope(kh, kg_ref[...], cos_t, sin_t, lane_lo).astype(BF16)
                vt_ref[0, hh, 0, :, rows] = acc[:, KV_W + hh * HEAD_DIM:KV_W + (hh + 1) * HEAD_DIM].T.astype(BF16)
        elif kind == "xa":
            tile, prev, nxt = with_halos(acc)
            cw = caw_ref[:, lanes]
            store(COL_UA, cw[0:1] * _shift_rows(tile, prev, 2) + cw[1:2] * _shift_rows(tile, prev, 1)
                  + cw[2:3] * tile + cw[3:4] * _shift_rows(tile, nxt, -1) + cab_ref[:, lanes])
        elif kind == "xc":
            held["xc"] = acc
        elif kind == "cc":
            tile, prev, nxt = with_halos(acc * held.pop("xc"))
            cw = ccw_ref[:, lanes]
            held["conv"] = cw[0:1] * _shift_rows(tile, prev, 1) + cw[1:2] * tile + cw[2:3] * _shift_rows(tile, nxt, -1)
        elif kind == "bc":
            held["conv"] = acc * held["conv"]
        elif kind == "gc":
            store(COL_YC, held.pop("conv") * _silu(acc))
        elif kind == "ga":
            store(COL_GA, _silu(acc))
        elif kind == "gb":
            store(COL_GB, _silu(acc))
        else:
            o_ref[0, rows, COL_MG * D_MODEL + half * INPROJ_TN:COL_MG * D_MODEL + (half + 1) * INPROJ_TN] = (
                _sigmoid(acc).astype(BF16))


def _layer_spec(arr, l):
    nd = arr.ndim - 1
    assert nd >= 2
    return pl.BlockSpec((None,) + arr.shape[1:], lambda *_: (l,) + (0,) * nd)


def _mod_spec(mods, l):
    return pl.BlockSpec((None, 1) + mods.shape[2:], lambda b, i: (l, b, 0, 0))


def _in_projection(x, mods, l, norm_g, w_in, tables, q_g, k_g, conv_a_w, conv_a_b, conv_c_w, tk):
    bsz, s, d = x.shape
    tm = min(2 * INPROJ_ROWS, s)
    assert INPROJ_TN == 2 * KV_W and D_MODEL == 2 * INPROJ_TN and tm % GRID_W == 0 and tk % tm == 0
    assert tm % min(INPROJ_ROWS, tm) == 0
    nt = s // tm
    nkb = s // tk
    sub = tk // tm
    hbk = tm // HALO
    rcos, rsin, ccos, csin = tables
    row_spec = pl.BlockSpec((tm // GRID_W, 8, HEAD_DIM), lambda b, i: (i, 0, 0))
    col_spec = pl.BlockSpec((GRID_W, HEAD_DIM), lambda b, i: (0, 0))
    return pl.pallas_call(
        functools.partial(_inproj_kernel, nt=nt),
        grid=(bsz, nt),
        in_specs=[
            pl.BlockSpec((1, tm, d), lambda b, i: (b, i, 0)),
            pl.BlockSpec((1, HALO, d), lambda b, i: (b, jnp.maximum(i * hbk - 1, 0), 0)),
            pl.BlockSpec((1, HALO, d), lambda b, i: (b, jnp.minimum((i + 1) * hbk, s // HALO - 1), 0)),
            _mod_spec(mods, l),
            _layer_spec(norm_g, l),
            pl.BlockSpec((None, d, N_IN), lambda b, i: (l, 0, 0), pipeline_mode=pl.Buffered(1)),
            row_spec, row_spec, col_spec, col_spec, _layer_spec(q_g, l), _layer_spec(k_g, l),
            _layer_spec(conv_a_w, l), _layer_spec(conv_a_b, l), _layer_spec(conv_c_w, l),
        ],
        out_specs=[
            pl.BlockSpec((1, tm, N_STORED), lambda b, i: (b, i, 0)),
            pl.BlockSpec((1, N_HEADS, HEAD_DIM, tm), lambda b, i: (b, 0, 0, i)),
            pl.BlockSpec((1, N_KV, 1, tm, HEAD_DIM), lambda b, i: (b, 0, i // sub, i % sub, 0)),
            pl.BlockSpec((1, N_KV, 1, HEAD_DIM, tm), lambda b, i: (b, 0, i // sub, 0, i % sub)),
        ],
        out_shape=[
            jax.ShapeDtypeStruct((bsz, s, N_STORED), BF16),
            jax.ShapeDtypeStruct((bsz, N_HEADS, HEAD_DIM, s), BF16),
            jax.ShapeDtypeStruct((bsz, N_KV, nkb, tk, HEAD_DIM), BF16),
            jax.ShapeDtypeStruct((bsz, N_KV, nkb, HEAD_DIM, tk), BF16),
        ],
        compiler_params=_cparams(("parallel", "parallel")),
        name="norm_inproj",
    )(x, x, x, mods, norm_g, w_in, rcos, rsin, ccos, csin, q_g, k_g, conv_a_w, conv_a_b, conv_c_w)


SCORE_BOUND_NOSHIFT = 40.0


def _score_bound(q_g, k_g):
    q_scale = (HEAD_DIM ** -0.5) * LOG2E
    return HEAD_DIM * q_scale * jnp.max(jnp.abs(q_g), axis=-1) * jnp.max(jnp.abs(k_g), axis=-1)


KV_BLOCKS_PER_TRIP = 16


def _largest_divisor(n, cap):
    return max(d for d in range(1, cap + 1) if n % d == 0)


def _attn_kernel(flag_ref, qt_ref, qtn_ref, k_ref, vt_ref, gb_ref, o_ref, m_ref, l_ref, acc_ref, s_ref, q0_ref,
                 *, nkb, layer):
    tq = acc_ref.shape[-1]
    l_ref[...] = jnp.zeros(l_ref.shape, F32)
    acc_ref[...] = jnp.zeros(acc_ref.shape, F32)
    bounded = flag_ref[layer] != 0

    @pl.when(bounded)
    def _():
        def scores(j, h):
            return jnp.dot(k_ref[0, 0, j], qt_ref[0, h], preferred_element_type=F32)

        @pl.when(pl.program_id(2) == 0)
        def _():
            s_ref[...] = scores(0, 0)

        q0_ref[0] = qt_ref[0, 0]
        q0_ref[1] = qtn_ref[0, 0]

        def first_head_scores(j_next):
            wrap = jnp.asarray(j_next >= nkb)
            return jnp.dot(k_ref[0, 0, jnp.where(wrap, 0, j_next)], q0_ref[wrap.astype(jnp.int32)],
                           preferred_element_type=F32)

        group = _largest_divisor(nkb, KV_BLOCKS_PER_TRIP)

        def kv_group(g, carry):
            s_cur = s_ref[...]
            for u in range(group):
                j = g * group + u
                vblk = vt_ref[0, 0, j]
                for h in range(GROUP):
                    s_next = scores(j, h + 1) if h + 1 < GROUP else first_head_scores(j + 1)
                    p = jnp.exp2(s_cur)
                    l_ref[h] += jnp.sum(p.reshape(-1, SUBLANES, tq), axis=0)
                    acc_ref[h] += jnp.dot(vblk, p.astype(BF16), preferred_element_type=F32)
                    s_cur = s_next
            s_ref[...] = s_cur
            return carry

        lax.fori_loop(0, nkb // group, kv_group, 0)

    @pl.when(jnp.logical_not(bounded))
    def _():
        m_ref[...] = jnp.full(m_ref.shape, -jnp.inf, F32)

        def kv_step(j, carry):
            kblk = k_ref[0, 0, j]
            vblk = vt_ref[0, 0, j]
            for h in range(GROUP):
                s = jnp.dot(kblk, qt_ref[0, h], preferred_element_type=F32)
                m_old = m_ref[h]
                m_new = jnp.maximum(m_old, jnp.max(s, axis=0, keepdims=True))
                alpha = jnp.exp2(m_old - m_new)
                p = jnp.exp2(s - m_new)
                l_ref[h] = alpha * l_ref[h] + jnp.sum(p.reshape(-1, SUBLANES, tq), axis=0)
                acc_ref[h] = alpha * acc_ref[h] + jnp.dot(vblk, p.astype(BF16), preferred_element_type=F32)
                m_ref[h] = m_new
            return carry

        lax.fori_loop(0, nkb, kv_step, 0)

    for h in range(GROUP):
        l = jnp.sum(l_ref[h], axis=0, keepdims=True)
        o = acc_ref[h] * (1.0 / l)
        gate = gb_ref[0, :, h * HEAD_DIM:(h + 1) * HEAD_DIM].astype(F32)
        o_ref[0, :, h * HEAD_DIM:(h + 1) * HEAD_DIM] = (o.T * gate).astype(BF16)


def _attention(qt, kk, vt, proj, bounded_flags, layer, tq):
    bsz, _, _, s = qt.shape
    nkb, tk = kk.shape[2], kk.shape[3]
    gw = GROUP * HEAD_DIM
    grid_spec = pltpu.PrefetchScalarGridSpec(
        num_scalar_prefetch=1,
        grid=(bsz, N_KV, s // tq),
        in_specs=[
            pl.BlockSpec((1, GROUP, HEAD_DIM, tq), lambda b, g, i, f: (b, g, 0, i)),
            pl.BlockSpec((1, 1, HEAD_DIM, tq), lambda b, g, i, f: (b, g * GROUP, 0, jnp.minimum(i + 1, s // tq - 1))),
            pl.BlockSpec((1, 1, nkb, tk, HEAD_DIM), lambda b, g, i, f: (b, g, 0, 0, 0)),
            pl.BlockSpec((1, 1, nkb, HEAD_DIM, tk), lambda b, g, i, f: (b, g, 0, 0, 0)),
            pl.BlockSpec((1, tq, gw), lambda b, g, i, f: (b, i, COL_GB * D_MODEL // gw + g)),
        ],
        out_specs=pl.BlockSpec((1, tq, gw), lambda b, g, i, f: (b, i, g)),
        scratch_shapes=[
            pltpu.VMEM((GROUP, 1, tq), F32),
            pltpu.VMEM((GROUP, SUBLANES, tq), F32),
            pltpu.VMEM((GROUP, HEAD_DIM, tq), F32),
            pltpu.VMEM((tk, tq), F32),
            pltpu.VMEM((2, HEAD_DIM, tq), BF16),
        ],
    )
    return pl.pallas_call(
        functools.partial(_attn_kernel, nkb=nkb, layer=layer),
        grid_spec=grid_spec,
        out_shape=jax.ShapeDtypeStruct((bsz, s, D_MODEL), BF16),
        compiler_params=_cparams(("parallel", "parallel", "arbitrary")),
        name="gqa_attention",
    )(bounded_flags, qt, qt, kk, vt, proj)


LRU_BWD = 2


def _gate_dot(ub, w_ref, d, n, pre_ref, slot):
    cols = slice(n * RNN_BW, (n + 1) * RNN_BW)
    pre_ref[slot, :, cols] = jnp.dot(ub[:, cols], w_ref[d, n], preferred_element_type=F32)


def _lru_coefficients(ua, r_half, i_half, b_r_half, b_i_half, c_half):
    y = c_half + c_half * jnp.tanh(r_half + b_r_half)
    ua_half = 0.5 * ua
    gated = ua_half + ua_half * jnp.tanh(i_half + b_i_half)
    a = jnp.exp2(y * (-LOG2E))
    x = jnp.tanh(y) * (1.0 + a * a)
    root = jnp.where(x > 0.0, x * lax.rsqrt(x), x)
    return a, root * gated


N_SLABS = D_MODEL // LANES
assert N_SLABS == SUBLANES


def _slab_pitch(t):
    return t + SUBLANES if (t // SUBLANES) % 2 == 0 else t


def _lru_kernel(xf_ref, xn_ref, xb_ref, wr_ref, br_ref, wi_ref, bi_ref, lam_ref, hf_ref, hb_ref, cf_ref, cbk_ref,
                sf_ref, sb_ref, ua_ref, rpre_ref, ipre_ref, chan_ref, *coef_refs, nt):
    i = pl.program_id(1)
    tm = hf_ref.shape[1]
    pitch = _slab_pitch(tm)
    coef_sets = (coef_refs[:4], coef_refs[4:])
    gates = ((wr_ref, rpre_ref), (wi_ref, ipre_ref))
    gate_dots = [(n, g) for n in range(RNN_BLOCKS) for g in range(len(gates))]

    @pl.when(i == 0)
    def _():
        cf_ref[...] = jnp.zeros(cf_ref.shape, F32)
        cbk_ref[...] = jnp.zeros(cbk_ref.shape, F32)
        for ref in coef_sets[1]:
            ref[...] = jnp.zeros(ref.shape, F32)
        ub = xf_ref[0]
        ua_ref[0] = ub.astype(F32)
        for n, g in gate_dots:
            _gate_dot(ub, gates[g][0], 0, n, gates[g][1], 0)

    ua_ref[LRU_BWD] = xb_ref[0].astype(F32)
    nl = -lam_ref[...]
    c_half = (0.5 * LRU_C) * (jnp.maximum(nl, 0.0) + jnp.log1p(jnp.exp(-jnp.abs(nl))))
    for d in range(2):
        for n, row in enumerate((0.5 * br_ref[d:d + 1, :], 0.5 * bi_ref[d:d + 1, :], c_half[d:d + 1, :])):
            chan_ref[3 * d + n] = jnp.broadcast_to(row, chan_ref.shape[1:])

    def step(cur, prev, parity):
        af_ref, uf_ref, ab_ref, ub_ref = prev

        hf, hb = cf_ref[...], cbk_ref[...]
        n_it = tm // SUBLANES
        chain_steps = tm // (2 * n_it)
        ub_bwd = xb_ref[0]
        ub_next = xn_ref[0]
        dot_every = n_it // len(gate_dots)
        slot_of = (parity, LRU_BWD)
        k = 0
        for it in range(2 * n_it):
            d, r0 = divmod(it, n_it)
            r0 *= SUBLANES
            rows = slice(r0, r0 + SUBLANES)
            src = slot_of[d]
            a, u = _lru_coefficients(ua_ref[src, rows, :], rpre_ref[src, rows, :], ipre_ref[src, rows, :],
                                     chan_ref[3 * d], chan_ref[3 * d + 1], chan_ref[3 * d + 2])
            for c in range(N_SLABS):
                dst = pl.ds(c * pitch + r0, SUBLANES)
                cur[2 * d][dst, :] = a[:, c * LANES:(c + 1) * LANES]
                cur[2 * d + 1][dst, :] = u[:, c * LANES:(c + 1) * LANES]
            for _ in range(chain_steps):
                tf = pl.ds(k, N_SLABS, stride=pitch)
                tb = pl.ds(tm - 1 - k, N_SLABS, stride=pitch)
                hf = af_ref[tf, :] * hf + uf_ref[tf, :]
                sf_ref[tf, :] = hf
                hb = ab_ref[tb, :] * hb + ub_ref[tb, :]
                sb_ref[tb, :] = hb
                k += 1
            if it % dot_every == 0:
                n, g = gate_dots[(it % n_it) // dot_every]
                if d == 0:
                    _gate_dot(ub_bwd, gates[g][0], 1, n, gates[g][1], LRU_BWD)
                else:
                    _gate_dot(ub_next, gates[g][0], 0, n, gates[g][1], 1 - parity)
            if it == n_it:
                ua_ref[1 - parity] = ub_next.astype(F32)
        assert k == tm
        keep = i > 0
        cf_ref[...] = jnp.where(keep, hf, cf_ref[...])
        cbk_ref[...] = jnp.where(keep, hb, cbk_ref[...])
        for s_ref, out_ref in ((sf_ref, hf_ref), (sb_ref, hb_ref)):
            for c in range(N_SLABS):
                out_ref[0, :, c * LANES:(c + 1) * LANES] = s_ref[pl.ds(c * pitch, tm), :].astype(out_ref.dtype)

    @pl.when(i % 2 == 0)
    def _():
        step(coef_sets[0], coef_sets[1], 0)

    @pl.when(i % 2 == 1)
    def _():
        step(coef_sets[1], coef_sets[0], 1)


def _rg_lru(proj, l, w_rg, b_rg, w_ig, b_ig, lam):
    bsz, s, _ = proj.shape
    d = D_MODEL
    tm = min(256, s)
    nt = s // tm
    out_sds = jax.ShapeDtypeStruct((bsz, s, d), BF16)
    return pl.pallas_call(
        functools.partial(_lru_kernel, nt=nt),
        grid=(bsz, nt + 1),
        in_specs=[
            pl.BlockSpec((1, tm, d), lambda b, i: (b, jnp.minimum(i, nt - 1), COL_UA)),
            pl.BlockSpec((1, tm, d), lambda b, i: (b, jnp.minimum(i + 1, nt - 1), COL_UA)),
            pl.BlockSpec((1, tm, d), lambda b, i: (b, nt - 1 - jnp.minimum(i, nt - 1), COL_UA)),
            _layer_spec(w_rg, l), _layer_spec(b_rg, l), _layer_spec(w_ig, l), _layer_spec(b_ig, l),
            _layer_spec(lam, l),
        ],
        out_specs=[
            pl.BlockSpec((1, tm, d), lambda b, i: (b, jnp.maximum(i - 1, 0), 0)),
            pl.BlockSpec((1, tm, d), lambda b, i: (b, nt - 1 - jnp.maximum(i - 1, 0), 0)),
        ],
        out_shape=[out_sds, out_sds],
        scratch_shapes=[pltpu.VMEM((N_SLABS, LANES), F32)] * 2
        + [pltpu.VMEM((N_SLABS * _slab_pitch(tm), LANES), F32)] * 2
        + [pltpu.VMEM((LRU_BWD + 1, tm, d), F32)] * 3
        + [pltpu.VMEM((6, 8, d), F32)]
        + [pltpu.VMEM((N_SLABS * _slab_pitch(tm), LANES), F32)] * 8,
        compiler_params=_cparams(("parallel", "arbitrary")),
        name="rg_lru",
    )(proj, proj, proj, w_rg, b_rg, w_ig, b_ig, lam)


def _merge_kernel(hf_ref, hb_ref, ga_ref, yb_ref, yc_ref, mg0_ref, mg1_ref, mg2_ref, x_ref, mod_ref, wb_ref,
                  wo_ref, o_ref):
    merged = mg1_ref[0].astype(F32) * jnp.dot(yb_ref[0], wb_ref[1], preferred_element_type=F32)
    merged += mg2_ref[0].astype(F32) * jnp.dot(yc_ref[0], wb_ref[2], preferred_element_type=F32)
    ya = (hf_ref[0].astype(F32) + hb_ref[0].astype(F32)) * ga_ref[0].astype(F32)
    merged += mg0_ref[0].astype(F32) * jnp.dot(ya.astype(BF16), wb_ref[0], preferred_element_type=F32)
    out = jnp.dot(merged.astype(BF16), wo_ref[...], preferred_element_type=F32)
    o_ref[0] = x_ref[0] + mod_ref[0, 2:3, :] * out


def _merge(hf, hb, yb, proj, x, mods, l, w_branch, w_out):
    bsz, s, d = x.shape
    tm = min(512, s)

    def col(c):
        return pl.BlockSpec((1, tm, d), lambda b, i: (b, i, c))

    return pl.pallas_call(
        _merge_kernel,
        grid=(bsz, s // tm),
        in_specs=[
            col(0), col(0), col(COL_GA), col(0), col(COL_YC), col(COL_MG), col(COL_MG + 1), col(COL_MG + 2),
            col(0),
            _mod_spec(mods, l), _layer_spec(w_branch, l), _layer_spec(w_out, l),
        ],
        out_specs=pl.BlockSpec((1, tm, d), lambda b, i: (b, i, 0)),
        out_shape=jax.ShapeDtypeStruct((bsz, s, d), F32),
        compiler_params=_cparams(("parallel", "parallel")),
        name="merge_outproj",
    )(hf, hb, proj, yb, proj, proj, proj, proj, x, mods, w_branch, w_out)


def _attn_tiles(s):
    tq = min(512, s)
    tk = min(512, s)
    return tq, tk


def kernel(x, c, w_ada, b_ada, norm_g, w_in, conv_a_w, conv_a_b, w_rg, b_rg, w_ig, b_ig, lru_lam, q_norm_g,
           k_norm_g, conv_c_w, w_branch, w_out):
    bsz, s, d = x.shape
    depth = w_ada.shape[0]
    tq, tk = _attn_tiles(s)
    tables = _rope_tables(s)
    mods = _modulation(c, w_ada, b_ada)
    w_in_bf = w_in.astype(BF16)
    w_rg_half = (0.5 * w_rg).astype(BF16)
    w_ig_half = (0.5 * w_ig).astype(BF16)
    w_branch_bf = w_branch.astype(BF16)
    w_out_bf = w_out.astype(BF16)
    bounded = (_score_bound(q_norm_g, k_norm_g) <= SCORE_BOUND_NOSHIFT).astype(jnp.int32)
    rows = lambda p: p.reshape(depth, 1, p.shape[-1])
    norm_g, conv_a_b, q_norm_g, k_norm_g = rows(norm_g), rows(conv_a_b), rows(q_norm_g), rows(k_norm_g)
    for l in range(depth):
        proj, qt, kk, vt = _in_projection(x, mods, l, norm_g, w_in_bf, tables, q_norm_g, k_norm_g, conv_a_w,
                                          conv_a_b, conv_c_w, tk)
        hf, hb = _rg_lru(proj, l, w_rg_half, b_rg, w_ig_half, b_ig, lru_lam)
        yb = _attention(qt, kk, vt, proj, bounded, l, tq)
        x = _merge(hf, hb, yb, proj, x, mods, l, w_branch_bf, w_out_bf)
    return x
```
